```python
import math
import jax, jax.numpy as jnp
from jax import lax
import numpy as np

D_MODEL = 1024
BATCH = 8
SEQ = 2048
DEPTH = 2
DEC_BATCH = 128
DEC_SEQ = 1
PAST_LEN = 16384
PAGE_SIZE = 128

D_CONV = D_MODEL // 2
DW_WIDTH = 31
N_HEADS = 4
HEAD_K = 128
HEAD_V = 128
QK_CONV = 4
CHUNK = 64
D_DELTA_K = N_HEADS * HEAD_K
D_DELTA_V = N_HEADS * HEAD_V
CONV_DIM = 2 * D_DELTA_K + D_DELTA_V
D_FF = ((8 * D_MODEL // 3 + 127) // 128) * 128
N_EXPERTS = 8
TOP_K = 2
D_FF_EXPERT = D_FF
N_DENSE = (DEPTH + 1) // 2
N_MOE = DEPTH // 2
EPS = 1e-6

OFF_GLU = 0
OFF_QKV = OFF_GLU + 2 * D_CONV
OFF_Z = OFF_QKV + CONV_DIM
OFF_A = OFF_Z + D_DELTA_V
OFF_B = OFF_A + N_HEADS
OFF_GATE_A = OFF_B + N_HEADS
OFF_GATE_B = OFF_GATE_A + D_MODEL
N_IN = OFF_GATE_B + D_MODEL

kernel_name = 'conformer_gated_deltanet_hybrid_step'


def _rmsnorm(x, g):
    xf = x.astype(jnp.float32)
    y = xf * lax.rsqrt(jnp.mean(xf * xf, axis=-1, keepdims=True) + EPS)
    return (y * g.astype(jnp.float32)).astype(x.dtype)


def _l2norm(t):
    return t * lax.rsqrt(jnp.sum(t * t, axis=-1, keepdims=True) + EPS)


def _causal_depthwise_conv(x, buf, w):
    xp = jnp.concatenate([buf.astype(x.dtype), x], axis=1)
    y = lax.conv_general_dilated(
        xp, w[:, None, :].astype(x.dtype), window_strides=(1,), padding='VALID',
        dimension_numbers=('NWC', 'WIO', 'NWC'), feature_group_count=x.shape[-1])
    new_buf = xp[:, xp.shape[1] - (w.shape[0] - 1):]
    return y, new_buf


def _gated_delta_rule(q, k, v, beta, g, S0):
    Bn, T, H, _ = q.shape
    C = min(CHUNK, T)
    pad = (-T) % C
    n = (T + pad) // C

    def blocks(t):
        t = jnp.pad(t, [(0, 0), (0, pad)] + [(0, 0)] * (t.ndim - 2))
        t = t.reshape((Bn, n, C) + t.shape[2:])
        return jnp.moveaxis(t, 3, 1)

    q, k, v, beta, g = blocks(q), blocks(k), blocks(v), blocks(beta), blocks(g)
    gc = jnp.cumsum(g, axis=-1)
    idx = jnp.arange(C)
    tril = idx[:, None] >= idx[None, :]
    strict = idx[:, None] > idx[None, :]
    diff = gc[..., :, None] - gc[..., None, :]
    L = jnp.where(tril, jnp.exp(jnp.where(tril, diff, 0.0)), 0.0)
    kb = k * beta[..., None]
    M = jnp.where(strict, jnp.einsum('bhnid,bhnjd->bhnij', kb, k) * L, 0.0)
    eye = jnp.eye(C, dtype=jnp.float32)
    Tm = lax.linalg.triangular_solve(eye + M, jnp.broadcast_to(eye, M.shape),
                                     left_side=True, lower=True, unit_diagonal=True)
    u = jnp.einsum('bhnij,bhnjv->bhniv', Tm, v * beta[..., None])
    w = jnp.einsum('bhnij,bhnjk->bhnik', Tm, kb * jnp.exp(gc)[..., None])
    a_intra = jnp.einsum('bhnik,bhnjk->bhnij', q, k) * L
    qg = q * jnp.exp(gc)[..., None]
    g_last = gc[..., -1]
    kd = k * jnp.exp(g_last[..., None] - gc)[..., None]
    dl = jnp.exp(g_last)
    xs = (jnp.moveaxis(u, 2, 0), jnp.moveaxis(w, 2, 0), jnp.moveaxis(a_intra, 2, 0),
          jnp.moveaxis(qg, 2, 0), jnp.moveaxis(kd, 2, 0), jnp.moveaxis(dl, 2, 0))

    def step(S, inp):
        u_c, w_c, a_c, qg_c, kd_c, dl_c = inp
        v_new = u_c - jnp.einsum('bhik,bhkv->bhiv', w_c, S)
        o = jnp.einsum('bhik,bhkv->bhiv', qg_c, S) + jnp.einsum('bhij,bhjv->bhiv', a_c, v_new)
        S = S * dl_c[..., None, None] + jnp.einsum('bhik,bhiv->bhkv', kd_c, v_new)
        return S, o

    S, o = lax.scan(step, S0, xs)
    o = jnp.transpose(o, (1, 0, 3, 2, 4)).reshape(Bn, n * C, H, o.shape[-1])[:, :T]
    return o, S


def _conformer_conv(glu, buf, dw_w, dw_b, ln_g, ln_b, w_out):
    u = glu[..., :D_CONV] * jax.nn.sigmoid(glu[..., D_CONV:])
    c, new_buf = _causal_depthwise_conv(u, buf, dw_w)
    c = (c + dw_b).astype(jnp.float32)
    mu = jnp.mean(c, axis=-1, keepdims=True)
    var = jnp.mean(jnp.square(c - mu), axis=-1, keepdims=True)
    c = (c - mu) * lax.rsqrt(var + EPS) * ln_g.astype(jnp.float32) + ln_b.astype(jnp.float32)
    c = jax.nn.silu(c).astype(glu.dtype)
    return c @ w_out, new_buf


def _gated_deltanet(qkv, z, a, b, conv_buf, S0, conv_w, a_log, dt_bias, o_norm_g, w_out):
    Bn, T, _ = qkv.shape
    f32 = jnp.float32
    c, new_buf = _causal_depthwise_conv(qkv, conv_buf, conv_w)
    c = jax.nn.silu(c.astype(f32))
    q = c[..., :D_DELTA_K].reshape(Bn, T, N_HEADS, HEAD_K)
    k = c[..., D_DELTA_K:2 * D_DELTA_K].reshape(Bn, T, N_HEADS, HEAD_K)
    v = c[..., 2 * D_DELTA_K:].reshape(Bn, T, N_HEADS, HEAD_V)
    q = _l2norm(q) * (HEAD_K ** -0.5)
    k = _l2norm(k)
    beta = jax.nn.sigmoid(b.astype(f32))
    g = -jnp.exp(a_log.astype(f32)) * jax.nn.softplus(a.astype(f32) + dt_bias.astype(f32))
    o, S = _gated_delta_rule(q, k, v, beta, g, S0.astype(f32))
    o = o * lax.rsqrt(jnp.mean(o * o, axis=-1, keepdims=True) + EPS) * o_norm_g.astype(f32)
    o = o * jax.nn.silu(z.astype(f32).reshape(Bn, T, N_HEADS, HEAD_V))
    y = o.reshape(Bn, T, D_DELTA_V).astype(qkv.dtype) @ w_out
    return y, new_buf, S.astype(S0.dtype)


def _swiglu(h, wg, wu, wd):
    return (jax.nn.silu(h @ wg) * (h @ wu)) @ wd


def _moe(h, router, wg, wu, wd):
    logits = jnp.einsum('btd,de->bte', h.astype(jnp.float32), router.astype(jnp.float32))
    top_v, top_i = lax.top_k(logits, TOP_K)
    probs = jax.nn.softmax(top_v, axis=-1)
    gates = jnp.sum(jax.nn.one_hot(top_i, N_EXPERTS, dtype=jnp.float32) * probs[..., None], axis=-2)
    y = jnp.zeros_like(h)
    for e in range(N_EXPERTS):
        y = y + gates[..., e:e + 1].astype(h.dtype) * _swiglu(h, wg[e], wu[e], wd[e])
    return y


def _trunk(x, S0, qkv0, dw0, P):
    news_S, news_q, news_d = [], [], []
    for l in range(DEPTH):
        h = _rmsnorm(x, P['norm1_g'][l])
        p = h @ P['w_in'][l]
        yA, dbuf = _conformer_conv(p[..., OFF_GLU:OFF_QKV], dw0[l], P['dw_w'][l], P['dw_b'][l],
                                   P['ln_g'][l], P['ln_b'][l], P['w_conv_out'][l])
        yB, qbuf, S = _gated_deltanet(p[..., OFF_QKV:OFF_Z], p[..., OFF_Z:OFF_A], p[..., OFF_A:OFF_B],
                                      p[..., OFF_B:OFF_GATE_A], qkv0[l], S0[l], P['qkv_conv_w'][l],
                                      P['a_log'][l], P['dt_bias'][l], P['o_norm_g'][l], P['w_delta_out'][l])
        merged = (jax.nn.sigmoid(p[..., OFF_GATE_A:OFF_GATE_B]) * yA
                  + jax.nn.sigmoid(p[..., OFF_GATE_B:N_IN]) * yB)
        x = x + merged @ P['w_o'][l]
        h2 = _rmsnorm(x, P['norm2_g'][l])
        i = l // 2
        if l % 2 == 0:
            x = x + _swiglu(h2, P['ffn_w_gate'][i], P['ffn_w_up'][i], P['ffn_w_down'][i])
        else:
            x = x + _moe(h2, P['router_w'][i], P['exp_w_gate'][i], P['exp_w_up'][i], P['exp_w_down'][i])
        news_S.append(S)
        news_q.append(qbuf)
        news_d.append(dbuf)
    y = _rmsnorm(x, P['final_norm_g'])
    return y, jnp.stack(news_S), jnp.stack(news_q), jnp.stack(news_d)


def setup_inputs(seed: int = 0) -> dict:
    key = jax.random.key(seed)
    ks = iter(jax.random.split(key, 40))

    def nrm(shape, scale):
        return scale * jax.random.normal(next(ks), shape, jnp.float32)

    x_prompt = nrm((BATCH, SEQ, D_MODEL), 1.0)
    x_sample = nrm((DEC_BATCH, DEC_SEQ, D_MODEL), 1.0)
    state_delta = nrm((DEPTH, DEC_BATCH, N_HEADS, HEAD_K, HEAD_V), 0.1)
    state_qkv_conv = nrm((DEPTH, DEC_BATCH, QK_CONV - 1, CONV_DIM), 1.0)
    state_dwconv = nrm((DEPTH, DEC_BATCH, DW_WIDTH - 1, D_CONV), 0.5)
    norm1_g = 1.0 + nrm((DEPTH, D_MODEL), 0.02)
    w_in = nrm((DEPTH, D_MODEL, N_IN), D_MODEL ** -0.5)
    qkv_conv_w = nrm((DEPTH, QK_CONV, CONV_DIM), QK_CONV ** -0.5)
    a_log = jnp.log(jax.random.uniform(next(ks), (DEPTH, N_HEADS), jnp.float32, 1.0, 16.0))
    dt = jnp.exp(jax.random.uniform(next(ks), (DEPTH, N_HEADS), jnp.float32,
                                    math.log(1e-3), math.log(1e-1)))
    dt_bias = dt + jnp.log(-jnp.expm1(-dt))
    o_norm_g = 1.0 + nrm((DEPTH, HEAD_V), 0.02)
    w_delta_out = nrm((DEPTH, D_DELTA_V, D_MODEL), D_DELTA_V ** -0.5)
    dw_w = nrm((DEPTH, DW_WIDTH, D_CONV), DW_WIDTH ** -0.5)
    dw_b = nrm((DEPTH, D_CONV), 0.02)
    ln_g = 1.0 + nrm((DEPTH, D_CONV), 0.02)
    ln_b = nrm((DEPTH, D_CONV), 0.02)
    w_conv_out = nrm((DEPTH, D_CONV, D_MODEL), D_CONV ** -0.5)
    w_o = nrm((DEPTH, D_MODEL, D_MODEL), D_MODEL ** -0.5)
    norm2_g = 1.0 + nrm((DEPTH, D_MODEL), 0.02)
    ffn_w_gate = nrm((N_DENSE, D_MODEL, D_FF), D_MODEL ** -0.5)
    ffn_w_up = nrm((N_DENSE, D_MODEL, D_FF), D_MODEL ** -0.5)
    ffn_w_down = nrm((N_DENSE, D_FF, D_MODEL), D_FF ** -0.5)
    router_w = nrm((N_MOE, D_MODEL, N_EXPERTS), D_MODEL ** -0.5)
    exp_w_gate = nrm((N_MOE, N_EXPERTS, D_MODEL, D_FF_EXPERT), D_MODEL ** -0.5)
    exp_w_up = nrm((N_MOE, N_EXPERTS, D_MODEL, D_FF_EXPERT), D_MODEL ** -0.5)
    exp_w_down = nrm((N_MOE, N_EXPERTS, D_FF_EXPERT, D_MODEL), D_FF_EXPERT ** -0.5)
    final_norm_g = 1.0 + nrm((D_MODEL,), 0.02)
    return {'x_prompt': x_prompt, 'x_sample': x_sample, 'state_delta': state_delta,
            'state_qkv_conv': state_qkv_conv, 'state_dwconv': state_dwconv,
            'norm1_g': norm1_g, 'w_in': w_in, 'qkv_conv_w': qkv_conv_w, 'a_log': a_log,
            'dt_bias': dt_bias, 'o_norm_g': o_norm_g, 'w_delta_out': w_delta_out,
            'dw_w': dw_w, 'dw_b': dw_b, 'ln_g': ln_g, 'ln_b': ln_b, 'w_conv_out': w_conv_out,
            'w_o': w_o, 'norm2_g': norm2_g, 'ffn_w_gate': ffn_w_gate, 'ffn_w_up': ffn_w_up,
            'ffn_w_down': ffn_w_down, 'router_w': router_w, 'exp_w_gate': exp_w_gate,
            'exp_w_up': exp_w_up, 'exp_w_down': exp_w_down, 'final_norm_g': final_norm_g}


def reference(x_prompt, x_sample, state_delta, state_qkv_conv, state_dwconv,
              norm1_g, w_in, qkv_conv_w, a_log, dt_bias, o_norm_g, w_delta_out,
              dw_w, dw_b, ln_g, ln_b, w_conv_out, w_o, norm2_g,
              ffn_w_gate, ffn_w_up, ffn_w_down, router_w, exp_w_gate, exp_w_up, exp_w_down,
              final_norm_g):
    P = dict(norm1_g=norm1_g, w_in=w_in, qkv_conv_w=qkv_conv_w, a_log=a_log, dt_bias=dt_bias,
             o_norm_g=o_norm_g, w_delta_out=w_delta_out, dw_w=dw_w, dw_b=dw_b, ln_g=ln_g,
             ln_b=ln_b, w_conv_out=w_conv_out, w_o=w_o, norm2_g=norm2_g,
             ffn_w_gate=ffn_w_gate, ffn_w_up=ffn_w_up, ffn_w_down=ffn_w_down,
             router_w=router_w, exp_w_gate=exp_w_gate, exp_w_up=exp_w_up,
             exp_w_down=exp_w_down, final_norm_g=final_norm_g)
    nb = x_prompt.shape[0]
    S0_p = jnp.zeros((DEPTH, nb, N_HEADS, HEAD_K, HEAD_V), state_delta.dtype)
    q0_p = jnp.zeros((DEPTH, nb, QK_CONV - 1, CONV_DIM), state_qkv_conv.dtype)
    d0_p = jnp.zeros((DEPTH, nb, DW_WIDTH - 1, D_CONV), state_dwconv.dtype)
    y_prompt, S_p, q_p, d_p = _trunk(x_prompt, S0_p, q0_p, d0_p, P)
    y_sample, S_s, q_s, d_s = _trunk(x_sample, state_delta, state_qkv_conv, state_dwconv, P)
    return (y_prompt, y_sample, S_p, q_p, d_p, S_s, q_s, d_s)
```

```python
import functools

import jax
import jax.numpy as jnp
from jax import lax
from jax.experimental import pallas as pl
from jax.experimental.pallas import tpu as pltpu

F32 = jnp.float32
BF16 = jnp.bfloat16
HIGHEST = lax.Precision.HIGHEST

D_MODEL = 1024
BATCH = 8
SEQ = 2048
DEPTH = 2
DEC_BATCH = 128
D_CONV = 512
DW_WIDTH = 31
N_HEADS = 4
HEAD_K = 128
HEAD_V = 128
QK_CONV = 4
D_DELTA_K = N_HEADS * HEAD_K
D_DELTA_V = N_HEADS * HEAD_V
CONV_DIM = 2 * D_DELTA_K + D_DELTA_V
D_FF = 2816
N_EXPERTS = 8
EPS = 1e-6

N_PROMPT = BATCH * SEQ
TOK_TILE = 512
SEQ_TILE = TOK_TILE
N_TOK = N_PROMPT + TOK_TILE
SEQ_TILES = SEQ // SEQ_TILE
PROMPT_TILES = N_PROMPT // TOK_TILE
CHUNK = 64
ROW_BLK = 64
FF_HALF = D_FF // 2
MOE_TILE = 256
MOE_ROWS = ((2 * N_TOK + N_EXPERTS * (MOE_TILE - 1)) // MOE_TILE + 1) * MOE_TILE
DEC_BLK = 8

C_GLU, C_QKV, C_Z, C_GA, C_GB, C_AB = 0, 1024, 2560, 3072, 4096, 5120
N_PROJ = 5248

VMEM_BIG = 56 * 1024 * 1024


def _bdot(a, b):
    return jnp.dot(a.astype(BF16), b.astype(BF16), preferred_element_type=F32)


def _hdot(a, b):
    return jnp.dot(a, b, preferred_element_type=F32, precision=HIGHEST)


def _rms(x, g):
    return x * lax.rsqrt(jnp.mean(x * x, axis=-1, keepdims=True) + EPS) * g


def _silu(x):
    return x * jax.nn.sigmoid(x)


def _resident(shape):
    return pl.BlockSpec(shape, lambda *_: (0,) * len(shape), pipeline_mode=pl.Buffered(1))


def _norm_proj_kernel(x_ref, g_ref, w_ref, glu_ref, qkv_ref, z_ref, ga_ref, gb_ref, ab_ref):
    h = _rms(x_ref[...], g_ref[...]).astype(BF16)

    def proj(lo, hi):
        return jnp.dot(h, w_ref[:, lo:hi], preferred_element_type=F32)

    glu_ref[...] = proj(C_GLU, C_QKV)
    qkv_ref[...] = proj(C_QKV, C_Z)
    z_ref[...] = proj(C_Z, C_GA)
    ga_ref[...] = proj(C_GA, C_GB)
    gb_ref[...] = proj(C_GB, C_AB)
    ab_ref[...] = proj(C_AB, N_PROJ)


def _norm_proj(x, g, w_all):
    row = lambda w: pl.BlockSpec((TOK_TILE, w), lambda i: (i, 0))
    widths = (1024, CONV_DIM, D_DELTA_V, D_MODEL, D_MODEL, 128)
    return pl.pallas_call(
        _norm_proj_kernel,
        grid=(N_TOK // TOK_TILE,),
        in_specs=[row(D_MODEL),
                  pl.BlockSpec((1, D_MODEL), lambda i: (0, 0)),
                  _resident((D_MODEL, N_PROJ))],
        out_specs=[row(w) for w in widths],
        out_shape=[jax.ShapeDtypeStruct((N_TOK, w), F32) for w in widths],
        compiler_params=pltpu.CompilerParams(
            dimension_semantics=("arbitrary",), vmem_limit_bytes=VMEM_BIG),
        name="norm_proj",
    )(x, g, w_all)


def _merge_kernel(x_ref, ya_ref, yb_ref, ga_ref, gb_ref, wo_ref, o_ref):
    m = jax.nn.sigmoid(ga_ref[...]) * ya_ref[...] + jax.nn.sigmoid(gb_ref[...]) * yb_ref[...]
    o_ref[...] = x_ref[...] + _bdot(m, wo_ref[...])


def _merge(x, ya, yb, ga, gb, wo):
    row = pl.BlockSpec((TOK_TILE, D_MODEL), lambda i: (i, 0))
    return pl.pallas_call(
        _merge_kernel,
        grid=(N_TOK // TOK_TILE,),
        in_specs=[row, row, row, row, row, pl.BlockSpec((D_MODEL, D_MODEL), lambda i: (0, 0))],
        out_specs=row,
        out_shape=jax.ShapeDtypeStruct((N_TOK, D_MODEL), F32),
        compiler_params=pltpu.CompilerParams(dimension_semantics=("arbitrary",)),
        name="merge",
    )(x, ya, yb, ga, gb, wo)


def _swiglu_rows(h, wg_ref, wu_ref, wd_ref):
    acc = None
    for lo in (0, FF_HALF):
        a = jnp.dot(h, wg_ref[:, lo:lo + FF_HALF], preferred_element_type=F32)
        b = jnp.dot(h, wu_ref[:, lo:lo + FF_HALF], preferred_element_type=F32)
        part = jnp.dot((_silu(a) * b).astype(BF16), wd_ref[lo:lo + FF_HALF, :],
                       preferred_element_type=F32)
        acc = part if acc is None else acc + part
    return acc


def _ffn_kernel(x_ref, g_ref, wg_ref, wu_ref, wd_ref, o_ref):
    x = x_ref[...]
    h = _rms(x, g_ref[...]).astype(BF16)
    o_ref[...] = x + _swiglu_rows(h, wg_ref, wu_ref, wd_ref)


def _ffn(x, g, wg, wu, wd):
    row = pl.BlockSpec((TOK_TILE, D_MODEL), lambda i: (i, 0))
    return pl.pallas_call(
        _ffn_kernel,
        grid=(N_TOK // TOK_TILE,),
        in_specs=[row, pl.BlockSpec((1, D_MODEL), lambda i: (0, 0)),
                  _resident((D_MODEL, D_FF)), _resident((D_MODEL, D_FF)),
                  _resident((D_FF, D_MODEL))],
        out_specs=row,
        out_shape=jax.ShapeDtypeStruct((N_TOK, D_MODEL), F32),
        compiler_params=pltpu.CompilerParams(
            dimension_semantics=("arbitrary",), vmem_limit_bytes=VMEM_BIG),
        name="ffn_dense",
    )(x, g, wg, wu, wd)


def _conv_tail(c, dwb, lng, lnb):
    c = c + dwb
    mu = jnp.mean(c, axis=-1, keepdims=True)
    var = jnp.mean(jnp.square(c - mu), axis=-1, keepdims=True)
    c = (c - mu) * lax.rsqrt(var + EPS) * lng + lnb
    return _silu(c)


def _sample_rows_tile(ys_ref, y_ref):
    y_ref[0:DEC_BATCH, :] = ys_ref[...]
    y_ref[DEC_BATCH:, :] = jnp.zeros((TOK_TILE - DEC_BATCH, D_MODEL), F32)


def _conv_prompt_kernel(glu_ref, ys_ref, dww_ref, dwb_ref, lng_ref, lnb_ref, wout_ref,
                        ya_ref, st_ref, ubuf, cbuf):
    i = pl.program_id(0)
    pl.when(i < PROMPT_TILES)(functools.partial(
        _conv_prompt_tile, i % SEQ_TILES, glu_ref, dww_ref, dwb_ref, lng_ref, lnb_ref, wout_ref,
        ya_ref, st_ref, ubuf, cbuf))
    pl.when(i == PROMPT_TILES)(functools.partial(_sample_rows_tile, ys_ref, ya_ref))


def _conv_prompt_tile(t, glu_ref, dww_ref, dwb_ref, lng_ref, lnb_ref, wout_ref,
                      ya_ref, st_ref, ubuf, cbuf):
    halo = 32

    @pl.when(t == 0)
    def _():
        ubuf[0:halo, :] = jnp.zeros((halo, D_CONV), F32)

    @pl.when(t > 0)
    def _():
        ubuf[0:halo, :] = ubuf[SEQ_TILE:SEQ_TILE + halo, :]

    glu = glu_ref[...]
    ubuf[halo:halo + SEQ_TILE, :] = glu[:, :D_CONV] * jax.nn.sigmoid(glu[:, D_CONV:])
    w = dww_ref[...]
    first = halo - (DW_WIDTH - 1)
    for rb in range(SEQ_TILE // ROW_BLK):
        r0 = rb * ROW_BLK
        acc = jnp.zeros((ROW_BLK, D_CONV), F32)
        for j in range(DW_WIDTH):
            acc = acc + w[j:j + 1, :] * ubuf[r0 + first + j:r0 + first + j + ROW_BLK, :]
        c = _conv_tail(acc, dwb_ref[...], lng_ref[...], lnb_ref[...])
        cbuf[r0:r0 + ROW_BLK, :] = c.astype(BF16)
    ya_ref[...] = jnp.dot(cbuf[...], wout_ref[...], preferred_element_type=F32)

    @pl.when(t == SEQ_TILES - 1)
    def _():
        st_ref[...] = ubuf[halo + SEQ_TILE - (DW_WIDTH - 1):halo + SEQ_TILE, :]


def _seq_of_tile(i):
    return jnp.minimum(i // SEQ_TILES, BATCH - 1)


def _conv_prompt(glu, ya_sample, dww, dwb, lng, lnb, wout):
    vec = pl.BlockSpec((1, D_CONV), lambda i: (0, 0))
    return pl.pallas_call(
        _conv_prompt_kernel,
        grid=(PROMPT_TILES + 1,),
        in_specs=[pl.BlockSpec((SEQ_TILE, 2 * D_CONV), lambda i: (i, 0)),
                  pl.BlockSpec((DEC_BATCH, D_MODEL), lambda i: (0, 0)),
                  pl.BlockSpec((DW_WIDTH, D_CONV), lambda i: (0, 0)),
                  vec, vec, vec,
                  pl.BlockSpec((D_CONV, D_MODEL), lambda i: (0, 0))],
        out_specs=[pl.BlockSpec((SEQ_TILE, D_MODEL), lambda i: (i, 0)),
                   pl.BlockSpec((None, DW_WIDTH - 1, D_CONV), lambda i: (_seq_of_tile(i), 0, 0))],
        out_shape=[jax.ShapeDtypeStruct((N_TOK, D_MODEL), F32),
                   jax.ShapeDtypeStruct((BATCH, DW_WIDTH - 1, D_CONV), F32)],
        scratch_shapes=[pltpu.VMEM((SEQ_TILE + 32, D_CONV), F32),
                        pltpu.VMEM((SEQ_TILE, D_CONV), BF16)],
        compiler_params=pltpu.CompilerParams(dimension_semantics=("arbitrary",)),
        name="conv_prompt",
    )(glu, ya_sample, dww, dwb, lng, lnb, wout)


def _conv_sample_kernel(glu_ref, st_ref, dww_ref, dwb_ref, lng_ref, lnb_ref, wout_ref,
                        ya_ref, nst_ref):
    glu = glu_ref[...]
    u = glu[:, :D_CONV] * jax.nn.sigmoid(glu[:, D_CONV:])
    w = dww_ref[...]
    nbuf = DW_WIDTH - 1
    acc = w[nbuf:nbuf + 1, :] * u
    for j in range(nbuf):
        acc = acc + w[j:j + 1, :] * st_ref[:, j * D_CONV:(j + 1) * D_CONV]
    c = _conv_tail(acc, dwb_ref[...], lng_ref[...], lnb_ref[...])
    ya_ref[...] = _bdot(c, wout_ref[...])
    nst_ref[:, 0:(nbuf - 1) * D_CONV] = st_ref[:, D_CONV:nbuf * D_CONV]
    nst_ref[:, (nbuf - 1) * D_CONV:nbuf * D_CONV] = u


def _conv_sample(glu, st, dww, dwb, lng, lnb, wout):
    nbuf = DW_WIDTH - 1
    blk = N_PROMPT // DEC_BATCH
    vec = pl.BlockSpec((1, D_CONV), lambda i: (0, 0))
    return pl.pallas_call(
        _conv_sample_kernel,
        grid=(1,),
        in_specs=[pl.BlockSpec((DEC_BATCH, 2 * D_CONV), lambda i: (blk, 0)),
                  pl.BlockSpec((DEC_BATCH, nbuf * D_CONV), lambda i: (0, 0)),
                  pl.BlockSpec((DW_WIDTH, D_CONV), lambda i: (0, 0)),
                  vec, vec, vec,
                  pl.BlockSpec((D_CONV, D_MODEL), lambda i: (0, 0))],
        out_specs=[pl.BlockSpec((DEC_BATCH, D_MODEL), lambda i: (0, 0)),
                   pl.BlockSpec((DEC_BATCH, nbuf * D_CONV), lambda i: (0, 0))],
        out_shape=[jax.ShapeDtypeStruct((DEC_BATCH, D_MODEL), F32),
                   jax.ShapeDtypeStruct((DEC_BATCH, nbuf * D_CONV), F32)],
        compiler_params=pltpu.CompilerParams(dimension_semantics=("arbitrary",)),
        name="conv_sample",
    )(glu, st, dww, dwb, lng, lnb, wout)


def _qkv_act(c):
    c = _silu(c)
    qs, ks, vs = [], [], []
    for h in range(N_HEADS):
        q = c[:, h * HEAD_K:(h + 1) * HEAD_K]
        k = c[:, D_DELTA_K + h * HEAD_K:D_DELTA_K + (h + 1) * HEAD_K]
        qs.append(q * lax.rsqrt(jnp.sum(q * q, axis=-1, keepdims=True) + EPS) * (HEAD_K ** -0.5))
        ks.append(k * lax.rsqrt(jnp.sum(k * k, axis=-1, keepdims=True) + EPS))
        vs.append(c[:, 2 * D_DELTA_K + h * HEAD_V:2 * D_DELTA_K + (h + 1) * HEAD_V])
    return qs, ks, vs


def _gate_norm(o, ong, z):
    o = o * lax.rsqrt(jnp.mean(o * o, axis=-1, keepdims=True) + EPS) * ong
    return o * _silu(z)


def _gdn_prompt_kernel(qkv_ref, z_ref, ab_ref, ys_ref, cw_ref, alog_ref, dtb_ref, ong_ref, wout_ref,
                       yb_ref, sq_ref, ss_ref,
                       xbuf, qkvs, gs, bs, obuf, s_scr):
    i = pl.program_id(0)
    pl.when(i < PROMPT_TILES)(functools.partial(
        _gdn_prompt_tile, i % SEQ_TILES, qkv_ref, z_ref, ab_ref, cw_ref, alog_ref, dtb_ref, ong_ref,
        wout_ref, yb_ref, sq_ref, ss_ref, xbuf, qkvs, gs, bs, obuf, s_scr))
    pl.when(i == PROMPT_TILES)(functools.partial(_sample_rows_tile, ys_ref, yb_ref))


def _gdn_prompt_tile(t, qkv_ref, z_ref, ab_ref, cw_ref, alog_ref, dtb_ref, ong_ref, wout_ref,
                     yb_ref, sq_ref, ss_ref, xbuf, qkvs, gs, bs, obuf, s_scr):
    halo = 8
    rows = N_HEADS * CHUNK

    @pl.when(t == 0)
    def _():
        xbuf[0:halo, :] = jnp.zeros((halo, CONV_DIM), F32)
        s_scr[...] = jnp.zeros((N_HEADS, HEAD_K, HEAD_V), F32)

    @pl.when(t > 0)
    def _():
        xbuf[0:halo, :] = xbuf[SEQ_TILE:SEQ_TILE + halo, :]

    xbuf[halo:halo + SEQ_TILE, :] = qkv_ref[...]
    cw = cw_ref[...]
    first = halo - (QK_CONV - 1)
    for rb in range(SEQ_TILE // ROW_BLK):
        r0 = rb * ROW_BLK
        acc = cw[0:1, :] * xbuf[r0 + first:r0 + first + ROW_BLK, :]
        for j in range(1, QK_CONV):
            acc = acc + cw[j:j + 1, :] * xbuf[r0 + first + j:r0 + first + j + ROW_BLK, :]
        qs, ks, vs = _qkv_act(acc)
        for h in range(N_HEADS):
            qkvs[r0:r0 + ROW_BLK, h * HEAD_K:(h + 1) * HEAD_K] = qs[h]
            qkvs[r0:r0 + ROW_BLK, D_DELTA_K + h * HEAD_K:D_DELTA_K + (h + 1) * HEAD_K] = ks[h]
            qkvs[r0:r0 + ROW_BLK, 2 * D_DELTA_K + h * HEAD_V:2 * D_DELTA_K + (h + 1) * HEAD_V] = vs[h]

    ab = ab_ref[...]
    gs[...] = -jnp.exp(alog_ref[...]) * jax.nn.softplus(ab + dtb_ref[...])
    bs[...] = jax.nn.sigmoid(ab)

    ri = lax.broadcasted_iota(jnp.int32, (rows, rows), 0)
    ci = lax.broadcasted_iota(jnp.int32, (rows, rows), 1)
    same_head = (ri // CHUNK) == (ci // CHUNK)
    tril = same_head & (ri >= ci)
    strict = same_head & (ri > ci)
    diag16 = (ri // 16) == (ci // 16)
    eye = jnp.where(ri == ci, 1.0, 0.0).astype(F32)
    r64 = lax.broadcasted_iota(jnp.int32, (CHUNK, CHUNK), 0)
    c64 = lax.broadcasted_iota(jnp.int32, (CHUNK, CHUNK), 1)
    csum = jnp.where(r64 >= c64, 1.0, 0.0).astype(F32)
    ong = ong_ref[...]

    def stack(x):
        return jnp.concatenate([x[:, h * 128:(h + 1) * 128] for h in range(N_HEADS)], axis=0)

    def chunk_body(c, carry):
        r0 = pl.multiple_of(c * CHUNK, CHUNK)
        gc = _hdot(csum, gs[pl.ds(r0, CHUNK), :])
        gct = gc.T
        bt = bs[pl.ds(r0, CHUNK), :]
        k_st = stack(qkvs[pl.ds(r0, CHUNK), D_DELTA_K:2 * D_DELTA_K])
        q_st = stack(qkvs[pl.ds(r0, CHUNK), 0:D_DELTA_K])
        v_st = stack(qkvs[pl.ds(r0, CHUNK), 2 * D_DELTA_K:CONV_DIM])
        beta = jnp.concatenate([bt[:, N_HEADS + h:N_HEADS + h + 1] for h in range(N_HEADS)], axis=0)
        gcol = jnp.concatenate([gc[:, h:h + 1] for h in range(N_HEADS)], axis=0)
        grow = jnp.concatenate([gct[h:h + 1, :] for h in range(N_HEADS)], axis=1)
        glast = jnp.concatenate(
            [jnp.broadcast_to(gc[CHUNK - 1:CHUNK, h:h + 1], (CHUNK, 1)) for h in range(N_HEADS)], axis=0)
        dec = jnp.where(tril, jnp.exp(jnp.where(tril, gcol - grow, 0.0)), 0.0)
        kb = k_st * beta
        kq = lax.dot_general(jnp.concatenate([kb, q_st], axis=0).astype(BF16), k_st.astype(BF16),
                             (((1,), (1,)), ((), ())), preferred_element_type=F32)
        m = jnp.where(strict, kq[:rows] * dec, 0.0)
        a_qk = kq[rows:] * dec
        mbd = jnp.where(diag16, m, 0.0)
        lo = m - mbd
        p = _hdot(mbd, mbd)
        x = eye - mbd
        x = x + _hdot(x, p)
        p = _hdot(p, p)
        x = x + _hdot(x, p)
        p = _hdot(p, p)
        x = x + _hdot(x, p)
        nn = _hdot(x, lo)
        n2 = _hdot(nn, nn)
        t1 = x + _hdot(n2, x)
        tm = t1 - _hdot(nn, t1)
        eg = jnp.exp(gcol)
        uw = _bdot(tm, jnp.concatenate([v_st * beta, kb * eg], axis=1))
        qg = q_st * eg
        kd = k_st * jnp.exp(glast - gcol)
        vnews, qss = [], []
        for h in range(N_HEADS):
            sl = slice(h * CHUNK, (h + 1) * CHUNK)
            s = s_scr[h]
            wq = _bdot(jnp.concatenate([uw[sl, HEAD_V:], qg[sl]], axis=0), s)
            vnew = uw[sl, :HEAD_V] - wq[:CHUNK]
            vnews.append(vnew)
            qss.append(wq[CHUNK:])
            dl = jnp.exp(gc[CHUNK - 1:CHUNK, h:h + 1])
            s_scr[h] = s * dl + lax.dot_general(
                kd[sl].astype(BF16), vnew.astype(BF16), (((0,), (0,)), ((), ())),
                preferred_element_type=F32)
        o = jnp.concatenate(qss, axis=0) + _bdot(a_qk, jnp.concatenate(vnews, axis=0))
        for h in range(N_HEADS):
            zh = z_ref[pl.ds(r0, CHUNK), h * HEAD_V:(h + 1) * HEAD_V]
            obuf[pl.ds(r0, CHUNK), h * HEAD_V:(h + 1) * HEAD_V] = _gate_norm(
                o[h * CHUNK:(h + 1) * CHUNK], ong, zh)
        return carry

    lax.fori_loop(0, SEQ_TILE // CHUNK, chunk_body, 0)
    yb_ref[...] = _bdot(obuf[...], wout_ref[...])

    @pl.when(t == SEQ_TILES - 1)
    def _():
        sq_ref[...] = xbuf[halo + SEQ_TILE - (QK_CONV - 1):halo + SEQ_TILE, :]
        ss_ref[...] = s_scr[...]


def _gdn_prompt(qkv, z, ab, yb_sample, cw, alog, dtb, ong, wout):
    tile = lambda w: pl.BlockSpec((SEQ_TILE, w), lambda i: (i, 0))
    vec = pl.BlockSpec((1, 128), lambda i: (0, 0))
    return pl.pallas_call(
        _gdn_prompt_kernel,
        grid=(PROMPT_TILES + 1,),
        in_specs=[tile(CONV_DIM), tile(D_DELTA_V), tile(128),
                  pl.BlockSpec((DEC_BATCH, D_MODEL), lambda i: (0, 0)),
                  pl.BlockSpec((QK_CONV, CONV_DIM), lambda i: (0, 0)),
                  vec, vec, vec,
                  pl.BlockSpec((D_DELTA_V, D_MODEL), lambda i: (0, 0))],
        out_specs=[tile(D_MODEL),
                   pl.BlockSpec((None, QK_CONV - 1, CONV_DIM), lambda i: (_seq_of_tile(i), 0, 0)),
                   pl.BlockSpec((None, N_HEADS, HEAD_K, HEAD_V),
                                lambda i: (_seq_of_tile(i), 0, 0, 0))],
        out_shape=[jax.ShapeDtypeStruct((N_TOK, D_MODEL), F32),
                   jax.ShapeDtypeStruct((BATCH, QK_CONV - 1, CONV_DIM), F32),
                   jax.ShapeDtypeStruct((BATCH, N_HEADS, HEAD_K, HEAD_V), F32)],
        scratch_shapes=[pltpu.VMEM((SEQ_TILE + 8, CONV_DIM), F32),
                        pltpu.VMEM((SEQ_TILE, CONV_DIM), F32),
                        pltpu.VMEM((SEQ_TILE, 128), F32),
                        pltpu.VMEM((SEQ_TILE, 128), F32),
                        pltpu.VMEM((SEQ_TILE, D_DELTA_V), F32),
                        pltpu.VMEM((N_HEADS, HEAD_K, HEAD_V), F32)],
        compiler_params=pltpu.CompilerParams(
            dimension_semantics=("arbitrary",), vmem_limit_bytes=VMEM_BIG),
        name="gdn_prompt",
    )(qkv, z, ab, yb_sample, cw, alog, dtb, ong, wout)


def _gdn_sample_kernel(qkv_ref, z_ref, ab_ref, st_ref, s_ref, cw_ref, alog_ref, dtb_ref,
                       ong_ref, wout_ref, yb_ref, nst_ref, ns_ref):
    x = qkv_ref[...]
    cw = cw_ref[...]
    nbuf = QK_CONV - 1
    acc = cw[nbuf:nbuf + 1, :] * x
    for j in range(nbuf):
        acc = acc + cw[j:j + 1, :] * st_ref[:, j * CONV_DIM:(j + 1) * CONV_DIM]
    nst_ref[:, 0:(nbuf - 1) * CONV_DIM] = st_ref[:, CONV_DIM:nbuf * CONV_DIM]
    nst_ref[:, (nbuf - 1) * CONV_DIM:nbuf * CONV_DIM] = x
    qs, ks, vs = _qkv_act(acc)
    ab = ab_ref[...]
    g = -jnp.exp(alog_ref[...]) * jax.nn.softplus(ab + dtb_ref[...])
    beta = jax.nn.sigmoid(ab)
    eg = jnp.exp(g)
    z = z_ref[...]
    outs = []
    for h in range(N_HEADS):
        bh = beta[:, N_HEADS + h:N_HEADS + h + 1]
        egh = eg[:, h:h + 1]
        kb = ks[h] * bh
        w = kb * egh
        qg = qs[h] * egh
        u = vs[h] * bh
        qk = jnp.sum(qs[h] * ks[h], axis=-1, keepdims=True)
        o_rows = []
        for s in range(DEC_BLK):
            st = s_ref[s, h]
            wq = _bdot(jnp.concatenate([w[s:s + 1], qg[s:s + 1]], axis=0), st)
            vnew = u[s:s + 1] - wq[0:1]
            o_rows.append(wq[1:2] + qk[s:s + 1] * vnew)
            ns_ref[s, h] = st * egh[s:s + 1] + lax.dot_general(
                ks[h][s:s + 1].astype(BF16), vnew.astype(BF16), (((0,), (0,)), ((), ())),
                preferred_element_type=F32)
        o = jnp.concatenate(o_rows, axis=0)
        outs.append(_gate_norm(o, ong_ref[...], z[:, h * HEAD_V:(h + 1) * HEAD_V]))
    yb_ref[...] = _bdot(jnp.concatenate(outs, axis=1), wout_ref[...])


def _gdn_sample(qkv, z, ab, st, s0, cw, alog, dtb, ong, wout):
    nbuf = QK_CONV - 1
    blk0 = N_PROMPT // DEC_BLK
    rows = lambda w: pl.BlockSpec((DEC_BLK, w), lambda i: (blk0 + i, 0))
    vec = pl.BlockSpec((1, 128), lambda i: (0, 0))
    sblk = pl.BlockSpec((DEC_BLK, N_HEADS, HEAD_K, HEAD_V), lambda i: (i, 0, 0, 0))
    return pl.pallas_call(
        _gdn_sample_kernel,
        grid=(DEC_BATCH // DEC_BLK,),
        in_specs=[rows(CONV_DIM), rows(D_DELTA_V), rows(128),
                  pl.BlockSpec((DEC_BLK, nbuf * CONV_DIM), lambda i: (i, 0)),
                  sblk,
                  pl.BlockSpec((QK_CONV, CONV_DIM), lambda i: (0, 0)),
                  vec, vec, vec,
                  pl.BlockSpec((D_DELTA_V, D_MODEL), lambda i: (0, 0))],
        out_specs=[pl.BlockSpec((DEC_BLK, D_MODEL), lambda i: (i, 0)),
                   pl.BlockSpec((DEC_BLK, nbuf * CONV_DIM), lambda i: (i, 0)),
                   sblk],
        out_shape=[jax.ShapeDtypeStruct((DEC_BATCH, D_MODEL), F32),
                   jax.ShapeDtypeStruct((DEC_BATCH, nbuf * CONV_DIM), F32),
                   jax.ShapeDtypeStruct((DEC_BATCH, N_HEADS, HEAD_K, HEAD_V), F32)],
        compiler_params=pltpu.CompilerParams(dimension_semantics=("arbitrary",)),
        name="gdn_sample",
    )(qkv, z, ab, st, s0, cw, alog, dtb, ong, wout)


def _router_kernel(x_ref, g_ref, r_ref, h_ref, info_ref):
    h = _rms(x_ref[...], g_ref[...])
    h_ref[...] = h
    logits = _hdot(h, r_ref[...])
    lane = lax.broadcasted_iota(jnp.int32, logits.shape, 1)
    neg = jnp.float32(-jnp.inf)
    logits = jnp.where(lane < N_EXPERTS, logits, neg)
    m1 = jnp.max(logits, axis=-1, keepdims=True)
    i1 = jnp.min(jnp.where(logits == m1, lane, 128), axis=-1, keepdims=True)
    rest = jnp.where(lane == i1, neg, logits)
    m2 = jnp.max(rest, axis=-1, keepdims=True)
    i2 = jnp.min(jnp.where(rest == m2, lane, 128), axis=-1, keepdims=True)
    e = jnp.exp(m2 - m1)
    p1 = 1.0 / (1.0 + e)
    p2 = e / (1.0 + e)
    info = jnp.where(lane == 0, i1.astype(F32), 0.0)
    info = jnp.where(lane == 1, i2.astype(F32), info)
    info = jnp.where(lane == 2, p1, info)
    info = jnp.where(lane == 3, p2, info)
    info_ref[...] = info


def _router(x, g, r_pad):
    row = lambda w: pl.BlockSpec((TOK_TILE, w), lambda i: (i, 0))
    return pl.pallas_call(
        _router_kernel,
        grid=(N_TOK // TOK_TILE,),
        in_specs=[row(D_MODEL), pl.BlockSpec((1, D_MODEL), lambda i: (0, 0)),
                  pl.BlockSpec((D_MODEL, 128), lambda i: (0, 0))],
        out_specs=[row(D_MODEL), row(128)],
        out_shape=[jax.ShapeDtypeStruct((N_TOK, D_MODEL), F32),
                   jax.ShapeDtypeStruct((N_TOK, 128), F32)],
        compiler_params=pltpu.CompilerParams(dimension_semantics=("arbitrary",)),
        name="moe_router",
    )(x, g, r_pad)


def _dispatch_kernel(dest_ref, xs_in_ref, h_ref, xs_ref, sem):
    del xs_in_ref
    i = pl.program_id(0)
    base = i * (2 * TOK_TILE)

    def copy(r, k):
        return pltpu.make_async_copy(
            h_ref.at[pl.ds(r, 1), :], xs_ref.at[pl.ds(dest_ref[base + 2 * r + k], 1), :], sem)

    def start(r, c):
        copy(r, 0).start()
        copy(r, 1).start()
        return c

    def wait(r, c):
        copy(r, 0).wait()
        copy(r, 1).wait()
        return c

    lax.fori_loop(0, TOK_TILE, start, 0)
    lax.fori_loop(0, TOK_TILE, wait, 0)


def _dispatch(dest, xs0, h):
    return pl.pallas_call(
        _dispatch_kernel,
        grid_spec=pltpu.PrefetchScalarGridSpec(
            num_scalar_prefetch=1,
            grid=(N_TOK // TOK_TILE,),
            in_specs=[pl.BlockSpec(memory_space=pl.ANY),
                      pl.BlockSpec((TOK_TILE, D_MODEL), lambda i, d: (i, 0))],
            out_specs=pl.BlockSpec(memory_space=pl.ANY),
            scratch_shapes=[pltpu.SemaphoreType.DMA(())]),
        out_shape=jax.ShapeDtypeStruct((MOE_ROWS, D_MODEL), F32),
        input_output_aliases={1: 0},
        compiler_params=pltpu.CompilerParams(dimension_semantics=("arbitrary",)),
        name="moe_dispatch",
    )(dest, xs0, h)


def _experts_kernel(te_ref, tv_ref, x_ref, wg_ref, wu_ref, wd_ref, o_ref):
    i = pl.program_id(0)

    @pl.when(tv_ref[i] > 0)
    def _():
        o_ref[...] = _swiglu_rows(x_ref[...].astype(BF16), wg_ref, wu_ref, wd_ref)

    @pl.when(tv_ref[i] == 0)
    def _():
        o_ref[...] = jnp.zeros((MOE_TILE, D_MODEL), F32)


def _experts(tile_expert, tile_valid, xs, wg, wu, wd):
    row = pl.BlockSpec((MOE_TILE, D_MODEL), lambda i, te, tv: (i, 0))
    return pl.pallas_call(
        _experts_kernel,
        grid_spec=pltpu.PrefetchScalarGridSpec(
            num_scalar_prefetch=2,
            grid=(MOE_ROWS // MOE_TILE,),
            in_specs=[row,
                      pl.BlockSpec((None, D_MODEL, D_FF), lambda i, te, tv: (te[i], 0, 0)),
                      pl.BlockSpec((None, D_MODEL, D_FF), lambda i, te, tv: (te[i], 0, 0)),
                      pl.BlockSpec((None, D_FF, D_MODEL), lambda i, te, tv: (te[i], 0, 0))],
            out_specs=row),
        out_shape=jax.ShapeDtypeStruct((MOE_ROWS, D_MODEL), F32),
        compiler_params=pltpu.CompilerParams(
            dimension_semantics=("arbitrary",), vmem_limit_bytes=VMEM_BIG),
        name="moe_experts",
    )(tile_expert, tile_valid, xs, wg, wu, wd)


def _combine_kernel(dest_ref, x_ref, info_ref, g_ref, ys_ref, o_ref, ybuf, sem):
    i = pl.program_id(0)
    base = i * (2 * TOK_TILE)

    def copy(r, k):
        return pltpu.make_async_copy(
            ys_ref.at[pl.ds(dest_ref[base + 2 * r + k], 1), :], ybuf.at[k, pl.ds(r, 1), :], sem)

    def start(r, c):
        copy(r, 0).start()
        copy(r, 1).start()
        return c

    def wait(r, c):
        copy(r, 0).wait()
        copy(r, 1).wait()
        return c

    lax.fori_loop(0, TOK_TILE, start, 0)
    lax.fori_loop(0, TOK_TILE, wait, 0)
    info = info_ref[...]
    x = x_ref[...] + info[:, 2:3] * ybuf[0] + info[:, 3:4] * ybuf[1]
    o_ref[...] = _rms(x, g_ref[...])


def _combine(dest, x, info, g, ys):
    return pl.pallas_call(
        _combine_kernel,
        grid_spec=pltpu.PrefetchScalarGridSpec(
            num_scalar_prefetch=1,
            grid=(N_TOK // TOK_TILE,),
            in_specs=[pl.BlockSpec((TOK_TILE, D_MODEL), lambda i, d: (i, 0)),
                      pl.BlockSpec((TOK_TILE, 128), lambda i, d: (i, 0)),
                      pl.BlockSpec((1, D_MODEL), lambda i, d: (0, 0)),
                      pl.BlockSpec(memory_space=pl.ANY)],
            out_specs=pl.BlockSpec((TOK_TILE, D_MODEL), lambda i, d: (i, 0)),
            scratch_shapes=[pltpu.VMEM((2, TOK_TILE, D_MODEL), F32),
                            pltpu.SemaphoreType.DMA(())]),
        out_shape=jax.ShapeDtypeStruct((N_TOK, D_MODEL), F32),
        compiler_params=pltpu.CompilerParams(dimension_semantics=("arbitrary",)),
        name="moe_combine",
    )(dest, x, info, g, ys)


def _moe_layer(x, g2, router_w, wg, wu, wd, final_g):
    r_pad = jnp.zeros((D_MODEL, 128), F32).at[:, :N_EXPERTS].set(router_w)
    h, info = _router(x, g2, r_pad)
    expert = info[:, 0:2].astype(jnp.int32).reshape(-1)
    onehot = (expert[:, None] == jnp.arange(N_EXPERTS, dtype=jnp.int32)[None, :]).astype(jnp.int32)
    csum = jnp.cumsum(onehot, axis=0)
    counts = csum[-1]
    rank = jnp.sum((csum - onehot) * onehot, axis=1)
    padded = ((counts + MOE_TILE - 1) // MOE_TILE) * MOE_TILE
    ends = jnp.cumsum(padded)
    starts = ends - padded
    dest = (jnp.sum(onehot * starts[None, :], axis=1) + rank).astype(jnp.int32)
    tile_start = jnp.arange(MOE_ROWS // MOE_TILE, dtype=jnp.int32) * MOE_TILE
    tile_expert = jnp.minimum(
        jnp.sum((tile_start[:, None] >= ends[None, :]).astype(jnp.int32), axis=1), N_EXPERTS - 1)
    tile_valid = (tile_start < ends[-1]).astype(jnp.int32)
    last_used = jnp.max(jnp.where(tile_valid > 0, tile_expert, 0))
    tile_expert = jnp.where(tile_valid > 0, tile_expert, last_used).astype(jnp.int32)

    xs = _dispatch(dest, jnp.zeros((MOE_ROWS, D_MODEL), F32), h)
    ys = _experts(tile_expert, tile_valid, xs, wg, wu, wd)
    return _combine(dest, x, info, final_g, ys)


def _rearranged_w_in(w_in):
    off_qkv = 2 * D_CONV
    off_z = off_qkv + CONV_DIM
    off_a = off_z + D_DELTA_V
    off_ga = off_a + 2 * N_HEADS
    off_gb = off_ga + D_MODEL
    return jnp.concatenate(
        [w_in[:, :off_a], w_in[:, off_ga:off_gb], w_in[:, off_gb:], w_in[:, off_a:off_ga],
         jnp.zeros((D_MODEL, 128 - 2 * N_HEADS), w_in.dtype)], axis=1).astype(BF16)


def _lane_vec(v, offset=0):
    return jnp.zeros((1, 128), F32).at[0, offset:offset + v.shape[0]].set(v)


def kernel(x_prompt, x_sample, state_delta, state_qkv_conv, state_dwconv, norm1_g, w_in, qkv_conv_w, a_log, dt_bias, o_norm_g, w_delta_out, dw_w, dw_b, ln_g, ln_b, w_conv_out, w_o, norm2_g, ffn_w_gate, ffn_w_up, ffn_w_down, router_w, exp_w_gate, exp_w_up, exp_w_down, final_norm_g):
    x = jnp.concatenate([x_prompt.reshape(N_PROMPT, D_MODEL), x_sample.reshape(DEC_BATCH, D_MODEL),
                         jnp.zeros((TOK_TILE - DEC_BATCH, D_MODEL), F32)], axis=0)
    s_p, q_p, d_p, s_s, q_s, d_s = [], [], [], [], [], []
    for l in range(DEPTH):
        glu, qkv, z, ga, gb, ab = _norm_proj(x, norm1_g[l][None, :], _rearranged_w_in(w_in[l]))
        dwb, lng, lnb = dw_b[l][None, :], ln_g[l][None, :], ln_b[l][None, :]
        wco = w_conv_out[l].astype(BF16)
        ya_s, dst_s = _conv_sample(glu, state_dwconv[l].reshape(DEC_BATCH, -1), dw_w[l], dwb, lng, lnb, wco)
        ya, dst_p = _conv_prompt(glu, ya_s, dw_w[l], dwb, lng, lnb, wco)
        alog, dtb, ong = _lane_vec(a_log[l]), _lane_vec(dt_bias[l]), o_norm_g[l][None, :]
        wdo = w_delta_out[l].astype(BF16)
        yb_s, qst_s, sst_s = _gdn_sample(qkv, z, ab, state_qkv_conv[l].reshape(DEC_BATCH, -1),
                                         state_delta[l], qkv_conv_w[l], alog, dtb, ong, wdo)
        yb, qst_p, sst_p = _gdn_prompt(qkv, z, ab, yb_s, qkv_conv_w[l], alog, dtb, ong, wdo)
        x = _merge(x, ya, yb, ga, gb, w_o[l].astype(BF16))
        i = l // 2
        if l % 2 == 0:
            x = _ffn(x, norm2_g[l][None, :], ffn_w_gate[i].astype(BF16), ffn_w_up[i].astype(BF16),
                     ffn_w_down[i].astype(BF16))
        else:
            x = _moe_layer(x, norm2_g[l][None, :], router_w[i], exp_w_gate[i].astype(BF16),
                           exp_w_up[i].astype(BF16), exp_w_down[i].astype(BF16),
                           final_norm_g[None, :])
        s_p.append(sst_p)
        q_p.append(qst_p)
        d_p.append(dst_p)
        s_s.append(sst_s)
        q_s.append(qst_s.reshape(DEC_BATCH, QK_CONV - 1, CONV_DIM))
        d_s.append(dst_s.reshape(DEC_BATCH, DW_WIDTH - 1, D_CONV))
    y_prompt = x[:N_PROMPT].reshape(BATCH, SEQ, D_MODEL)
    y_sample = x[N_PROMPT:N_PROMPT + DEC_BATCH].reshape(DEC_BATCH, 1, D_MODEL)
    return (y_prompt, y_sample, jnp.stack(s_p), jnp.stack(q_p), jnp.stack(d_p),
            jnp.stack(s_s), jnp.stack(q_s), jnp.stack(d_s))
```

```python
import functools

import jax
import jax.numpy as jnp
from jax import lax
from jax.experimental import pallas as pl
from jax.experimental.pallas import tpu as pltpu

F32 = jnp.float32
BF16 = jnp.bfloat16
HIGHEST = lax.Precision.HIGHEST

D_MODEL = 1024
BATCH = 8
SEQ = 2048
DEPTH = 2
DEC_BATCH = 128
D_CONV = 512
DW_WIDTH = 31
N_HEADS = 4
HEAD_K = 128
HEAD_V = 128
QK_CONV = 4
D_DELTA_K = N_HEADS * HEAD_K
D_DELTA_V = N_HEADS * HEAD_V
CONV_DIM = 2 * D_DELTA_K + D_DELTA_V
D_FF = 2816
N_EXPERTS = 8
EPS = 1e-6

N_PROMPT = BATCH * SEQ
TOK_TILE = 512
SEQ_TILE = TOK_TILE
N_TOK = N_PROMPT + TOK_TILE
SEQ_TILES = SEQ // SEQ_TILE
PROMPT_TILES = N_PROMPT // TOK_TILE
CHUNK = 64
M_PITCH = 68
SLABS_TILE = 2 * (SEQ_TILE // CHUNK)
M_ROWS_TILE = SLABS_TILE * M_PITCH
SOLVE_SLABS = 128
ROW_BLK = 64
FF_HALF = D_FF // 2
MOE_TILE = 256
MOE_ROWS = ((2 * N_TOK + N_EXPERTS * (MOE_TILE - 1)) // MOE_TILE + 1) * MOE_TILE
DEC_BLK = 8

C_GLU, C_QKV, C_Z, C_GA, C_GB, C_AB = 0, 1024, 2560, 3072, 4096, 5120
N_PROJ = 5248

VMEM_BIG = 56 * 1024 * 1024


def _bdot(a, b):
    return jnp.dot(a.astype(BF16), b.astype(BF16), preferred_element_type=F32)


def _hdot(a, b):
    return jnp.dot(a, b, preferred_element_type=F32, precision=HIGHEST)


def _rms(x, g):
    return x * lax.rsqrt(jnp.mean(x * x, axis=-1, keepdims=True) + EPS) * g


def _silu(x):
    return x * jax.nn.sigmoid(x)


def _resident(shape):
    return pl.BlockSpec(shape, lambda *_: (0,) * len(shape), pipeline_mode=pl.Buffered(1))


def _norm_proj_kernel(x_ref, g_ref, w_ref, glu_ref, qkv_ref, z_ref, ga_ref, gb_ref, ab_ref):
    h = _rms(x_ref[...], g_ref[...]).astype(BF16)

    def proj(lo, hi):
        return jnp.dot(h, w_ref[:, lo:hi], preferred_element_type=F32)

    glu_ref[...] = proj(C_GLU, C_QKV)
    qkv_ref[...] = proj(C_QKV, C_Z)
    z_ref[...] = proj(C_Z, C_GA)
    ga_ref[...] = proj(C_GA, C_GB)
    gb_ref[...] = proj(C_GB, C_AB)
    ab_ref[...] = proj(C_AB, N_PROJ)


def _norm_proj(x, g, w_all):
    row = lambda w: pl.BlockSpec((TOK_TILE, w), lambda i: (i, 0))
    widths = (1024, CONV_DIM, D_DELTA_V, D_MODEL, D_MODEL, 128)
    return pl.pallas_call(
        _norm_proj_kernel,
        grid=(N_TOK // TOK_TILE,),
        in_specs=[row(D_MODEL),
                  pl.BlockSpec((1, D_MODEL), lambda i: (0, 0)),
                  _resident((D_MODEL, N_PROJ))],
        out_specs=[row(w) for w in widths],
        out_shape=[jax.ShapeDtypeStruct((N_TOK, w), F32) for w in widths],
        compiler_params=pltpu.CompilerParams(
            dimension_semantics=("arbitrary",), vmem_limit_bytes=VMEM_BIG),
        name="norm_proj",
    )(x, g, w_all)


def _merge_kernel(x_ref, ya_ref, yb_ref, ga_ref, gb_ref, wo_ref, o_ref):
    m = jax.nn.sigmoid(ga_ref[...]) * ya_ref[...] + jax.nn.sigmoid(gb_ref[...]) * yb_ref[...]
    o_ref[...] = x_ref[...] + _bdot(m, wo_ref[...])


def _merge(x, ya, yb, ga, gb, wo):
    row = pl.BlockSpec((TOK_TILE, D_MODEL), lambda i: (i, 0))
    return pl.pallas_call(
        _merge_kernel,
        grid=(N_TOK // TOK_TILE,),
        in_specs=[row, row, row, row, row, pl.BlockSpec((D_MODEL, D_MODEL), lambda i: (0, 0))],
        out_specs=row,
        out_shape=jax.ShapeDtypeStruct((N_TOK, D_MODEL), F32),
        compiler_params=pltpu.CompilerParams(dimension_semantics=("arbitrary",)),
        name="merge",
    )(x, ya, yb, ga, gb, wo)


def _swiglu_rows(h, wg_ref, wu_ref, wd_ref):
    acc = None
    for lo in (0, FF_HALF):
        a = jnp.dot(h, wg_ref[:, lo:lo + FF_HALF], preferred_element_type=F32)
        b = jnp.dot(h, wu_ref[:, lo:lo + FF_HALF], preferred_element_type=F32)
        part = jnp.dot((_silu(a) * b).astype(BF16), wd_ref[lo:lo + FF_HALF, :],
                       preferred_element_type=F32)
        acc = part if acc is None else acc + part
    return acc


def _ffn_kernel(x_ref, g_ref, wg_ref, wu_ref, wd_ref, o_ref):
    x = x_ref[...]
    h = _rms(x, g_ref[...]).astype(BF16)
    o_ref[...] = x + _swiglu_rows(h, wg_ref, wu_ref, wd_ref)


def _ffn(x, g, wg, wu, wd):
    row = pl.BlockSpec((TOK_TILE, D_MODEL), lambda i: (i, 0))
    return pl.pallas_call(
        _ffn_kernel,
        grid=(N_TOK // TOK_TILE,),
        in_specs=[row, pl.BlockSpec((1, D_MODEL), lambda i: (0, 0)),
                  _resident((D_MODEL, D_FF)), _resident((D_MODEL, D_FF)),
                  _resident((D_FF, D_MODEL))],
        out_specs=row,
        out_shape=jax.ShapeDtypeStruct((N_TOK, D_MODEL), F32),
        compiler_params=pltpu.CompilerParams(
            dimension_semantics=("arbitrary",), vmem_limit_bytes=VMEM_BIG),
        name="ffn_dense",
    )(x, g, wg, wu, wd)


def _conv_tail(c, dwb, lng, lnb):
    c = c + dwb
    mu = jnp.mean(c, axis=-1, keepdims=True)
    var = jnp.mean(jnp.square(c - mu), axis=-1, keepdims=True)
    c = (c - mu) * lax.rsqrt(var + EPS) * lng + lnb
    return _silu(c)


def _sample_rows_tile(ys_ref, y_ref):
    y_ref[0:DEC_BATCH, :] = ys_ref[...]
    y_ref[DEC_BATCH:, :] = jnp.zeros((TOK_TILE - DEC_BATCH, D_MODEL), F32)


def _conv_prompt_kernel(glu_ref, ys_ref, dww_ref, dwb_ref, lng_ref, lnb_ref, wout_ref,
                        ya_ref, st_ref, ubuf, cbuf):
    i = pl.program_id(0)
    pl.when(i < PROMPT_TILES)(functools.partial(
        _conv_prompt_tile, i % SEQ_TILES, glu_ref, dww_ref, dwb_ref, lng_ref, lnb_ref, wout_ref,
        ya_ref, st_ref, ubuf, cbuf))
    pl.when(i == PROMPT_TILES)(functools.partial(_sample_rows_tile, ys_ref, ya_ref))


def _conv_prompt_tile(t, glu_ref, dww_ref, dwb_ref, lng_ref, lnb_ref, wout_ref,
                      ya_ref, st_ref, ubuf, cbuf):
    halo = 32

    @pl.when(t == 0)
    def _():
        ubuf[0:halo, :] = jnp.zeros((halo, D_CONV), F32)

    @pl.when(t > 0)
    def _():
        ubuf[0:halo, :] = ubuf[SEQ_TILE:SEQ_TILE + halo, :]

    glu = glu_ref[...]
    ubuf[halo:halo + SEQ_TILE, :] = glu[:, :D_CONV] * jax.nn.sigmoid(glu[:, D_CONV:])
    w = dww_ref[...]
    first = halo - (DW_WIDTH - 1)
    for rb in range(SEQ_TILE // ROW_BLK):
        r0 = rb * ROW_BLK
        acc = jnp.zeros((ROW_BLK, D_CONV), F32)
        for j in range(DW_WIDTH):
            acc = acc + w[j:j + 1, :] * ubuf[r0 + first + j:r0 + first + j + ROW_BLK, :]
        c = _conv_tail(acc, dwb_ref[...], lng_ref[...], lnb_ref[...])
        cbuf[r0:r0 + ROW_BLK, :] = c.astype(BF16)
    ya_ref[...] = jnp.dot(cbuf[...], wout_ref[...], preferred_element_type=F32)

    @pl.when(t == SEQ_TILES - 1)
    def _():
        st_ref[...] = ubuf[halo + SEQ_TILE - (DW_WIDTH - 1):halo + SEQ_TILE, :]


def _seq_of_tile(i):
    return jnp.minimum(i // SEQ_TILES, BATCH - 1)


def _conv_prompt(glu, ya_sample, dww, dwb, lng, lnb, wout):
    vec = pl.BlockSpec((1, D_CONV), lambda i: (0, 0))
    return pl.pallas_call(
        _conv_prompt_kernel,
        grid=(PROMPT_TILES + 1,),
        in_specs=[pl.BlockSpec((SEQ_TILE, 2 * D_CONV), lambda i: (i, 0)),
                  pl.BlockSpec((DEC_BATCH, D_MODEL), lambda i: (0, 0)),
                  pl.BlockSpec((DW_WIDTH, D_CONV), lambda i: (0, 0)),
                  vec, vec, vec,
                  pl.BlockSpec((D_CONV, D_MODEL), lambda i: (0, 0))],
        out_specs=[pl.BlockSpec((SEQ_TILE, D_MODEL), lambda i: (i, 0)),
                   pl.BlockSpec((None, DW_WIDTH - 1, D_CONV), lambda i: (_seq_of_tile(i), 0, 0))],
        out_shape=[jax.ShapeDtypeStruct((N_TOK, D_MODEL), F32),
                   jax.ShapeDtypeStruct((BATCH, DW_WIDTH - 1, D_CONV), F32)],
        scratch_shapes=[pltpu.VMEM((SEQ_TILE + 32, D_CONV), F32),
                        pltpu.VMEM((SEQ_TILE, D_CONV), BF16)],
        compiler_params=pltpu.CompilerParams(dimension_semantics=("arbitrary",)),
        name="conv_prompt",
    )(glu, ya_sample, dww, dwb, lng, lnb, wout)


def _conv_sample_kernel(glu_ref, st_ref, dww_ref, dwb_ref, lng_ref, lnb_ref, wout_ref,
                        ya_ref, nst_ref):
    glu = glu_ref[...]
    u = glu[:, :D_CONV] * jax.nn.sigmoid(glu[:, D_CONV:])
    w = dww_ref[...]
    nbuf = DW_WIDTH - 1
    acc = w[nbuf:nbuf + 1, :] * u
    for j in range(nbuf):
        acc = acc + w[j:j + 1, :] * st_ref[:, j * D_CONV:(j + 1) * D_CONV]
    c = _conv_tail(acc, dwb_ref[...], lng_ref[...], lnb_ref[...])
    ya_ref[...] = _bdot(c, wout_ref[...])
    nst_ref[:, 0:(nbuf - 1) * D_CONV] = st_ref[:, D_CONV:nbuf * D_CONV]
    nst_ref[:, (nbuf - 1) * D_CONV:nbuf * D_CONV] = u


def _conv_sample(glu, st, dww, dwb, lng, lnb, wout):
    nbuf = DW_WIDTH - 1
    blk = N_PROMPT // DEC_BATCH
    vec = pl.BlockSpec((1, D_CONV), lambda i: (0, 0))
    return pl.pallas_call(
        _conv_sample_kernel,
        grid=(1,),
        in_specs=[pl.BlockSpec((DEC_BATCH, 2 * D_CONV), lambda i: (blk, 0)),
                  pl.BlockSpec((DEC_BATCH, nbuf * D_CONV), lambda i: (0, 0)),
                  pl.BlockSpec((DW_WIDTH, D_CONV), lambda i: (0, 0)),
                  vec, vec, vec,
                  pl.BlockSpec((D_CONV, D_MODEL), lambda i: (0, 0))],
        out_specs=[pl.BlockSpec((DEC_BATCH, D_MODEL), lambda i: (0, 0)),
                   pl.BlockSpec((DEC_BATCH, nbuf * D_CONV), lambda i: (0, 0))],
        out_shape=[jax.ShapeDtypeStruct((DEC_BATCH, D_MODEL), F32),
                   jax.ShapeDtypeStruct((DEC_BATCH, nbuf * D_CONV), F32)],
        compiler_params=pltpu.CompilerParams(dimension_semantics=("arbitrary",)),
        name="conv_sample",
    )(glu, st, dww, dwb, lng, lnb, wout)


def _qkv_act(c):
    c = _silu(c)
    qs, ks, vs = [], [], []
    for h in range(N_HEADS):
        q = c[:, h * HEAD_K:(h + 1) * HEAD_K]
        k = c[:, D_DELTA_K + h * HEAD_K:D_DELTA_K + (h + 1) * HEAD_K]
        qs.append(q * lax.rsqrt(jnp.sum(q * q, axis=-1, keepdims=True) + EPS) * (HEAD_K ** -0.5))
        ks.append(k * lax.rsqrt(jnp.sum(k * k, axis=-1, keepdims=True) + EPS))
        vs.append(c[:, 2 * D_DELTA_K + h * HEAD_V:2 * D_DELTA_K + (h + 1) * HEAD_V])
    return qs, ks, vs


def _gate_norm(o, ong, z):
    o = o * lax.rsqrt(jnp.mean(o * o, axis=-1, keepdims=True) + EPS) * ong
    return o * _silu(z)


def _gdn_pre_kernel(qkv_ref, ab_ref, cw_ref, alog_ref, dtb_ref,
                    qa_ref, gb_ref, m_ref, aq_ref, sq_ref, xbuf):
    t = pl.program_id(0) % SEQ_TILES
    halo = 8

    @pl.when(t == 0)
    def _():
        xbuf[0:halo, :] = jnp.zeros((halo, CONV_DIM), F32)

    @pl.when(t > 0)
    def _():
        xbuf[0:halo, :] = xbuf[SEQ_TILE:SEQ_TILE + halo, :]

    xbuf[halo:halo + SEQ_TILE, :] = qkv_ref[...]

    @pl.when(t == SEQ_TILES - 1)
    def _():
        sq_ref[...] = xbuf[halo + SEQ_TILE - (QK_CONV - 1):halo + SEQ_TILE, :]

    ab = ab_ref[...]
    lane = lax.broadcasted_iota(jnp.int32, (SEQ_TILE, 128), 1)
    gb = jnp.where(lane < N_HEADS,
                   -jnp.exp(alog_ref[...]) * jax.nn.softplus(ab + dtb_ref[...]),
                   jax.nn.sigmoid(ab))
    gb_ref[...] = gb

    cw = cw_ref[...]
    first = halo - (QK_CONV - 1)
    ri = lax.broadcasted_iota(jnp.int32, (CHUNK, 128), 0)
    li = lax.broadcasted_iota(jnp.int32, (CHUNK, 128), 1)
    left = li < CHUNK
    jj = jnp.where(left, li, li - CHUNK)
    strict = ri > jj
    tril = ri >= jj
    r64 = lax.broadcasted_iota(jnp.int32, (CHUNK, CHUNK), 0)
    c64 = lax.broadcasted_iota(jnp.int32, (CHUNK, CHUNK), 1)
    csum = jnp.where(r64 >= c64, 1.0, 0.0).astype(F32)
    for c in range(SEQ_TILE // CHUNK):
        r0 = c * CHUNK
        acc = cw[0:1, :] * xbuf[r0 + first:r0 + first + CHUNK, :]
        for j in range(1, QK_CONV):
            acc = acc + cw[j:j + 1, :] * xbuf[r0 + first + j:r0 + first + j + CHUNK, :]
        qs, ks, vs = _qkv_act(acc)
        for h in range(N_HEADS):
            qa_ref[r0:r0 + CHUNK, h * HEAD_K:(h + 1) * HEAD_K] = qs[h]
            qa_ref[r0:r0 + CHUNK, D_DELTA_K + h * HEAD_K:D_DELTA_K + (h + 1) * HEAD_K] = ks[h]
            qa_ref[r0:r0 + CHUNK, 2 * D_DELTA_K + h * HEAD_V:2 * D_DELTA_K + (h + 1) * HEAD_V] = vs[h]
        gbc = gb[r0:r0 + CHUNK, :]
        gc = _hdot(csum, gbc)
        gct = gc.T
        for hp in range(N_HEADS // 2):
            h0, h1 = 2 * hp, 2 * hp + 1
            lhs = jnp.concatenate(
                [ks[h0] * gbc[:, N_HEADS + h0:N_HEADS + h0 + 1],
                 ks[h1] * gbc[:, N_HEADS + h1:N_HEADS + h1 + 1], qs[h0], qs[h1]], axis=0)
            rhs = jnp.concatenate([ks[h0], ks[h1]], axis=0)
            prod = lax.dot_general(lhs.astype(BF16), rhs.astype(BF16), (((1,), (1,)), ((), ())),
                                   preferred_element_type=F32)
            kk = jnp.where(left, prod[0:CHUNK], prod[CHUNK:2 * CHUNK])
            qk = jnp.where(left, prod[2 * CHUNK:3 * CHUNK], prod[3 * CHUNK:4 * CHUNK])
            gcol = jnp.where(left, gc[:, h0:h0 + 1], gc[:, h1:h1 + 1])
            grow = jnp.concatenate([gct[h0:h0 + 1, :], gct[h1:h1 + 1, :]], axis=1)
            diff = gcol - grow
            slab = 2 * c + hp
            m_ref[slab * M_PITCH:slab * M_PITCH + CHUNK, :] = jnp.where(
                strict, kk * jnp.exp(jnp.where(strict, diff, 0.0)), 0.0)
            m_ref[slab * M_PITCH + CHUNK:(slab + 1) * M_PITCH, :] = jnp.zeros(
                (M_PITCH - CHUNK, 128), F32)
            aq_ref[slab * CHUNK:(slab + 1) * CHUNK, :] = jnp.where(
                tril, qk * jnp.exp(jnp.where(tril, diff, 0.0)), 0.0)


def _gdn_pre(qkv, ab, cw, alog, dtb):
    tile = lambda w: pl.BlockSpec((SEQ_TILE, w), lambda i: (i, 0))
    vec = pl.BlockSpec((1, 128), lambda i: (0, 0))
    return pl.pallas_call(
        _gdn_pre_kernel,
        grid=(PROMPT_TILES,),
        in_specs=[tile(CONV_DIM), tile(128),
                  pl.BlockSpec((QK_CONV, CONV_DIM), lambda i: (0, 0)), vec, vec],
        out_specs=[tile(CONV_DIM), tile(128),
                   pl.BlockSpec((M_ROWS_TILE, 128), lambda i: (i, 0)),
                   pl.BlockSpec((SLABS_TILE * CHUNK, 128), lambda i: (i, 0)),
                   pl.BlockSpec((None, QK_CONV - 1, CONV_DIM), lambda i: (i // SEQ_TILES, 0, 0))],
        out_shape=[jax.ShapeDtypeStruct((N_PROMPT, CONV_DIM), F32),
                   jax.ShapeDtypeStruct((N_PROMPT, 128), F32),
                   jax.ShapeDtypeStruct((PROMPT_TILES * M_ROWS_TILE, 128), F32),
                   jax.ShapeDtypeStruct((PROMPT_TILES * SLABS_TILE * CHUNK, 128), F32),
                   jax.ShapeDtypeStruct((BATCH, QK_CONV - 1, CONV_DIM), F32)],
        scratch_shapes=[pltpu.VMEM((SEQ_TILE + 8, CONV_DIM), F32)],
        compiler_params=pltpu.CompilerParams(dimension_semantics=("arbitrary",)),
        name="gdn_pre",
    )(qkv, ab, cw, alog, dtb)


def _gdn_solve_kernel(m_ref, t_ref, mt, xt):
    def to_lanes(i, c):
        g = m_ref[pl.ds(i, SOLVE_SLABS, stride=M_PITCH), :]
        mt[pl.ds(pl.multiple_of(i * 128, 128), 128), :] = g.T
        return c

    lax.fori_loop(0, CHUNK, to_lanes, 0)

    row = lax.broadcasted_iota(jnp.int32, (CHUNK, SOLVE_SLABS), 0)

    def solve_row(i, c):
        base = pl.multiple_of(i * 128, 128)
        unit = jnp.where(row == i, 1.0, 0.0).astype(F32)

        def step(j, acc):
            a0, a1 = acc
            xb = pl.multiple_of(j * 128, 128)
            m0 = mt[pl.ds(base + j, 1), :]
            m1 = mt[pl.ds(base + CHUNK + j, 1), :]
            a0 = a0 - m0 * xt[pl.ds(xb, CHUNK), :]
            a1 = a1 - m1 * xt[pl.ds(xb + CHUNK, CHUNK), :]
            return a0, a1

        a0, a1 = lax.fori_loop(0, i, step, (unit, unit))
        xt[pl.ds(base, CHUNK), :] = a0
        xt[pl.ds(base + CHUNK, CHUNK), :] = a1
        return c

    lax.fori_loop(0, CHUNK, solve_row, 0)

    def from_lanes(i, c):
        x = xt[pl.ds(pl.multiple_of(i * 128, 128), 128), :]
        t_ref[pl.ds(i, SOLVE_SLABS, stride=M_PITCH), :] = x.T
        return c

    lax.fori_loop(0, CHUNK, from_lanes, 0)
    for k in range(CHUNK, M_PITCH):
        t_ref[pl.ds(k, SOLVE_SLABS, stride=M_PITCH), :] = jnp.zeros((SOLVE_SLABS, 128), F32)


def _gdn_solve(m):
    rows = SOLVE_SLABS * M_PITCH
    return pl.pallas_call(
        _gdn_solve_kernel,
        grid=(PROMPT_TILES * SLABS_TILE // SOLVE_SLABS,),
        in_specs=[pl.BlockSpec((rows, 128), lambda i: (i, 0))],
        out_specs=pl.BlockSpec((rows, 128), lambda i: (i, 0)),
        out_shape=jax.ShapeDtypeStruct(m.shape, F32),
        scratch_shapes=[pltpu.VMEM((CHUNK * 128, SOLVE_SLABS), F32),
                        pltpu.VMEM((CHUNK * 128, SOLVE_SLABS), F32)],
        compiler_params=pltpu.CompilerParams(dimension_semantics=("arbitrary",)),
        name="gdn_solve",
    )(m)


def _gdn_scan_kernel(qa_ref, gb_ref, z_ref, tm_ref, aq_ref, ys_ref, ong_ref, wout_ref,
                     yb_ref, ss_ref, obuf, s_scr):
    i = pl.program_id(0)
    pl.when(i < PROMPT_TILES)(functools.partial(
        _gdn_scan_tile, i % SEQ_TILES, qa_ref, gb_ref, z_ref, tm_ref, aq_ref, ong_ref, wout_ref,
        yb_ref, ss_ref, obuf, s_scr))
    pl.when(i == PROMPT_TILES)(functools.partial(_sample_rows_tile, ys_ref, yb_ref))


def _gdn_scan_tile(t, qa_ref, gb_ref, z_ref, tm_ref, aq_ref, ong_ref, wout_ref,
                   yb_ref, ss_ref, obuf, s_scr):
    @pl.when(t == 0)
    def _():
        s_scr[...] = jnp.zeros((N_HEADS, HEAD_K, HEAD_V), F32)

    r64 = lax.broadcasted_iota(jnp.int32, (CHUNK, CHUNK), 0)
    c64 = lax.broadcasted_iota(jnp.int32, (CHUNK, CHUNK), 1)
    csum = jnp.where(r64 >= c64, 1.0, 0.0).astype(F32)
    left = lax.broadcasted_iota(jnp.int32, (CHUNK, 128), 1) < CHUNK
    ong = ong_ref[...]

    def pair_diag(x):
        return jnp.concatenate([jnp.where(left, x, 0.0), jnp.where(left, 0.0, x)], axis=0)

    for c in range(SEQ_TILE // CHUNK):
        r0 = c * CHUNK
        gbc = gb_ref[r0:r0 + CHUNK, :]
        gc = _hdot(csum, gbc)
        eg = jnp.exp(gc)
        glast = gc[CHUNK - 1:CHUNK, :]
        kdec = jnp.exp(glast - gc)
        dl = jnp.exp(glast)
        for hp in range(N_HEADS // 2):
            heads = (2 * hp, 2 * hp + 1)
            slab = 2 * c + hp
            tm = pair_diag(tm_ref[slab * M_PITCH:slab * M_PITCH + CHUNK, :])
            aq = pair_diag(aq_ref[slab * CHUNK:(slab + 1) * CHUNK, :])
            qg, kd, rhs = [], [], []
            for h in heads:
                q = qa_ref[r0:r0 + CHUNK, h * HEAD_K:(h + 1) * HEAD_K]
                k = qa_ref[r0:r0 + CHUNK, D_DELTA_K + h * HEAD_K:D_DELTA_K + (h + 1) * HEAD_K]
                v = qa_ref[r0:r0 + CHUNK, 2 * D_DELTA_K + h * HEAD_V:2 * D_DELTA_K + (h + 1) * HEAD_V]
                beta = gbc[:, N_HEADS + h:N_HEADS + h + 1]
                qg.append(q * eg[:, h:h + 1])
                kd.append(k * kdec[:, h:h + 1])
                rhs.append(jnp.concatenate([v * beta, k * beta * eg[:, h:h + 1]], axis=1))
            uw = _bdot(tm, jnp.concatenate(rhs, axis=0))
            vnews, qss = [], []
            for n, h in enumerate(heads):
                sl = slice(n * CHUNK, (n + 1) * CHUNK)
                s = s_scr[h]
                wq = _bdot(jnp.concatenate([uw[sl, HEAD_V:], qg[n]], axis=0), s)
                vnew = uw[sl, :HEAD_V] - wq[:CHUNK]
                vnews.append(vnew)
                qss.append(wq[CHUNK:])
                s_scr[h] = s * dl[:, h:h + 1] + lax.dot_general(
                    kd[n].astype(BF16), vnew.astype(BF16), (((0,), (0,)), ((), ())),
                    preferred_element_type=F32)
            o = jnp.concatenate(qss, axis=0) + _bdot(aq, jnp.concatenate(vnews, axis=0))
            for n, h in enumerate(heads):
                zh = z_ref[r0:r0 + CHUNK, h * HEAD_V:(h + 1) * HEAD_V]
                obuf[r0:r0 + CHUNK, h * HEAD_V:(h + 1) * HEAD_V] = _gate_norm(
                    o[n * CHUNK:(n + 1) * CHUNK], ong, zh)

    yb_ref[...] = _bdot(obuf[...], wout_ref[...])

    @pl.when(t == SEQ_TILES - 1)
    def _():
        ss_ref[...] = s_scr[...]


def _gdn_scan(qa, gb, z, tm, aq, yb_sample, ong, wout):
    last = PROMPT_TILES - 1
    tile = lambda w: pl.BlockSpec((SEQ_TILE, w), lambda i: (jnp.minimum(i, last), 0))
    return pl.pallas_call(
        _gdn_scan_kernel,
        grid=(PROMPT_TILES + 1,),
        in_specs=[tile(CONV_DIM), tile(128),
                  pl.BlockSpec((SEQ_TILE, D_DELTA_V), lambda i: (i, 0)),
                  pl.BlockSpec((M_ROWS_TILE, 128), lambda i: (jnp.minimum(i, last), 0)),
                  pl.BlockSpec((SLABS_TILE * CHUNK, 128), lambda i: (jnp.minimum(i, last), 0)),
                  pl.BlockSpec((DEC_BATCH, D_MODEL), lambda i: (0, 0)),
                  pl.BlockSpec((1, 128), lambda i: (0, 0)),
                  pl.BlockSpec((D_DELTA_V, D_MODEL), lambda i: (0, 0))],
        out_specs=[pl.BlockSpec((SEQ_TILE, D_MODEL), lambda i: (i, 0)),
                   pl.BlockSpec((None, N_HEADS, HEAD_K, HEAD_V),
                                lambda i: (_seq_of_tile(i), 0, 0, 0))],
        out_shape=[jax.ShapeDtypeStruct((N_TOK, D_MODEL), F32),
                   jax.ShapeDtypeStruct((BATCH, N_HEADS, HEAD_K, HEAD_V), F32)],
        scratch_shapes=[pltpu.VMEM((SEQ_TILE, D_DELTA_V), F32),
                        pltpu.VMEM((N_HEADS, HEAD_K, HEAD_V), F32)],
        compiler_params=pltpu.CompilerParams(dimension_semantics=("arbitrary",)),
        name="gdn_scan",
    )(qa, gb, z, tm, aq, yb_sample, ong, wout)


def _gdn_sample_kernel(qkv_ref, z_ref, ab_ref, st_ref, s_ref, cw_ref, alog_ref, dtb_ref,
                       ong_ref, wout_ref, yb_ref, nst_ref, ns_ref):
    x = qkv_ref[...]
    cw = cw_ref[...]
    nbuf = QK_CONV - 1
    acc = cw[nbuf:nbuf + 1, :] * x
    for j in range(nbuf):
        acc = acc + cw[j:j + 1, :] * st_ref[:, j * CONV_DIM:(j + 1) * CONV_DIM]
    nst_ref[:, 0:(nbuf - 1) * CONV_DIM] = st_ref[:, CONV_DIM:nbuf * CONV_DIM]
    nst_ref[:, (nbuf - 1) * CONV_DIM:nbuf * CONV_DIM] = x
    qs, ks, vs = _qkv_act(acc)
    ab = ab_ref[...]
    g = -jnp.exp(alog_ref[...]) * jax.nn.softplus(ab + dtb_ref[...])
    beta = jax.nn.sigmoid(ab)
    eg = jnp.exp(g)
    z = z_ref[...]
    outs = []
    for h in range(N_HEADS):
        bh = beta[:, N_HEADS + h:N_HEADS + h + 1]
        egh = eg[:, h:h + 1]
        kb = ks[h] * bh
        w = kb * egh
        qg = qs[h] * egh
        u = vs[h] * bh
        qk = jnp.sum(qs[h] * ks[h], axis=-1, keepdims=True)
        o_rows = []
        for s in range(DEC_BLK):
            st = s_ref[s, h]
            wq = _bdot(jnp.concatenate([w[s:s + 1], qg[s:s + 1]], axis=0), st)
            vnew = u[s:s + 1] - wq[0:1]
            o_rows.append(wq[1:2] + qk[s:s + 1] * vnew)
            ns_ref[s, h] = st * egh[s:s + 1] + lax.dot_general(
                ks[h][s:s + 1].astype(BF16), vnew.astype(BF16), (((0,), (0,)), ((), ())),
                preferred_element_type=F32)
        o = jnp.concatenate(o_rows, axis=0)
        outs.append(_gate_norm(o, ong_ref[...], z[:, h * HEAD_V:(h + 1) * HEAD_V]))
    yb_ref[...] = _bdot(jnp.concatenate(outs, axis=1), wout_ref[...])


def _gdn_sample(qkv, z, ab, st, s0, cw, alog, dtb, ong, wout):
    nbuf = QK_CONV - 1
    blk0 = N_PROMPT // DEC_BLK
    rows = lambda w: pl.BlockSpec((DEC_BLK, w), lambda i: (blk0 + i, 0))
    vec = pl.BlockSpec((1, 128), lambda i: (0, 0))
    sblk = pl.BlockSpec((DEC_BLK, N_HEADS, HEAD_K, HEAD_V), lambda i: (i, 0, 0, 0))
    return pl.pallas_call(
        _gdn_sample_kernel,
        grid=(DEC_BATCH // DEC_BLK,),
        in_specs=[rows(CONV_DIM), rows(D_DELTA_V), rows(128),
                  pl.BlockSpec((DEC_BLK, nbuf * CONV_DIM), lambda i: (i, 0)),
                  sblk,
                  pl.BlockSpec((QK_CONV, CONV_DIM), lambda i: (0, 0)),
                  vec, vec, vec,
                  pl.BlockSpec((D_DELTA_V, D_MODEL), lambda i: (0, 0))],
        out_specs=[pl.BlockSpec((DEC_BLK, D_MODEL), lambda i: (i, 0)),
                   pl.BlockSpec((DEC_BLK, nbuf * CONV_DIM), lambda i: (i, 0)),
                   sblk],
        out_shape=[jax.ShapeDtypeStruct((DEC_BATCH, D_MODEL), F32),
                   jax.ShapeDtypeStruct((DEC_BATCH, nbuf * CONV_DIM), F32),
                   jax.ShapeDtypeStruct((DEC_BATCH, N_HEADS, HEAD_K, HEAD_V), F32)],
        compiler_params=pltpu.CompilerParams(dimension_semantics=("arbitrary",)),
        name="gdn_sample",
    )(qkv, z, ab, st, s0, cw, alog, dtb, ong, wout)


def _router_kernel(x_ref, g_ref, r_ref, h_ref, info_ref):
    h = _rms(x_ref[...], g_ref[...])
    h_ref[...] = h
    logits = _hdot(h, r_ref[...])
    lane = lax.broadcasted_iota(jnp.int32, logits.shape, 1)
    neg = jnp.float32(-jnp.inf)
    logits = jnp.where(lane < N_EXPERTS, logits, neg)
    m1 = jnp.max(logits, axis=-1, keepdims=True)
    i1 = jnp.min(jnp.where(logits == m1, lane, 128), axis=-1, keepdims=True)
    rest = jnp.where(lane == i1, neg, logits)
    m2 = jnp.max(rest, axis=-1, keepdims=True)
    i2 = jnp.min(jnp.where(rest == m2, lane, 128), axis=-1, keepdims=True)
    e = jnp.exp(m2 - m1)
    p1 = 1.0 / (1.0 + e)
    p2 = e / (1.0 + e)
    info = jnp.where(lane == 0, i1.astype(F32), 0.0)
    info = jnp.where(lane == 1, i2.astype(F32), info)
    info = jnp.where(lane == 2, p1, info)
    info = jnp.where(lane == 3, p2, info)
    info_ref[...] = info


def _router(x, g, r_pad):
    row = lambda w: pl.BlockSpec((TOK_TILE, w), lambda i: (i, 0))
    return pl.pallas_call(
        _router_kernel,
        grid=(N_TOK // TOK_TILE,),
        in_specs=[row(D_MODEL), pl.BlockSpec((1, D_MODEL), lambda i: (0, 0)),
                  pl.BlockSpec((D_MODEL, 128), lambda i: (0, 0))],
        out_specs=[row(D_MODEL), row(128)],
        out_shape=[jax.ShapeDtypeStruct((N_TOK, D_MODEL), F32),
                   jax.ShapeDtypeStruct((N_TOK, 128), F32)],
        compiler_params=pltpu.CompilerParams(dimension_semantics=("arbitrary",)),
        name="moe_router",
    )(x, g, r_pad)


def _dispatch_kernel(dest_ref, xs_in_ref, h_ref, xs_ref, sem):
    del xs_in_ref
    i = pl.program_id(0)
    base = i * (2 * TOK_TILE)

    def copy(r, k):
        return pltpu.make_async_copy(
            h_ref.at[pl.ds(r, 1), :], xs_ref.at[pl.ds(dest_ref[base + 2 * r + k], 1), :], sem)

    def start(r, c):
        copy(r, 0).start()
        copy(r, 1).start()
        return c

    def wait(r, c):
        copy(r, 0).wait()
        copy(r, 1).wait()
        return c

    lax.fori_loop(0, TOK_TILE, start, 0)
    lax.fori_loop(0, TOK_TILE, wait, 0)


def _dispatch(dest, xs0, h):
    return pl.pallas_call(
        _dispatch_kernel,
        grid_spec=pltpu.PrefetchScalarGridSpec(
            num_scalar_prefetch=1,
            grid=(N_TOK // TOK_TILE,),
            in_specs=[pl.BlockSpec(memory_space=pl.ANY),
                      pl.BlockSpec((TOK_TILE, D_MODEL), lambda i, d: (i, 0))],
            out_specs=pl.BlockSpec(memory_space=pl.ANY),
            scratch_shapes=[pltpu.SemaphoreType.DMA(())]),
        out_shape=jax.ShapeDtypeStruct((MOE_ROWS, D_MODEL), F32),
        input_output_aliases={1: 0},
        compiler_params=pltpu.CompilerParams(dimension_semantics=("arbitrary",)),
        name="moe_dispatch",
    )(dest, xs0, h)


def _experts_kernel(te_ref, tv_ref, x_ref, wg_ref, wu_ref, wd_ref, o_ref):
    i = pl.program_id(0)

    @pl.when(tv_ref[i] > 0)
    def _():
        o_ref[...] = _swiglu_rows(x_ref[...].astype(BF16), wg_ref, wu_ref, wd_ref)

    @pl.when(tv_ref[i] == 0)
    def _():
        o_ref[...] = jnp.zeros((MOE_TILE, D_MODEL), F32)


def _experts(tile_expert, tile_valid, xs, wg, wu, wd):
    row = pl.BlockSpec((MOE_TILE, D_MODEL), lambda i, te, tv: (i, 0))
    return pl.pallas_call(
        _experts_kernel,
        grid_spec=pltpu.PrefetchScalarGridSpec(
            num_scalar_prefetch=2,
            grid=(MOE_ROWS // MOE_TILE,),
            in_specs=[row,
                      pl.BlockSpec((None, D_MODEL, D_FF), lambda i, te, tv: (te[i], 0, 0)),
                      pl.BlockSpec((None, D_MODEL, D_FF), lambda i, te, tv: (te[i], 0, 0)),
                      pl.BlockSpec((None, D_FF, D_MODEL), lambda i, te, tv: (te[i], 0, 0))],
            out_specs=row),
        out_shape=jax.ShapeDtypeStruct((MOE_ROWS, D_MODEL), F32),
        compiler_params=pltpu.CompilerParams(
            dimension_semantics=("arbitrary",), vmem_limit_bytes=VMEM_BIG),
        name="moe_experts",
    )(tile_expert, tile_valid, xs, wg, wu, wd)


def _combine_kernel(dest_ref, x_ref, info_ref, g_ref, ys_ref, o_ref, ybuf, sem):
    i = pl.program_id(0)
    base = i * (2 * TOK_TILE)

    def copy(r, k):
        return pltpu.make_async_copy(
            ys_ref.at[pl.ds(dest_ref[base + 2 * r + k], 1), :], ybuf.at[k, pl.ds(r, 1), :], sem)

    def start(r, c):
        copy(r, 0).start()
        copy(r, 1).start()
        return c

    def wait(r, c):
        copy(r, 0).wait()
        copy(r, 1).wait()
        return c

    lax.fori_loop(0, TOK_TILE, start, 0)
    lax.fori_loop(0, TOK_TILE, wait, 0)
    info = info_ref[...]
    x = x_ref[...] + info[:, 2:3] * ybuf[0] + info[:, 3:4] * ybuf[1]
    o_ref[...] = _rms(x, g_ref[...])


def _combine(dest, x, info, g, ys):
    return pl.pallas_call(
        _combine_kernel,
        grid_spec=pltpu.PrefetchScalarGridSpec(
            num_scalar_prefetch=1,
            grid=(N_TOK // TOK_TILE,),
            in_specs=[pl.BlockSpec((TOK_TILE, D_MODEL), lambda i, d: (i, 0)),
                      pl.BlockSpec((TOK_TILE, 128), lambda i, d: (i, 0)),
                      pl.BlockSpec((1, D_MODEL), lambda i, d: (0, 0)),
                      pl.BlockSpec(memory_space=pl.ANY)],
            out_specs=pl.BlockSpec((TOK_TILE, D_MODEL), lambda i, d: (i, 0)),
            scratch_shapes=[pltpu.VMEM((2, TOK_TILE, D_MODEL), F32),
                            pltpu.SemaphoreType.DMA(())]),
        out_shape=jax.ShapeDtypeStruct((N_TOK, D_MODEL), F32),
        compiler_params=pltpu.CompilerParams(dimension_semantics=("arbitrary",)),
        name="moe_combine",
    )(dest, x, info, g, ys)


def _moe_layer(x, g2, router_w, wg, wu, wd, final_g):
    r_pad = jnp.zeros((D_MODEL, 128), F32).at[:, :N_EXPERTS].set(router_w)
    h, info = _router(x, g2, r_pad)
    expert = info[:, 0:2].astype(jnp.int32).reshape(-1)
    onehot = (expert[:, None] == jnp.arange(N_EXPERTS, dtype=jnp.int32)[None, :]).astype(jnp.int32)
    csum = jnp.cumsum(onehot, axis=0)
    counts = csum[-1]
    rank = jnp.sum((csum - onehot) * onehot, axis=1)
    padded = ((counts + MOE_TILE - 1) // MOE_TILE) * MOE_TILE
    ends = jnp.cumsum(padded)
    starts = ends - padded
    dest = (jnp.sum(onehot * starts[None, :], axis=1) + rank).astype(jnp.int32)
    tile_start = jnp.arange(MOE_ROWS // MOE_TILE, dtype=jnp.int32) * MOE_TILE
    tile_expert = jnp.minimum(
        jnp.sum((tile_start[:, None] >= ends[None, :]).astype(jnp.int32), axis=1), N_EXPERTS - 1)
    tile_valid = (tile_start < ends[-1]).astype(jnp.int32)
    last_used = jnp.max(jnp.where(tile_valid > 0, tile_expert, 0))
    tile_expert = jnp.where(tile_valid > 0, tile_expert, last_used).astype(jnp.int32)

    xs = _dispatch(dest, jnp.zeros((MOE_ROWS, D_MODEL), F32), h)
    ys = _experts(tile_expert, tile_valid, xs, wg, wu, wd)
    return _combine(dest, x, info, final_g, ys)


def _rearranged_w_in(w_in):
    off_qkv = 2 * D_CONV
    off_z = off_qkv + CONV_DIM
    off_a = off_z + D_DELTA_V
    off_ga = off_a + 2 * N_HEADS
    off_gb = off_ga + D_MODEL
    return jnp.concatenate(
        [w_in[:, :off_a], w_in[:, off_ga:off_gb], w_in[:, off_gb:], w_in[:, off_a:off_ga],
         jnp.zeros((D_MODEL, 128 - 2 * N_HEADS), w_in.dtype)], axis=1).astype(BF16)


def _lane_vec(v, offset=0):
    return jnp.zeros((1, 128), F32).at[0, offset:offset + v.shape[0]].set(v)


def kernel(x_prompt, x_sample, state_delta, state_qkv_conv, state_dwconv, norm1_g, w_in, qkv_conv_w, a_log, dt_bias, o_norm_g, w_delta_out, dw_w, dw_b, ln_g, ln_b, w_conv_out, w_o, norm2_g, ffn_w_gate, ffn_w_up, ffn_w_down, router_w, exp_w_gate, exp_w_up, exp_w_down, final_norm_g):
    x = jnp.concatenate([x_prompt.reshape(N_PROMPT, D_MODEL), x_sample.reshape(DEC_BATCH, D_MODEL),
                         jnp.zeros((TOK_TILE - DEC_BATCH, D_MODEL), F32)], axis=0)
    s_p, q_p, d_p, s_s, q_s, d_s = [], [], [], [], [], []
    for l in range(DEPTH):
        glu, qkv, z, ga, gb, ab = _norm_proj(x, norm1_g[l][None, :], _rearranged_w_in(w_in[l]))
        dwb, lng, lnb = dw_b[l][None, :], ln_g[l][None, :], ln_b[l][None, :]
        wco = w_conv_out[l].astype(BF16)
        ya_s, dst_s = _conv_sample(glu, state_dwconv[l].reshape(DEC_BATCH, -1), dw_w[l], dwb, lng, lnb, wco)
        ya, dst_p = _conv_prompt(glu, ya_s, dw_w[l], dwb, lng, lnb, wco)
        alog, dtb, ong = _lane_vec(a_log[l]), _lane_vec(dt_bias[l]), o_norm_g[l][None, :]
        wdo = w_delta_out[l].astype(BF16)
        yb_s, qst_s, sst_s = _gdn_sample(qkv, z, ab, state_qkv_conv[l].reshape(DEC_BATCH, -1),
                                         state_delta[l], qkv_conv_w[l], alog, dtb, ong, wdo)
        qa, gbeta, m, aq, qst_p = _gdn_pre(qkv, ab, qkv_conv_w[l], alog, dtb)
        yb, sst_p = _gdn_scan(qa, gbeta, z, _gdn_solve(m), aq, yb_s, ong, wdo)
        x = _merge(x, ya, yb, ga, gb, w_o[l].astype(BF16))
        i = l // 2
        if l % 2 == 0:
            x = _ffn(x, norm2_g[l][None, :], ffn_w_gate[i].astype(BF16), ffn_w_up[i].astype(BF16),
                     ffn_w_down[i].astype(BF16))
        else:
            x = _moe_layer(x, norm2_g[l][None, :], router_w[i], exp_w_gate[i].astype(BF16),
                           exp_w_up[i].astype(BF16), exp_w_down[i].astype(BF16),
                           final_norm_g[None, :])
        s_p.append(sst_p)
        q_p.append(qst_p)
        d_p.append(dst_p)
        s_s.append(sst_s)
        q_s.append(qst_s.reshape(DEC_BATCH, QK_CONV - 1, CONV_DIM))
        d_s.append(dst_s.reshape(DEC_BATCH, DW_WIDTH - 1, D_CONV))
    y_prompt = x[:N_PROMPT].reshape(BATCH, SEQ, D_MODEL)
    y_sample = x[N_PROMPT:N_PROMPT + DEC_BATCH].reshape(DEC_BATCH, 1, D_MODEL)
    return (y_prompt, y_sample, jnp.stack(s_p), jnp.stack(q_p), jnp.stack(d_p),
            jnp.stack(s_s), jnp.stack(q_s), jnp.stack(d_s))
```

```python
import functools

import jax
import jax.numpy as jnp
from jax import lax
from jax.experimental import pallas as pl
from jax.experimental.pallas import tpu as pltpu

F32 = jnp.float32
BF16 = jnp.bfloat16
HIGHEST = lax.Precision.HIGHEST

D_MODEL = 1024
BATCH = 8
SEQ = 2048
DEPTH = 2
DEC_BATCH = 128
D_CONV = 512
DW_WIDTH = 31
N_HEADS = 4
HEAD_K = 128
HEAD_V = 128
QK_CONV = 4
D_DELTA_K = N_HEADS * HEAD_K
D_DELTA_V = N_HEADS * HEAD_V
CONV_DIM = 2 * D_DELTA_K + D_DELTA_V
D_FF = 2816
N_EXPERTS = 8
EPS = 1e-6

N_PROMPT = BATCH * SEQ
TOK_TILE = 512
SEQ_TILE = TOK_TILE
N_TOK = N_PROMPT + TOK_TILE
SEQ_TILES = SEQ // SEQ_TILE
PROMPT_TILES = N_PROMPT // TOK_TILE
HALF_TILES = PROMPT_TILES // 2
HALF_SEQS = BATCH // 2
CHUNK = 64
M_PITCH = 68
SLABS_TILE = 2 * (SEQ_TILE // CHUNK)
M_ROWS_TILE = SLABS_TILE * M_PITCH
SOLVE_SLABS = 128
GC_LANE = 8
ROW_BLK = 64
FF_HALF = D_FF // 2
MOE_TILE = 256
MOE_ROWS = ((2 * N_TOK + N_EXPERTS * (MOE_TILE - 1)) // MOE_TILE + 1) * MOE_TILE
DEC_BLK = 8

C_GLU, C_QKV, C_Z, C_GA, C_GB, C_AB = 0, 1024, 2560, 3072, 4096, 5120
N_PROJ = 5248

VMEM_BIG = 56 * 1024 * 1024


def _bdot(a, b):
    return jnp.dot(a.astype(BF16), b.astype(BF16), preferred_element_type=F32)


def _hdot(a, b):
    return jnp.dot(a, b, preferred_element_type=F32, precision=HIGHEST)


def _rms(x, g):
    return x * lax.rsqrt(jnp.mean(x * x, axis=-1, keepdims=True) + EPS) * g


def _silu(x):
    return x * jax.nn.sigmoid(x)


def _resident(shape):
    return pl.BlockSpec(shape, lambda *_: (0,) * len(shape), pipeline_mode=pl.Buffered(1))


def _stream_x_specs(x):
    if isinstance(x, tuple):
        last = PROMPT_TILES - 1
        return ([pl.BlockSpec((TOK_TILE, D_MODEL), lambda i: (jnp.minimum(i, last), 0)),
                 pl.BlockSpec((DEC_BATCH, D_MODEL), lambda i: (0, 0))], list(x))
    return [pl.BlockSpec((TOK_TILE, D_MODEL), lambda i: (i, 0))], [x]


def _stream_x_tile(x_refs, xbuf):
    if len(x_refs) == 1:
        return x_refs[0][...]
    xp_ref, xs_ref = x_refs
    i = pl.program_id(0)

    @pl.when(i < PROMPT_TILES)
    def _():
        xbuf[...] = xp_ref[...]

    @pl.when(i == PROMPT_TILES)
    def _():
        xbuf[0:DEC_BATCH, :] = xs_ref[...]
        xbuf[DEC_BATCH:, :] = jnp.zeros((TOK_TILE - DEC_BATCH, D_MODEL), F32)

    return xbuf[...]


def _norm_proj_kernel(n_x, *refs):
    x_refs = refs[:n_x]
    g_ref, w_ref, glu_ref, qkv_ref, z_ref, ga_ref, gb_ref, ab_ref, xbuf = refs[n_x:]
    h = _rms(_stream_x_tile(x_refs, xbuf), g_ref[...]).astype(BF16)

    def proj(lo, hi):
        return jnp.dot(h, w_ref[:, lo:hi], preferred_element_type=F32)

    glu_ref[...] = proj(C_GLU, C_QKV)
    qkv_ref[...] = proj(C_QKV, C_Z)
    z_ref[...] = proj(C_Z, C_GA)
    ga_ref[...] = proj(C_GA, C_GB)
    gb_ref[...] = proj(C_GB, C_AB)
    ab_ref[...] = proj(C_AB, N_PROJ)


def _norm_proj(x, g, w_all):
    row = lambda w: pl.BlockSpec((TOK_TILE, w), lambda i: (i, 0))
    widths = (1024, CONV_DIM, D_DELTA_V, D_MODEL, D_MODEL, 128)
    x_specs, x_ops = _stream_x_specs(x)
    return pl.pallas_call(
        functools.partial(_norm_proj_kernel, len(x_ops)),
        grid=(N_TOK // TOK_TILE,),
        in_specs=x_specs + [pl.BlockSpec((1, D_MODEL), lambda i: (0, 0)),
                            _resident((D_MODEL, N_PROJ))],
        out_specs=[row(w) for w in widths],
        out_shape=[jax.ShapeDtypeStruct((N_TOK, w), F32) for w in widths],
        scratch_shapes=[pltpu.VMEM((TOK_TILE, D_MODEL), F32)],
        compiler_params=pltpu.CompilerParams(
            dimension_semantics=("arbitrary",), vmem_limit_bytes=VMEM_BIG),
        name="norm_proj",
    )(*x_ops, g, w_all)


def _merge_kernel(n_x, *refs):
    x_refs = refs[:n_x]
    ya_ref, yb_ref, ga_ref, gb_ref, wo_ref, o_ref, xbuf = refs[n_x:]
    m = jax.nn.sigmoid(ga_ref[...]) * ya_ref[...] + jax.nn.sigmoid(gb_ref[...]) * yb_ref[...]
    o_ref[...] = _stream_x_tile(x_refs, xbuf) + _bdot(m, wo_ref[...])


def _merge(x, ya, yb, ga, gb, wo):
    row = pl.BlockSpec((TOK_TILE, D_MODEL), lambda i: (i, 0))
    x_specs, x_ops = _stream_x_specs(x)
    yb_spec = pl.BlockSpec((None, TOK_TILE, D_MODEL),
                           lambda i: (i // HALF_TILES - i // (2 * HALF_TILES),
                                      i - HALF_TILES * (i // HALF_TILES - i // (2 * HALF_TILES)), 0))
    return pl.pallas_call(
        functools.partial(_merge_kernel, len(x_ops)),
        grid=(N_TOK // TOK_TILE,),
        in_specs=x_specs + [row, yb_spec, row, row,
                            pl.BlockSpec((D_MODEL, D_MODEL), lambda i: (0, 0))],
        out_specs=row,
        out_shape=jax.ShapeDtypeStruct((N_TOK, D_MODEL), F32),
        scratch_shapes=[pltpu.VMEM((TOK_TILE, D_MODEL), F32)],
        compiler_params=pltpu.CompilerParams(dimension_semantics=("arbitrary",)),
        name="merge",
    )(*x_ops, ya, yb, ga, gb, wo)


def _swiglu_rows(h, wg_ref, wu_ref, wd_ref):
    acc = None
    for lo in (0, FF_HALF):
        a = jnp.dot(h, wg_ref[:, lo:lo + FF_HALF], preferred_element_type=F32)
        b = jnp.dot(h, wu_ref[:, lo:lo + FF_HALF], preferred_element_type=F32)
        part = jnp.dot((_silu(a) * b).astype(BF16), wd_ref[lo:lo + FF_HALF, :],
                       preferred_element_type=F32)
        acc = part if acc is None else acc + part
    return acc


def _ffn_kernel(x_ref, g_ref, wg_ref, wu_ref, wd_ref, o_ref):
    x = x_ref[...]
    h = _rms(x, g_ref[...]).astype(BF16)
    o_ref[...] = x + _swiglu_rows(h, wg_ref, wu_ref, wd_ref)


def _ffn(x, g, wg, wu, wd):
    row = pl.BlockSpec((TOK_TILE, D_MODEL), lambda i: (i, 0))
    return pl.pallas_call(
        _ffn_kernel,
        grid=(N_TOK // TOK_TILE,),
        in_specs=[row, pl.BlockSpec((1, D_MODEL), lambda i: (0, 0)),
                  _resident((D_MODEL, D_FF)), _resident((D_MODEL, D_FF)),
                  _resident((D_FF, D_MODEL))],
        out_specs=row,
        out_shape=jax.ShapeDtypeStruct((N_TOK, D_MODEL), F32),
        compiler_params=pltpu.CompilerParams(
            dimension_semantics=("arbitrary",), vmem_limit_bytes=VMEM_BIG),
        name="ffn_dense",
    )(x, g, wg, wu, wd)


def _conv_tail(c, dwb, lng, lnb):
    c = c + dwb
    mu = jnp.mean(c, axis=-1, keepdims=True)
    var = jnp.mean(jnp.square(c - mu), axis=-1, keepdims=True)
    c = (c - mu) * lax.rsqrt(var + EPS) * lng + lnb
    return _silu(c)


def _sample_rows_tile(ys_ref, y_ref):
    y_ref[0:DEC_BATCH, :] = ys_ref[...]
    y_ref[DEC_BATCH:, :] = jnp.zeros((TOK_TILE - DEC_BATCH, D_MODEL), F32)


def _conv_prompt_kernel(glu_ref, ys_ref, dww_ref, dwb_ref, lng_ref, lnb_ref, wout_ref,
                        ya_ref, st_ref, ubuf, ushift, cbuf):
    i = pl.program_id(0)
    pl.when(i < PROMPT_TILES)(functools.partial(
        _conv_prompt_tile, i % SEQ_TILES, glu_ref, dww_ref, dwb_ref, lng_ref, lnb_ref, wout_ref,
        ya_ref, st_ref, ubuf, ushift, cbuf))
    pl.when(i == PROMPT_TILES)(functools.partial(_sample_rows_tile, ys_ref, ya_ref))


def _conv_prompt_tile(t, glu_ref, dww_ref, dwb_ref, lng_ref, lnb_ref, wout_ref,
                      ya_ref, st_ref, ubuf, ushift, cbuf):
    halo = 32

    @pl.when(t == 0)
    def _():
        ubuf[0:halo, :] = jnp.zeros((halo, D_CONV), F32)

    @pl.when(t > 0)
    def _():
        ubuf[0:halo, :] = ubuf[SEQ_TILE:SEQ_TILE + halo, :]

    glu = glu_ref[...]
    ubuf[halo:halo + SEQ_TILE, :] = glu[:, :D_CONV] * jax.nn.sigmoid(glu[:, D_CONV:])
    for r in range(1, 8):
        ushift[r - 1, 0:SEQ_TILE + halo - 8, :] = ubuf[r:r + SEQ_TILE + halo - 8, :]
    w = dww_ref[...]
    first = halo - (DW_WIDTH - 1)
    for rb in range(SEQ_TILE // ROW_BLK):
        r0 = rb * ROW_BLK
        acc = jnp.zeros((ROW_BLK, D_CONV), F32)
        for j in range(DW_WIDTH):
            a, r = divmod(first + j, 8)
            if r == 0:
                win = ubuf[r0 + 8 * a:r0 + 8 * a + ROW_BLK, :]
            else:
                win = ushift[r - 1, r0 + 8 * a:r0 + 8 * a + ROW_BLK, :]
            acc = acc + w[j:j + 1, :] * win
        c = _conv_tail(acc, dwb_ref[...], lng_ref[...], lnb_ref[...])
        cbuf[r0:r0 + ROW_BLK, :] = c.astype(BF16)
    ya_ref[...] = jnp.dot(cbuf[...], wout_ref[...], preferred_element_type=F32)

    @pl.when(t == SEQ_TILES - 1)
    def _():
        st_ref[...] = ubuf[halo + SEQ_TILE - (DW_WIDTH - 1):halo + SEQ_TILE, :]


def _seq_of_tile(i):
    return jnp.minimum(i // SEQ_TILES, BATCH - 1)


def _conv_prompt(glu, ya_sample, dww, dwb, lng, lnb, wout):
    vec = pl.BlockSpec((1, D_CONV), lambda i: (0, 0))
    return pl.pallas_call(
        _conv_prompt_kernel,
        grid=(PROMPT_TILES + 1,),
        in_specs=[pl.BlockSpec((SEQ_TILE, 2 * D_CONV), lambda i: (i, 0)),
                  pl.BlockSpec((DEC_BATCH, D_MODEL), lambda i: (0, 0)),
                  pl.BlockSpec((DW_WIDTH, D_CONV), lambda i: (0, 0)),
                  vec, vec, vec,
                  pl.BlockSpec((D_CONV, D_MODEL), lambda i: (0, 0))],
        out_specs=[pl.BlockSpec((SEQ_TILE, D_MODEL), lambda i: (i, 0)),
                   pl.BlockSpec((None, DW_WIDTH - 1, D_CONV), lambda i: (_seq_of_tile(i), 0, 0))],
        out_shape=[jax.ShapeDtypeStruct((N_TOK, D_MODEL), F32),
                   jax.ShapeDtypeStruct((BATCH, DW_WIDTH - 1, D_CONV), F32)],
        scratch_shapes=[pltpu.VMEM((SEQ_TILE + 32, D_CONV), F32),
                        pltpu.VMEM((7, SEQ_TILE + 24, D_CONV), F32),
                        pltpu.VMEM((SEQ_TILE, D_CONV), BF16)],
        compiler_params=pltpu.CompilerParams(dimension_semantics=("arbitrary",)),
        name="conv_prompt",
    )(glu, ya_sample, dww, dwb, lng, lnb, wout)


def _conv_sample_kernel(glu_ref, st_ref, dww_ref, dwb_ref, lng_ref, lnb_ref, wout_ref,
                        ya_ref, nst_ref):
    glu = glu_ref[...]
    u = glu[:, :D_CONV] * jax.nn.sigmoid(glu[:, D_CONV:])
    w = dww_ref[...]
    nbuf = DW_WIDTH - 1
    acc = w[nbuf:nbuf + 1, :] * u
    for j in range(nbuf):
        acc = acc + w[j:j + 1, :] * st_ref[:, j * D_CONV:(j + 1) * D_CONV]
    c = _conv_tail(acc, dwb_ref[...], lng_ref[...], lnb_ref[...])
    ya_ref[...] = _bdot(c, wout_ref[...])
    nst_ref[:, 0:(nbuf - 1) * D_CONV] = st_ref[:, D_CONV:nbuf * D_CONV]
    nst_ref[:, (nbuf - 1) * D_CONV:nbuf * D_CONV] = u


def _conv_sample(glu, st, dww, dwb, lng, lnb, wout):
    nbuf = DW_WIDTH - 1
    blk = N_PROMPT // DEC_BATCH
    vec = pl.BlockSpec((1, D_CONV), lambda i: (0, 0))
    return pl.pallas_call(
        _conv_sample_kernel,
        grid=(1,),
        in_specs=[pl.BlockSpec((DEC_BATCH, 2 * D_CONV), lambda i: (blk, 0)),
                  pl.BlockSpec((DEC_BATCH, nbuf * D_CONV), lambda i: (0, 0)),
                  pl.BlockSpec((DW_WIDTH, D_CONV), lambda i: (0, 0)),
                  vec, vec, vec,
                  pl.BlockSpec((D_CONV, D_MODEL), lambda i: (0, 0))],
        out_specs=[pl.BlockSpec((DEC_BATCH, D_MODEL), lambda i: (0, 0)),
                   pl.BlockSpec((DEC_BATCH, nbuf * D_CONV), lambda i: (0, 0))],
        out_shape=[jax.ShapeDtypeStruct((DEC_BATCH, D_MODEL), F32),
                   jax.ShapeDtypeStruct((DEC_BATCH, nbuf * D_CONV), F32)],
        compiler_params=pltpu.CompilerParams(dimension_semantics=("arbitrary",)),
        name="conv_sample",
    )(glu, st, dww, dwb, lng, lnb, wout)


def _qkv_act(c):
    c = _silu(c)
    qs, ks, vs = [], [], []
    for h in range(N_HEADS):
        q = c[:, h * HEAD_K:(h + 1) * HEAD_K]
        k = c[:, D_DELTA_K + h * HEAD_K:D_DELTA_K + (h + 1) * HEAD_K]
        qs.append(q * lax.rsqrt(jnp.sum(q * q, axis=-1, keepdims=True) + EPS) * (HEAD_K ** -0.5))
        ks.append(k * lax.rsqrt(jnp.sum(k * k, axis=-1, keepdims=True) + EPS))
        vs.append(c[:, 2 * D_DELTA_K + h * HEAD_V:2 * D_DELTA_K + (h + 1) * HEAD_V])
    return qs, ks, vs


def _gate_norm(o, ong, z):
    o = o * lax.rsqrt(jnp.mean(o * o, axis=-1, keepdims=True) + EPS) * ong
    return o * _silu(z)


def _gdn_pre_kernel(qkv_ref, ab_ref, cw_ref, alog_ref, dtb_ref,
                    qa_ref, gb_ref, m_ref, aq_ref, sq_ref, xbuf):
    t = pl.program_id(0) % SEQ_TILES
    halo = 8

    @pl.when(t == 0)
    def _():
        xbuf[0:halo, :] = jnp.zeros((halo, CONV_DIM), F32)

    @pl.when(t > 0)
    def _():
        xbuf[0:halo, :] = xbuf[SEQ_TILE:SEQ_TILE + halo, :]

    xbuf[halo:halo + SEQ_TILE, :] = qkv_ref[...]

    @pl.when(t == SEQ_TILES - 1)
    def _():
        sq_ref[...] = xbuf[halo + SEQ_TILE - (QK_CONV - 1):halo + SEQ_TILE, :]

    ab = ab_ref[...]
    lane = lax.broadcasted_iota(jnp.int32, (SEQ_TILE, 128), 1)
    gb = jnp.where(lane < N_HEADS,
                   -jnp.exp(alog_ref[...]) * jax.nn.softplus(ab + dtb_ref[...]),
                   jax.nn.sigmoid(ab))
    cw = cw_ref[...]
    first = halo - (QK_CONV - 1)
    ri = lax.broadcasted_iota(jnp.int32, (CHUNK, 128), 0)
    li = lax.broadcasted_iota(jnp.int32, (CHUNK, 128), 1)
    left = li < CHUNK
    jj = jnp.where(left, li, li - CHUNK)
    strict = ri > jj
    tril = ri >= jj
    r64 = lax.broadcasted_iota(jnp.int32, (CHUNK, CHUNK), 0)
    c64 = lax.broadcasted_iota(jnp.int32, (CHUNK, CHUNK), 1)
    csum = jnp.where(r64 >= c64, 1.0, 0.0).astype(F32)
    for c in range(SEQ_TILE // CHUNK):
        r0 = c * CHUNK
        acc = cw[0:1, :] * xbuf[r0 + first:r0 + first + CHUNK, :]
        for j in range(1, QK_CONV):
            acc = acc + cw[j:j + 1, :] * xbuf[r0 + first + j:r0 + first + j + CHUNK, :]
        qs, ks, vs = _qkv_act(acc)
        for h in range(N_HEADS):
            qa_ref[r0:r0 + CHUNK, h * HEAD_K:(h + 1) * HEAD_K] = qs[h]
            qa_ref[r0:r0 + CHUNK, D_DELTA_K + h * HEAD_K:D_DELTA_K + (h + 1) * HEAD_K] = ks[h]
            qa_ref[r0:r0 + CHUNK, 2 * D_DELTA_K + h * HEAD_V:2 * D_DELTA_K + (h + 1) * HEAD_V] = vs[h]
        gbc = gb[r0:r0 + CHUNK, :]
        gc = _hdot(csum, gbc)
        gb_ref[r0:r0 + CHUNK, :] = jnp.where(li < GC_LANE, gbc, pltpu.roll(gc, GC_LANE, axis=1))
        gct = gc.T
        for hp in range(N_HEADS // 2):
            h0, h1 = 2 * hp, 2 * hp + 1
            lhs = jnp.concatenate(
                [ks[h0] * gbc[:, N_HEADS + h0:N_HEADS + h0 + 1],
                 ks[h1] * gbc[:, N_HEADS + h1:N_HEADS + h1 + 1], qs[h0], qs[h1]], axis=0)
            rhs = jnp.concatenate([ks[h0], ks[h1]], axis=0)
            prod = lax.dot_general(lhs.astype(BF16), rhs.astype(BF16), (((1,), (1,)), ((), ())),
                                   preferred_element_type=F32)
            kk = jnp.where(left, prod[0:CHUNK], prod[CHUNK:2 * CHUNK])
            qk = jnp.where(left, prod[2 * CHUNK:3 * CHUNK], prod[3 * CHUNK:4 * CHUNK])
            gcol = jnp.where(left, gc[:, h0:h0 + 1], gc[:, h1:h1 + 1])
            grow = jnp.concatenate([gct[h0:h0 + 1, :], gct[h1:h1 + 1, :]], axis=1)
            diff = gcol - grow
            slab = 2 * c + hp
            m_ref[slab * M_PITCH:slab * M_PITCH + CHUNK, :] = jnp.where(
                strict, kk * jnp.exp(jnp.where(strict, diff, 0.0)), 0.0)
            m_ref[slab * M_PITCH + CHUNK:(slab + 1) * M_PITCH, :] = jnp.zeros(
                (M_PITCH - CHUNK, 128), F32)
            aq_ref[slab * CHUNK:(slab + 1) * CHUNK, :] = jnp.where(
                tril, qk * jnp.exp(jnp.where(tril, diff, 0.0)), 0.0)


def _gdn_pre(qkv, ab, cw, alog, dtb):
    tile = lambda w: pl.BlockSpec((SEQ_TILE, w), lambda i: (i, 0))
    vec = pl.BlockSpec((1, 128), lambda i: (0, 0))
    return pl.pallas_call(
        _gdn_pre_kernel,
        grid=(PROMPT_TILES,),
        in_specs=[tile(CONV_DIM), tile(128),
                  pl.BlockSpec((QK_CONV, CONV_DIM), lambda i: (0, 0)), vec, vec],
        out_specs=[tile(CONV_DIM), tile(128),
                   pl.BlockSpec((M_ROWS_TILE, 128), lambda i: (i, 0)),
                   pl.BlockSpec((SLABS_TILE * CHUNK, 128), lambda i: (i, 0)),
                   pl.BlockSpec((None, QK_CONV - 1, CONV_DIM), lambda i: (i // SEQ_TILES, 0, 0))],
        out_shape=[jax.ShapeDtypeStruct((N_PROMPT, CONV_DIM), F32),
                   jax.ShapeDtypeStruct((N_PROMPT, 128), F32),
                   jax.ShapeDtypeStruct((PROMPT_TILES * M_ROWS_TILE, 128), F32),
                   jax.ShapeDtypeStruct((PROMPT_TILES * SLABS_TILE * CHUNK, 128), F32),
                   jax.ShapeDtypeStruct((BATCH, QK_CONV - 1, CONV_DIM), F32)],
        scratch_shapes=[pltpu.VMEM((SEQ_TILE + 8, CONV_DIM), F32)],
        compiler_params=pltpu.CompilerParams(dimension_semantics=("arbitrary",)),
        name="gdn_pre",
    )(qkv, ab, cw, alog, dtb)


def _gdn_solve_kernel(m_ref, t_ref, mt, xt):
    def to_lanes(i, c):
        g = m_ref[pl.ds(i, SOLVE_SLABS, stride=M_PITCH), :]
        mt[pl.ds(pl.multiple_of(i * 128, 128), 128), :] = g.T
        return c

    lax.fori_loop(0, CHUNK, to_lanes, 0, unroll=4)

    row = lax.broadcasted_iota(jnp.int32, (CHUNK, SOLVE_SLABS), 0)

    def solve_row(i, c):
        base = pl.multiple_of(i * 128, 128)
        unit = jnp.where(row == i, 1.0, 0.0).astype(F32)

        def step(j, acc):
            a0, a1 = acc
            xb = pl.multiple_of(j * 128, 128)
            m0 = mt[pl.ds(base + j, 1), :]
            m1 = mt[pl.ds(base + CHUNK + j, 1), :]
            a0 = a0 - m0 * xt[pl.ds(xb, CHUNK), :]
            a1 = a1 - m1 * xt[pl.ds(xb + CHUNK, CHUNK), :]
            return a0, a1

        a0, a1 = lax.fori_loop(0, i, step, (unit, unit))
        xt[pl.ds(base, CHUNK), :] = a0
        xt[pl.ds(base + CHUNK, CHUNK), :] = a1
        return c

    lax.fori_loop(0, CHUNK, solve_row, 0)

    def from_lanes(i, c):
        x = xt[pl.ds(pl.multiple_of(i * 128, 128), 128), :]
        t_ref[pl.ds(i, SOLVE_SLABS, stride=M_PITCH), :] = x.T
        return c

    lax.fori_loop(0, CHUNK, from_lanes, 0, unroll=4)
    for k in range(CHUNK, M_PITCH):
        t_ref[pl.ds(k, SOLVE_SLABS, stride=M_PITCH), :] = jnp.zeros((SOLVE_SLABS, 128), F32)


def _gdn_solve(m):
    rows = SOLVE_SLABS * M_PITCH
    return pl.pallas_call(
        _gdn_solve_kernel,
        grid=(PROMPT_TILES * SLABS_TILE // SOLVE_SLABS,),
        in_specs=[pl.BlockSpec((rows, 128), lambda i: (i, 0))],
        out_specs=pl.BlockSpec((rows, 128), lambda i: (i, 0)),
        out_shape=jax.ShapeDtypeStruct(m.shape, F32),
        scratch_shapes=[pltpu.VMEM((CHUNK * 128, SOLVE_SLABS), F32),
                        pltpu.VMEM((CHUNK * 128, SOLVE_SLABS), F32)],
        compiler_params=pltpu.CompilerParams(dimension_semantics=("arbitrary",)),
        name="gdn_solve",
    )(m)


def _gdn_scan_kernel(qa_ref, gb_ref, z0_ref, z1_ref, tm_ref, aq_ref, ys_ref, ong_ref, wout_ref,
                     yb_ref, ss_ref, obuf, s_scr):
    i = pl.program_id(0)
    pl.when(i < HALF_TILES)(functools.partial(
        _gdn_scan_tile, i % SEQ_TILES, qa_ref, gb_ref, (z0_ref, z1_ref), tm_ref, aq_ref, ong_ref,
        wout_ref, yb_ref, ss_ref, obuf, s_scr))

    @pl.when(i == HALF_TILES)
    def _():
        yb_ref[0] = jnp.zeros((TOK_TILE, D_MODEL), F32)
        _sample_rows_tile(ys_ref, yb_ref.at[1])


def _gdn_scan_tile(t, qa_ref, gb_ref, z_refs, tm_ref, aq_ref, ong_ref, wout_ref,
                   yb_ref, ss_ref, obuf, s_scr):
    @pl.when(t == 0)
    def _():
        s_scr[...] = jnp.zeros((2, N_HEADS, HEAD_K, HEAD_V), F32)

    left = lax.broadcasted_iota(jnp.int32, (CHUNK, 128), 1) < CHUNK
    ong = ong_ref[...]

    def pair_diag(x):
        return jnp.concatenate([jnp.where(left, x, 0.0), jnp.where(left, 0.0, x)], axis=0)

    for c in range(SEQ_TILE // CHUNK):
        r0 = c * CHUNK
        for half in range(2):
            gbc = gb_ref[half, r0:r0 + CHUNK, :]
            eg = jnp.exp(gbc)
            glast = gbc[CHUNK - 1:CHUNK, :]
            kdec = jnp.exp(glast - gbc)
            dl = jnp.exp(glast)
            for hp in range(N_HEADS // 2):
                heads = (2 * hp, 2 * hp + 1)
                slab = 2 * c + hp
                tm = pair_diag(tm_ref[half, slab * M_PITCH:slab * M_PITCH + CHUNK, :])
                aq = pair_diag(aq_ref[half, slab * CHUNK:(slab + 1) * CHUNK, :])
                qg, kd, rhs = [], [], []
                for h in heads:
                    q = qa_ref[half, r0:r0 + CHUNK, h * HEAD_K:(h + 1) * HEAD_K]
                    k = qa_ref[half, r0:r0 + CHUNK, D_DELTA_K + h * HEAD_K:D_DELTA_K + (h + 1) * HEAD_K]
                    v = qa_ref[half, r0:r0 + CHUNK,
                               2 * D_DELTA_K + h * HEAD_V:2 * D_DELTA_K + (h + 1) * HEAD_V]
                    beta = gbc[:, N_HEADS + h:N_HEADS + h + 1]
                    egh = eg[:, GC_LANE + h:GC_LANE + h + 1]
                    qg.append(q * egh)
                    kd.append(k * kdec[:, GC_LANE + h:GC_LANE + h + 1])
                    rhs.append(jnp.concatenate([v * beta, k * beta * egh], axis=1))
                uw = _bdot(tm, jnp.concatenate(rhs, axis=0))
                vnews, qss = [], []
                for n, h in enumerate(heads):
                    sl = slice(n * CHUNK, (n + 1) * CHUNK)
                    s = s_scr[half, h]
                    wq = _bdot(jnp.concatenate([uw[sl, HEAD_V:], qg[n]], axis=0), s)
                    vnew = uw[sl, :HEAD_V] - wq[:CHUNK]
                    vnews.append(vnew)
                    qss.append(wq[CHUNK:])
                    s_scr[half, h] = s * dl[:, GC_LANE + h:GC_LANE + h + 1] + lax.dot_general(
                        kd[n].astype(BF16), vnew.astype(BF16), (((0,), (0,)), ((), ())),
                        preferred_element_type=F32)
                o = jnp.concatenate(qss, axis=0) + _bdot(aq, jnp.concatenate(vnews, axis=0))
                for n, h in enumerate(heads):
                    zh = z_refs[half][r0:r0 + CHUNK, h * HEAD_V:(h + 1) * HEAD_V]
                    obuf[half, r0:r0 + CHUNK, h * HEAD_V:(h + 1) * HEAD_V] = _gate_norm(
                        o[n * CHUNK:(n + 1) * CHUNK], ong, zh)

    for half in range(2):
        yb_ref[half] = _bdot(obuf[half], wout_ref[...])

    @pl.when(t == SEQ_TILES - 1)
    def _():
        ss_ref[...] = s_scr[...]


def _gdn_scan(qa, gb, z, tm, aq, yb_sample, ong, wout):
    last = HALF_TILES - 1
    halves = lambda a: a.reshape((2, a.shape[0] // 2) + a.shape[1:])
    both = lambda rows, w: pl.BlockSpec((2, rows, w), lambda i: (0, jnp.minimum(i, last), 0))
    ztile = lambda off: pl.BlockSpec((SEQ_TILE, D_DELTA_V),
                                     lambda i: (jnp.minimum(i, last) + off, 0))
    yb, ss = pl.pallas_call(
        _gdn_scan_kernel,
        grid=(HALF_TILES + 1,),
        in_specs=[both(SEQ_TILE, CONV_DIM), both(SEQ_TILE, 128), ztile(0), ztile(HALF_TILES),
                  both(M_ROWS_TILE, 128), both(SLABS_TILE * CHUNK, 128),
                  pl.BlockSpec((DEC_BATCH, D_MODEL), lambda i: (0, 0)),
                  pl.BlockSpec((1, 128), lambda i: (0, 0)),
                  pl.BlockSpec((D_DELTA_V, D_MODEL), lambda i: (0, 0))],
        out_specs=[pl.BlockSpec((2, SEQ_TILE, D_MODEL), lambda i: (0, i, 0)),
                   pl.BlockSpec((2, None, N_HEADS, HEAD_K, HEAD_V),
                                lambda i: (0, jnp.minimum(i // SEQ_TILES, HALF_SEQS - 1), 0, 0, 0))],
        out_shape=[jax.ShapeDtypeStruct((2, (HALF_TILES + 1) * TOK_TILE, D_MODEL), F32),
                   jax.ShapeDtypeStruct((2, HALF_SEQS, N_HEADS, HEAD_K, HEAD_V), F32)],
        scratch_shapes=[pltpu.VMEM((2, SEQ_TILE, D_DELTA_V), F32),
                        pltpu.VMEM((2, N_HEADS, HEAD_K, HEAD_V), F32)],
        compiler_params=pltpu.CompilerParams(
            dimension_semantics=("arbitrary",), vmem_limit_bytes=VMEM_BIG),
        name="gdn_scan",
    )(halves(qa), halves(gb), z, z, halves(tm), halves(aq), yb_sample, ong, wout)
    return yb, ss.reshape(BATCH, N_HEADS, HEAD_K, HEAD_V)


def _gdn_sample_kernel(qkv_ref, z_ref, ab_ref, st_ref, s_ref, cw_ref, alog_ref, dtb_ref,
                       ong_ref, wout_ref, yb_ref, nst_ref, ns_ref):
    x = qkv_ref[...]
    cw = cw_ref[...]
    nbuf = QK_CONV - 1
    acc = cw[nbuf:nbuf + 1, :] * x
    for j in range(nbuf):
        acc = acc + cw[j:j + 1, :] * st_ref[:, j * CONV_DIM:(j + 1) * CONV_DIM]
    nst_ref[:, 0:(nbuf - 1) * CONV_DIM] = st_ref[:, CONV_DIM:nbuf * CONV_DIM]
    nst_ref[:, (nbuf - 1) * CONV_DIM:nbuf * CONV_DIM] = x
    qs, ks, vs = _qkv_act(acc)
    ab = ab_ref[...]
    g = -jnp.exp(alog_ref[...]) * jax.nn.softplus(ab + dtb_ref[...])
    beta = jax.nn.sigmoid(ab)
    eg = jnp.exp(g)
    z = z_ref[...]
    outs = []
    for h in range(N_HEADS):
        bh = beta[:, N_HEADS + h:N_HEADS + h + 1]
        egh = eg[:, h:h + 1]
        kb = ks[h] * bh
        w = kb * egh
        qg = qs[h] * egh
        u = vs[h] * bh
        qk = jnp.sum(qs[h] * ks[h], axis=-1, keepdims=True)
        o_rows = []
        for s in range(DEC_BLK):
            st = s_ref[s, h]
            wq = _bdot(jnp.concatenate([w[s:s + 1], qg[s:s + 1]], axis=0), st)
            vnew = u[s:s + 1] - wq[0:1]
            o_rows.append(wq[1:2] + qk[s:s + 1] * vnew)
            ns_ref[s, h] = st * egh[s:s + 1] + lax.dot_general(
                ks[h][s:s + 1].astype(BF16), vnew.astype(BF16), (((0,), (0,)), ((), ())),
                preferred_element_type=F32)
        o = jnp.concatenate(o_rows, axis=0)
        outs.append(_gate_norm(o, ong_ref[...], z[:, h * HEAD_V:(h + 1) * HEAD_V]))
    yb_ref[...] = _bdot(jnp.concatenate(outs, axis=1), wout_ref[...])


def _gdn_sample(qkv, z, ab, st, s0, cw, alog, dtb, ong, wout):
    nbuf = QK_CONV - 1
    blk0 = N_PROMPT // DEC_BLK
    rows = lambda w: pl.BlockSpec((DEC_BLK, w), lambda i: (blk0 + i, 0))
    vec = pl.BlockSpec((1, 128), lambda i: (0, 0))
    sblk = pl.BlockSpec((DEC_BLK, N_HEADS, HEAD_K, HEAD_V), lambda i: (i, 0, 0, 0))
    return pl.pallas_call(
        _gdn_sample_kernel,
        grid=(DEC_BATCH // DEC_BLK,),
        in_specs=[rows(CONV_DIM), rows(D_DELTA_V), rows(128),
                  pl.BlockSpec((DEC_BLK, nbuf * CONV_DIM), lambda i: (i, 0)),
                  sblk,
                  pl.BlockSpec((QK_CONV, CONV_DIM), lambda i: (0, 0)),
                  vec, vec, vec,
                  pl.BlockSpec((D_DELTA_V, D_MODEL), lambda i: (0, 0))],
        out_specs=[pl.BlockSpec((DEC_BLK, D_MODEL), lambda i: (i, 0)),
                   pl.BlockSpec((DEC_BLK, nbuf * CONV_DIM), lambda i: (i, 0)),
                   sblk],
        out_shape=[jax.ShapeDtypeStruct((DEC_BATCH, D_MODEL), F32),
                   jax.ShapeDtypeStruct((DEC_BATCH, nbuf * CONV_DIM), F32),
                   jax.ShapeDtypeStruct((DEC_BATCH, N_HEADS, HEAD_K, HEAD_V), F32)],
        compiler_params=pltpu.CompilerParams(dimension_semantics=("arbitrary",)),
        name="gdn_sample",
    )(qkv, z, ab, st, s0, cw, alog, dtb, ong, wout)


def _router_kernel(x_ref, g_ref, r_ref, h_ref, info_ref, cnt_ref):
    h = _rms(x_ref[...], g_ref[...])
    h_ref[...] = h
    logits = _hdot(h, r_ref[...])
    lane = lax.broadcasted_iota(jnp.int32, logits.shape, 1)
    neg = jnp.float32(-jnp.inf)
    logits = jnp.where(lane < N_EXPERTS, logits, neg)
    m1 = jnp.max(logits, axis=-1, keepdims=True)
    i1 = jnp.min(jnp.where(logits == m1, lane, 128), axis=-1, keepdims=True)
    rest = jnp.where(lane == i1, neg, logits)
    m2 = jnp.max(rest, axis=-1, keepdims=True)
    i2 = jnp.min(jnp.where(rest == m2, lane, 128), axis=-1, keepdims=True)
    e = jnp.exp(m2 - m1)
    p1 = 1.0 / (1.0 + e)
    p2 = e / (1.0 + e)
    oh1 = jnp.where(lane == i1, 1.0, 0.0).astype(F32)
    oh2 = jnp.where(lane == i2, 1.0, 0.0).astype(F32)
    tr = lax.broadcasted_iota(jnp.int32, (TOK_TILE, TOK_TILE), 0)
    tc = lax.broadcasted_iota(jnp.int32, (TOK_TILE, TOK_TILE), 1)
    cum = _bdot(jnp.where(tr >= tc, 1.0, 0.0), oh1 + oh2)
    rank = cum - (oh1 + oh2)
    r1 = jnp.sum(oh1 * rank, axis=-1, keepdims=True)
    r2 = jnp.sum(oh2 * rank, axis=-1, keepdims=True)
    info = jnp.where(lane == 0, i1.astype(F32), 0.0)
    info = jnp.where(lane == 1, i2.astype(F32), info)
    info = jnp.where(lane == 2, p1, info)
    info = jnp.where(lane == 3, p2, info)
    info = jnp.where(lane == 4, r1, info)
    info = jnp.where(lane == 5, r2, info)
    info_ref[...] = info
    cnt_ref[...] = jnp.broadcast_to(cum[TOK_TILE - 1:TOK_TILE, :], (8, 128))


def _router(x, g, r_pad):
    row = lambda w: pl.BlockSpec((TOK_TILE, w), lambda i: (i, 0))
    return pl.pallas_call(
        _router_kernel,
        grid=(N_TOK // TOK_TILE,),
        in_specs=[row(D_MODEL), pl.BlockSpec((1, D_MODEL), lambda i: (0, 0)),
                  pl.BlockSpec((D_MODEL, 128), lambda i: (0, 0))],
        out_specs=[row(D_MODEL), row(128), pl.BlockSpec((None, 8, 128), lambda i: (i, 0, 0))],
        out_shape=[jax.ShapeDtypeStruct((N_TOK, D_MODEL), F32),
                   jax.ShapeDtypeStruct((N_TOK, 128), F32),
                   jax.ShapeDtypeStruct((N_TOK // TOK_TILE, 8, 128), F32)],
        compiler_params=pltpu.CompilerParams(dimension_semantics=("arbitrary",)),
        name="moe_router",
    )(x, g, r_pad)


def _tile_rows_wait(src_ref, dst_ref, sem):
    pltpu.make_async_copy(src_ref.at[pl.ds(0, TOK_TILE), :], dst_ref, sem).wait()


def _dispatch_kernel(dest_ref, xs_in_ref, h_ref, xs_ref, sem):
    del xs_in_ref
    i = pl.program_id(0)
    base = i * (2 * TOK_TILE)
    tok0 = i * TOK_TILE

    def start(r, c):
        for k in range(2):
            pltpu.make_async_copy(
                h_ref.at[pl.ds(tok0 + r, 1), :],
                xs_ref.at[pl.ds(dest_ref[base + 2 * r + k], 1), :], sem).start()
        return c

    lax.fori_loop(0, TOK_TILE, start, 0, unroll=8)
    for _ in range(2):
        _tile_rows_wait(h_ref, xs_ref.at[pl.ds(0, TOK_TILE), :], sem)


def _dispatch(dest, xs0, h):
    return pl.pallas_call(
        _dispatch_kernel,
        grid_spec=pltpu.PrefetchScalarGridSpec(
            num_scalar_prefetch=1,
            grid=(N_TOK // TOK_TILE,),
            in_specs=[pl.BlockSpec(memory_space=pl.ANY), pl.BlockSpec(memory_space=pl.ANY)],
            out_specs=pl.BlockSpec(memory_space=pl.ANY),
            scratch_shapes=[pltpu.SemaphoreType.DMA(())]),
        out_shape=jax.ShapeDtypeStruct((MOE_ROWS, D_MODEL), F32),
        input_output_aliases={1: 0},
        compiler_params=pltpu.CompilerParams(dimension_semantics=("arbitrary",)),
        name="moe_dispatch",
    )(dest, xs0, h)


def _experts_kernel(te_ref, tv_ref, x_ref, wg_ref, wu_ref, wd_ref, o_ref):
    i = pl.program_id(0)

    @pl.when(tv_ref[i] > 0)
    def _():
        o_ref[...] = _swiglu_rows(x_ref[...].astype(BF16), wg_ref, wu_ref, wd_ref)

    @pl.when(tv_ref[i] == 0)
    def _():
        o_ref[...] = jnp.zeros((MOE_TILE, D_MODEL), F32)


def _experts(tile_expert, tile_valid, xs, wg, wu, wd):
    row = pl.BlockSpec((MOE_TILE, D_MODEL), lambda i, te, tv: (i, 0))
    return pl.pallas_call(
        _experts_kernel,
        grid_spec=pltpu.PrefetchScalarGridSpec(
            num_scalar_prefetch=2,
            grid=(MOE_ROWS // MOE_TILE,),
            in_specs=[row,
                      pl.BlockSpec((None, D_MODEL, D_FF), lambda i, te, tv: (te[i], 0, 0)),
                      pl.BlockSpec((None, D_MODEL, D_FF), lambda i, te, tv: (te[i], 0, 0)),
                      pl.BlockSpec((None, D_FF, D_MODEL), lambda i, te, tv: (te[i], 0, 0))],
            out_specs=row),
        out_shape=jax.ShapeDtypeStruct((MOE_ROWS, D_MODEL), F32),
        compiler_params=pltpu.CompilerParams(
            dimension_semantics=("arbitrary",), vmem_limit_bytes=VMEM_BIG),
        name="moe_experts",
    )(tile_expert, tile_valid, xs, wg, wu, wd)


def _combine_kernel(dest_ref, x_ref, info_ref, g_ref, ys_ref, yp_ref, ysm_ref, ybuf, sem):
    i = pl.program_id(0)
    n = pl.num_programs(0)

    def start_tile(tile):
        slot = tile % 2
        base = tile * (2 * TOK_TILE)

        def start(r, c):
            for k in range(2):
                pltpu.make_async_copy(
                    ys_ref.at[pl.ds(dest_ref[base + 2 * r + k], 1), :],
                    ybuf.at[slot, k, pl.ds(r, 1), :], sem.at[slot]).start()
            return c

        lax.fori_loop(0, TOK_TILE, start, 0, unroll=8)

    @pl.when(i == 0)
    def _():
        start_tile(i)

    @pl.when(i + 1 < n)
    def _():
        start_tile(i + 1)

    slot = i % 2
    for k in range(2):
        _tile_rows_wait(ys_ref, ybuf.at[slot, k], sem.at[slot])
    info = info_ref[...]
    x = x_ref[...] + info[:, 2:3] * ybuf[slot, 0] + info[:, 3:4] * ybuf[slot, 1]
    y = _rms(x, g_ref[...])

    @pl.when(i < PROMPT_TILES)
    def _():
        yp_ref[...] = y

    @pl.when(i == PROMPT_TILES)
    def _():
        ysm_ref[...] = y[0:DEC_BATCH, :]


def _combine(dest, x, info, g, ys):
    last = PROMPT_TILES - 1
    return pl.pallas_call(
        _combine_kernel,
        grid_spec=pltpu.PrefetchScalarGridSpec(
            num_scalar_prefetch=1,
            grid=(N_TOK // TOK_TILE,),
            in_specs=[pl.BlockSpec((TOK_TILE, D_MODEL), lambda i, d: (i, 0)),
                      pl.BlockSpec((TOK_TILE, 128), lambda i, d: (i, 0)),
                      pl.BlockSpec((1, D_MODEL), lambda i, d: (0, 0)),
                      pl.BlockSpec(memory_space=pl.ANY)],
            out_specs=[pl.BlockSpec((TOK_TILE, D_MODEL), lambda i, d: (jnp.minimum(i, last), 0)),
                       pl.BlockSpec((DEC_BATCH, D_MODEL), lambda i, d: (0, 0))],
            scratch_shapes=[pltpu.VMEM((2, 2, TOK_TILE, D_MODEL), F32),
                            pltpu.SemaphoreType.DMA((2,))]),
        out_shape=[jax.ShapeDtypeStruct((N_PROMPT, D_MODEL), F32),
                   jax.ShapeDtypeStruct((DEC_BATCH, D_MODEL), F32)],
        compiler_params=pltpu.CompilerParams(dimension_semantics=("arbitrary",)),
        name="moe_combine",
    )(dest, x, info, g, ys)


def _moe_layer(x, g2, router_w, wg, wu, wd, final_g):
    r_pad = jnp.zeros((D_MODEL, 128), F32).at[:, :N_EXPERTS].set(router_w)
    h, info, cnt = _router(x, g2, r_pad)
    n_tiles = N_TOK // TOK_TILE
    counts = cnt[:, 0, :N_EXPERTS].astype(jnp.int32)
    total = jnp.sum(counts, axis=0)
    padded = ((total + MOE_TILE - 1) // MOE_TILE) * MOE_TILE
    ends = jnp.cumsum(padded)
    starts = ends - padded
    base = starts[None, :] + jnp.cumsum(counts, axis=0) - counts
    choice = info[:, 0:2].astype(jnp.int32).reshape(n_tiles, TOK_TILE, 2)
    rank = info[:, 4:6].astype(jnp.int32).reshape(n_tiles, TOK_TILE, 2)
    onehot = choice[..., None] == jnp.arange(N_EXPERTS, dtype=jnp.int32)
    dest = (jnp.sum(jnp.where(onehot, base[:, None, None, :], 0), axis=-1) + rank).reshape(-1)
    tile_start = jnp.arange(MOE_ROWS // MOE_TILE, dtype=jnp.int32) * MOE_TILE
    tile_expert = jnp.minimum(
        jnp.sum((tile_start[:, None] >= ends[None, :]).astype(jnp.int32), axis=1), N_EXPERTS - 1)
    tile_valid = (tile_start < ends[-1]).astype(jnp.int32)
    last_used = jnp.max(jnp.where(tile_valid > 0, tile_expert, 0))
    tile_expert = jnp.where(tile_valid > 0, tile_expert, last_used).astype(jnp.int32)

    xs = _dispatch(dest, jnp.zeros((MOE_ROWS, D_MODEL), F32), h)
    ys = _experts(tile_expert, tile_valid, xs, wg, wu, wd)
    return _combine(dest, x, info, final_g, ys)


def _rearranged_w_in(w_in):
    off_qkv = 2 * D_CONV
    off_z = off_qkv + CONV_DIM
    off_a = off_z + D_DELTA_V
    off_ga = off_a + 2 * N_HEADS
    off_gb = off_ga + D_MODEL
    return jnp.concatenate(
        [w_in[:, :off_a], w_in[:, off_ga:off_gb], w_in[:, off_gb:], w_in[:, off_a:off_ga],
         jnp.zeros((D_MODEL, 128 - 2 * N_HEADS), w_in.dtype)], axis=1).astype(BF16)


def _lane_vec(v, offset=0):
    return jnp.zeros((1, 128), F32).at[0, offset:offset + v.shape[0]].set(v)


def kernel(x_prompt, x_sample, state_delta, state_qkv_conv, state_dwconv, norm1_g, w_in, qkv_conv_w, a_log, dt_bias, o_norm_g, w_delta_out, dw_w, dw_b, ln_g, ln_b, w_conv_out, w_o, norm2_g, ffn_w_gate, ffn_w_up, ffn_w_down, router_w, exp_w_gate, exp_w_up, exp_w_down, final_norm_g):
    x = (x_prompt.reshape(N_PROMPT, D_MODEL), x_sample.reshape(DEC_BATCH, D_MODEL))
    s_p, q_p, d_p, s_s, q_s, d_s = [], [], [], [], [], []
    for l in range(DEPTH):
        glu, qkv, z, ga, gb, ab = _norm_proj(x, norm1_g[l][None, :], _rearranged_w_in(w_in[l]))
        dwb, lng, lnb = dw_b[l][None, :], ln_g[l][None, :], ln_b[l][None, :]
        wco = w_conv_out[l].astype(BF16)
        ya_s, dst_s = _conv_sample(glu, state_dwconv[l].reshape(DEC_BATCH, -1), dw_w[l], dwb, lng, lnb, wco)
        ya, dst_p = _conv_prompt(glu, ya_s, dw_w[l], dwb, lng, lnb, wco)
        alog, dtb, ong = _lane_vec(a_log[l]), _lane_vec(dt_bias[l]), o_norm_g[l][None, :]
        wdo = w_delta_out[l].astype(BF16)
        yb_s, qst_s, sst_s = _gdn_sample(qkv, z, ab, state_qkv_conv[l].reshape(DEC_BATCH, -1),
                                         state_delta[l], qkv_conv_w[l], alog, dtb, ong, wdo)
        qa, gbeta, m, aq, qst_p = _gdn_pre(qkv, ab, qkv_conv_w[l], alog, dtb)
        yb, sst_p = _gdn_scan(qa, gbeta, z, _gdn_solve(m), aq, yb_s, ong, wdo)
        x = _merge(x, ya, yb, ga, gb, w_o[l].astype(BF16))
        i = l // 2
        if l % 2 == 0:
            x = _ffn(x, norm2_g[l][None, :], ffn_w_gate[i].astype(BF16), ffn_w_up[i].astype(BF16),
                     ffn_w_down[i].astype(BF16))
        else:
            y_prompt, y_sample = _moe_layer(
                x, norm2_g[l][None, :], router_w[i], exp_w_gate[i].astype(BF16),
                exp_w_up[i].astype(BF16), exp_w_down[i].astype(BF16), final_norm_g[None, :])
        s_p.append(sst_p)
        q_p.append(qst_p)
        d_p.append(dst_p)
        s_s.append(sst_s)
        q_s.append(qst_s.reshape(DEC_BATCH, QK_CONV - 1, CONV_DIM))
        d_s.append(dst_s.reshape(DEC_BATCH, DW_WIDTH - 1, D_CONV))
    return (y_prompt.reshape(BATCH, SEQ, D_MODEL), y_sample.reshape(DEC_BATCH, 1, D_MODEL), jnp.stack(s_p), jnp.stack(q_p), jnp.stack(d_p),
            jnp.stack(s_s), jnp.stack(q_s), jnp.stack(d_s))
```

```python
import functools

import jax
import jax.numpy as jnp
from jax import lax
from jax.experimental import pallas as pl
from jax.experimental.pallas import tpu as pltpu

F32 = jnp.float32
BF16 = jnp.bfloat16
HIGHEST = lax.Precision.HIGHEST

D_MODEL = 1024
BATCH = 8
SEQ = 2048
DEPTH = 2
DEC_BATCH = 128
D_CONV = 512
DW_WIDTH = 31
N_HEADS = 4
HEAD_K = 128
HEAD_V = 128
QK_CONV = 4
D_DELTA_K = N_HEADS * HEAD_K
D_DELTA_V = N_HEADS * HEAD_V
CONV_DIM = 2 * D_DELTA_K + D_DELTA_V
D_FF = 2816
N_EXPERTS = 8
EPS = 1e-6

N_PROMPT = BATCH * SEQ
TOK_TILE = 512
SEQ_TILE = TOK_TILE
N_TOK = N_PROMPT + TOK_TILE
SEQ_TILES = SEQ // SEQ_TILE
PROMPT_TILES = N_PROMPT // TOK_TILE
HALF_TILES = PROMPT_TILES // 2
HALF_SEQS = BATCH // 2
CHUNK = 64
M_PITCH = 68
SLABS_TILE = 2 * (SEQ_TILE // CHUNK)
M_ROWS_TILE = SLABS_TILE * M_PITCH
SOLVE_SLABS = 128
GC_LANE = 8
ROW_BLK = 64
FF_HALF = D_FF // 2
MOE_TILE = 256
MOE_ROWS = ((2 * N_TOK + N_EXPERTS * (MOE_TILE - 1)) // MOE_TILE + 1) * MOE_TILE
DEC_BLK = 8

C_GLU, C_QKV, C_Z, C_GA, C_GB, C_AB = 0, 1024, 2560, 3072, 4096, 5120
N_PROJ = 5248

VMEM_BIG = 56 * 1024 * 1024


def _bdot(a, b):
    return jnp.dot(a.astype(BF16), b.astype(BF16), preferred_element_type=F32)


def _hdot(a, b):
    return jnp.dot(a, b, preferred_element_type=F32, precision=HIGHEST)


def _rms(x, g):
    return x * lax.rsqrt(jnp.mean(x * x, axis=-1, keepdims=True) + EPS) * g


def _silu(x):
    return x * jax.nn.sigmoid(x)


def _resident(shape):
    return pl.BlockSpec(shape, lambda *_: (0,) * len(shape), pipeline_mode=pl.Buffered(1))


def _stream_x_specs(x):
    if isinstance(x, tuple):
        last = PROMPT_TILES - 1
        return ([pl.BlockSpec((TOK_TILE, D_MODEL), lambda i: (jnp.minimum(i, last), 0)),
                 pl.BlockSpec((DEC_BATCH, D_MODEL), lambda i: (0, 0))], list(x))
    return [pl.BlockSpec((TOK_TILE, D_MODEL), lambda i: (i, 0))], [x]


def _stream_x_tile(x_refs, xbuf):
    if len(x_refs) == 1:
        return x_refs[0][...]
    xp_ref, xs_ref = x_refs
    i = pl.program_id(0)

    @pl.when(i < PROMPT_TILES)
    def _():
        xbuf[...] = xp_ref[...]

    @pl.when(i == PROMPT_TILES)
    def _():
        xbuf[0:DEC_BATCH, :] = xs_ref[...]
        xbuf[DEC_BATCH:, :] = jnp.zeros((TOK_TILE - DEC_BATCH, D_MODEL), F32)

    return xbuf[...]


def _norm_proj_kernel(n_x, *refs):
    x_refs = refs[:n_x]
    g_ref, w_ref, glu_ref, qkv_ref, z_ref, ga_ref, gb_ref, ab_ref, xbuf = refs[n_x:]
    h = _rms(_stream_x_tile(x_refs, xbuf), g_ref[...]).astype(BF16)

    def proj(lo, hi):
        return jnp.dot(h, w_ref[:, lo:hi], preferred_element_type=F32)

    glu_ref[...] = proj(C_GLU, C_QKV)
    qkv_ref[...] = proj(C_QKV, C_Z)
    z_ref[...] = proj(C_Z, C_GA)
    ga_ref[...] = proj(C_GA, C_GB)
    gb_ref[...] = proj(C_GB, C_AB)
    ab_ref[...] = proj(C_AB, N_PROJ)


def _norm_proj(x, g, w_all):
    row = lambda w: pl.BlockSpec((TOK_TILE, w), lambda i: (i, 0))
    widths = (1024, CONV_DIM, D_DELTA_V, D_MODEL, D_MODEL, 128)
    x_specs, x_ops = _stream_x_specs(x)
    return pl.pallas_call(
        functools.partial(_norm_proj_kernel, len(x_ops)),
        grid=(N_TOK // TOK_TILE,),
        in_specs=x_specs + [pl.BlockSpec((1, D_MODEL), lambda i: (0, 0)),
                            _resident((D_MODEL, N_PROJ))],
        out_specs=[row(w) for w in widths],
        out_shape=[jax.ShapeDtypeStruct((N_TOK, w), F32) for w in widths],
        scratch_shapes=[pltpu.VMEM((TOK_TILE, D_MODEL), F32)],
        compiler_params=pltpu.CompilerParams(
            dimension_semantics=("arbitrary",), vmem_limit_bytes=VMEM_BIG),
        name="norm_proj",
    )(*x_ops, g, w_all)


def _merge_kernel(n_x, *refs):
    x_refs = refs[:n_x]
    ya_ref, yb_ref, ga_ref, gb_ref, wo_ref, o_ref, xbuf = refs[n_x:]
    m = jax.nn.sigmoid(ga_ref[...]) * ya_ref[...] + jax.nn.sigmoid(gb_ref[...]) * yb_ref[...]
    o_ref[...] = _stream_x_tile(x_refs, xbuf) + _bdot(m, wo_ref[...])


def _merge(x, ya, yb, ga, gb, wo):
    row = pl.BlockSpec((TOK_TILE, D_MODEL), lambda i: (i, 0))
    x_specs, x_ops = _stream_x_specs(x)
    yb_spec = pl.BlockSpec((None, TOK_TILE, D_MODEL),
                           lambda i: (i // HALF_TILES - i // (2 * HALF_TILES),
                                      i - HALF_TILES * (i // HALF_TILES - i // (2 * HALF_TILES)), 0))
    return pl.pallas_call(
        functools.partial(_merge_kernel, len(x_ops)),
        grid=(N_TOK // TOK_TILE,),
        in_specs=x_specs + [row, yb_spec, row, row,
                            pl.BlockSpec((D_MODEL, D_MODEL), lambda i: (0, 0))],
        out_specs=row,
        out_shape=jax.ShapeDtypeStruct((N_TOK, D_MODEL), F32),
        scratch_shapes=[pltpu.VMEM((TOK_TILE, D_MODEL), F32)],
        compiler_params=pltpu.CompilerParams(dimension_semantics=("arbitrary",)),
        name="merge",
    )(*x_ops, ya, yb, ga, gb, wo)


def _swiglu_rows(h, wg_ref, wu_ref, wd_ref):
    acc = None
    for lo in (0, FF_HALF):
        a = jnp.dot(h, wg_ref[:, lo:lo + FF_HALF], preferred_element_type=F32)
        b = jnp.dot(h, wu_ref[:, lo:lo + FF_HALF], preferred_element_type=F32)
        part = jnp.dot((_silu(a) * b).astype(BF16), wd_ref[lo:lo + FF_HALF, :],
                       preferred_element_type=F32)
        acc = part if acc is None else acc + part
    return acc


def _ffn_kernel(x_ref, g_ref, wg_ref, wu_ref, wd_ref, o_ref):
    x = x_ref[...]
    h = _rms(x, g_ref[...]).astype(BF16)
    o_ref[...] = x + _swiglu_rows(h, wg_ref, wu_ref, wd_ref)


def _ffn(x, g, wg, wu, wd):
    row = pl.BlockSpec((TOK_TILE, D_MODEL), lambda i: (i, 0))
    return pl.pallas_call(
        _ffn_kernel,
        grid=(N_TOK // TOK_TILE,),
        in_specs=[row, pl.BlockSpec((1, D_MODEL), lambda i: (0, 0)),
                  _resident((D_MODEL, D_FF)), _resident((D_MODEL, D_FF)),
                  _resident((D_FF, D_MODEL))],
        out_specs=row,
        out_shape=jax.ShapeDtypeStruct((N_TOK, D_MODEL), F32),
        compiler_params=pltpu.CompilerParams(
            dimension_semantics=("arbitrary",), vmem_limit_bytes=VMEM_BIG),
        name="ffn_dense",
    )(x, g, wg, wu, wd)


def _conv_tail(c, dwb, lng, lnb):
    c = c + dwb
    mu = jnp.mean(c, axis=-1, keepdims=True)
    var = jnp.mean(jnp.square(c - mu), axis=-1, keepdims=True)
    c = (c - mu) * lax.rsqrt(var + EPS) * lng + lnb
    return _silu(c)


def _sample_rows_tile(ys_ref, y_ref):
    y_ref[0:DEC_BATCH, :] = ys_ref[...]
    y_ref[DEC_BATCH:, :] = jnp.zeros((TOK_TILE - DEC_BATCH, D_MODEL), F32)


def _conv_prompt_kernel(glu_ref, ys_ref, dww_ref, dwb_ref, lng_ref, lnb_ref, wout_ref,
                        ya_ref, st_ref, ubuf, ushift, cbuf):
    i = pl.program_id(0)
    pl.when(i < PROMPT_TILES)(functools.partial(
        _conv_prompt_tile, i % SEQ_TILES, glu_ref, dww_ref, dwb_ref, lng_ref, lnb_ref, wout_ref,
        ya_ref, st_ref, ubuf, ushift, cbuf))
    pl.when(i == PROMPT_TILES)(functools.partial(_sample_rows_tile, ys_ref, ya_ref))


def _conv_prompt_tile(t, glu_ref, dww_ref, dwb_ref, lng_ref, lnb_ref, wout_ref,
                      ya_ref, st_ref, ubuf, ushift, cbuf):
    halo = 32

    @pl.when(t == 0)
    def _():
        ubuf[0:halo, :] = jnp.zeros((halo, D_CONV), F32)

    @pl.when(t > 0)
    def _():
        ubuf[0:halo, :] = ubuf[SEQ_TILE:SEQ_TILE + halo, :]

    glu = glu_ref[...]
    ubuf[halo:halo + SEQ_TILE, :] = glu[:, :D_CONV] * jax.nn.sigmoid(glu[:, D_CONV:])
    for r in range(1, 8):
        ushift[r - 1, 0:SEQ_TILE + halo - 8, :] = ubuf[r:r + SEQ_TILE + halo - 8, :]
    w = dww_ref[...]
    first = halo - (DW_WIDTH - 1)
    for rb in range(SEQ_TILE // ROW_BLK):
        r0 = rb * ROW_BLK
        acc = jnp.zeros((ROW_BLK, D_CONV), F32)
        for j in range(DW_WIDTH):
            a, r = divmod(first + j, 8)
            if r == 0:
                win = ubuf[r0 + 8 * a:r0 + 8 * a + ROW_BLK, :]
            else:
                win = ushift[r - 1, r0 + 8 * a:r0 + 8 * a + ROW_BLK, :]
            acc = acc + w[j:j + 1, :] * win
        c = _conv_tail(acc, dwb_ref[...], lng_ref[...], lnb_ref[...])
        cbuf[r0:r0 + ROW_BLK, :] = c.astype(BF16)
    ya_ref[...] = jnp.dot(cbuf[...], wout_ref[...], preferred_element_type=F32)

    @pl.when(t == SEQ_TILES - 1)
    def _():
        st_ref[...] = ubuf[halo + SEQ_TILE - (DW_WIDTH - 1):halo + SEQ_TILE, :]


def _seq_of_tile(i):
    return jnp.minimum(i // SEQ_TILES, BATCH - 1)


def _conv_prompt(glu, ya_sample, dww, dwb, lng, lnb, wout):
    vec = pl.BlockSpec((1, D_CONV), lambda i: (0, 0))
    return pl.pallas_call(
        _conv_prompt_kernel,
        grid=(PROMPT_TILES + 1,),
        in_specs=[pl.BlockSpec((SEQ_TILE, 2 * D_CONV), lambda i: (i, 0)),
                  pl.BlockSpec((DEC_BATCH, D_MODEL), lambda i: (0, 0)),
                  pl.BlockSpec((DW_WIDTH, D_CONV), lambda i: (0, 0)),
                  vec, vec, vec,
                  pl.BlockSpec((D_CONV, D_MODEL), lambda i: (0, 0))],
        out_specs=[pl.BlockSpec((SEQ_TILE, D_MODEL), lambda i: (i, 0)),
                   pl.BlockSpec((None, DW_WIDTH - 1, D_CONV), lambda i: (_seq_of_tile(i), 0, 0))],
        out_shape=[jax.ShapeDtypeStruct((N_TOK, D_MODEL), F32),
                   jax.ShapeDtypeStruct((BATCH, DW_WIDTH - 1, D_CONV), F32)],
        scratch_shapes=[pltpu.VMEM((SEQ_TILE + 32, D_CONV), F32),
                        pltpu.VMEM((7, SEQ_TILE + 24, D_CONV), F32),
                        pltpu.VMEM((SEQ_TILE, D_CONV), BF16)],
        compiler_params=pltpu.CompilerParams(dimension_semantics=("arbitrary",)),
        name="conv_prompt",
    )(glu, ya_sample, dww, dwb, lng, lnb, wout)


def _conv_sample_kernel(glu_ref, st_ref, dww_ref, dwb_ref, lng_ref, lnb_ref, wout_ref,
                        ya_ref, nst_ref):
    glu = glu_ref[...]
    u = glu[:, :D_CONV] * jax.nn.sigmoid(glu[:, D_CONV:])
    w = dww_ref[...]
    nbuf = DW_WIDTH - 1
    acc = w[nbuf:nbuf + 1, :] * u
    for j in range(nbuf):
        acc = acc + w[j:j + 1, :] * st_ref[:, j * D_CONV:(j + 1) * D_CONV]
    c = _conv_tail(acc, dwb_ref[...], lng_ref[...], lnb_ref[...])
    ya_ref[...] = _bdot(c, wout_ref[...])
    nst_ref[:, 0:(nbuf - 1) * D_CONV] = st_ref[:, D_CONV:nbuf * D_CONV]
    nst_ref[:, (nbuf - 1) * D_CONV:nbuf * D_CONV] = u


def _conv_sample(glu, st, dww, dwb, lng, lnb, wout):
    nbuf = DW_WIDTH - 1
    blk = N_PROMPT // DEC_BATCH
    vec = pl.BlockSpec((1, D_CONV), lambda i: (0, 0))
    return pl.pallas_call(
        _conv_sample_kernel,
        grid=(1,),
        in_specs=[pl.BlockSpec((DEC_BATCH, 2 * D_CONV), lambda i: (blk, 0)),
                  pl.BlockSpec((DEC_BATCH, nbuf * D_CONV), lambda i: (0, 0)),
                  pl.BlockSpec((DW_WIDTH, D_CONV), lambda i: (0, 0)),
                  vec, vec, vec,
                  pl.BlockSpec((D_CONV, D_MODEL), lambda i: (0, 0))],
        out_specs=[pl.BlockSpec((DEC_BATCH, D_MODEL), lambda i: (0, 0)),
                   pl.BlockSpec((DEC_BATCH, nbuf * D_CONV), lambda i: (0, 0))],
        out_shape=[jax.ShapeDtypeStruct((DEC_BATCH, D_MODEL), F32),
                   jax.ShapeDtypeStruct((DEC_BATCH, nbuf * D_CONV), F32)],
        compiler_params=pltpu.CompilerParams(dimension_semantics=("arbitrary",)),
        name="conv_sample",
    )(glu, st, dww, dwb, lng, lnb, wout)


def _qkv_act(c):
    c = _silu(c)
    qs, ks, vs = [], [], []
    for h in range(N_HEADS):
        q = c[:, h * HEAD_K:(h + 1) * HEAD_K]
        k = c[:, D_DELTA_K + h * HEAD_K:D_DELTA_K + (h + 1) * HEAD_K]
        qs.append(q * lax.rsqrt(jnp.sum(q * q, axis=-1, keepdims=True) + EPS) * (HEAD_K ** -0.5))
        ks.append(k * lax.rsqrt(jnp.sum(k * k, axis=-1, keepdims=True) + EPS))
        vs.append(c[:, 2 * D_DELTA_K + h * HEAD_V:2 * D_DELTA_K + (h + 1) * HEAD_V])
    return qs, ks, vs


def _gate_norm(o, ong, z):
    o = o * lax.rsqrt(jnp.mean(o * o, axis=-1, keepdims=True) + EPS) * ong
    return o * _silu(z)


def _gdn_pre_kernel(qkv_ref, ab_ref, cw_ref, alog_ref, dtb_ref,
                    qa_ref, gb_ref, m_ref, aq_ref, sq_ref, xbuf):
    t = pl.program_id(0) % SEQ_TILES
    halo = 8

    @pl.when(t == 0)
    def _():
        xbuf[0:halo, :] = jnp.zeros((halo, CONV_DIM), F32)

    @pl.when(t > 0)
    def _():
        xbuf[0:halo, :] = xbuf[SEQ_TILE:SEQ_TILE + halo, :]

    xbuf[halo:halo + SEQ_TILE, :] = qkv_ref[...]

    @pl.when(t == SEQ_TILES - 1)
    def _():
        sq_ref[...] = xbuf[halo + SEQ_TILE - (QK_CONV - 1):halo + SEQ_TILE, :]

    ab = ab_ref[...]
    lane = lax.broadcasted_iota(jnp.int32, (SEQ_TILE, 128), 1)
    gb = jnp.where(lane < N_HEADS,
                   -jnp.exp(alog_ref[...]) * jax.nn.softplus(ab + dtb_ref[...]),
                   jax.nn.sigmoid(ab))
    cw = cw_ref[...]
    first = halo - (QK_CONV - 1)
    ri = lax.broadcasted_iota(jnp.int32, (CHUNK, 128), 0)
    li = lax.broadcasted_iota(jnp.int32, (CHUNK, 128), 1)
    left = li < CHUNK
    jj = jnp.where(left, li, li - CHUNK)
    strict = ri > jj
    tril = ri >= jj
    r64 = lax.broadcasted_iota(jnp.int32, (CHUNK, CHUNK), 0)
    c64 = lax.broadcasted_iota(jnp.int32, (CHUNK, CHUNK), 1)
    csum = jnp.where(r64 >= c64, 1.0, 0.0).astype(F32)
    for c in range(SEQ_TILE // CHUNK):
        r0 = c * CHUNK
        acc = cw[0:1, :] * xbuf[r0 + first:r0 + first + CHUNK, :]
        for j in range(1, QK_CONV):
            acc = acc + cw[j:j + 1, :] * xbuf[r0 + first + j:r0 + first + j + CHUNK, :]
        qs, ks, vs = _qkv_act(acc)
        for h in range(N_HEADS):
            qa_ref[r0:r0 + CHUNK, h * HEAD_K:(h + 1) * HEAD_K] = qs[h]
            qa_ref[r0:r0 + CHUNK, D_DELTA_K + h * HEAD_K:D_DELTA_K + (h + 1) * HEAD_K] = ks[h]
            qa_ref[r0:r0 + CHUNK, 2 * D_DELTA_K + h * HEAD_V:2 * D_DELTA_K + (h + 1) * HEAD_V] = vs[h]
        gbc = gb[r0:r0 + CHUNK, :]
        gc = _hdot(csum, gbc)
        gb_ref[r0:r0 + CHUNK, :] = jnp.where(li < GC_LANE, gbc, pltpu.roll(gc, GC_LANE, axis=1))
        gct = gc.T
        for hp in range(N_HEADS // 2):
            h0, h1 = 2 * hp, 2 * hp + 1
            lhs = jnp.concatenate(
                [ks[h0] * gbc[:, N_HEADS + h0:N_HEADS + h0 + 1],
                 ks[h1] * gbc[:, N_HEADS + h1:N_HEADS + h1 + 1], qs[h0], qs[h1]], axis=0)
            rhs = jnp.concatenate([ks[h0], ks[h1]], axis=0)
            prod = lax.dot_general(lhs.astype(BF16), rhs.astype(BF16), (((1,), (1,)), ((), ())),
                                   preferred_element_type=F32)
            kk = jnp.where(left, prod[0:CHUNK], prod[CHUNK:2 * CHUNK])
            qk = jnp.where(left, prod[2 * CHUNK:3 * CHUNK], prod[3 * CHUNK:4 * CHUNK])
            gcol = jnp.where(left, gc[:, h0:h0 + 1], gc[:, h1:h1 + 1])
            grow = jnp.concatenate([gct[h0:h0 + 1, :], gct[h1:h1 + 1, :]], axis=1)
            diff = gcol - grow
            slab = 2 * c + hp
            m_ref[slab * M_PITCH:slab * M_PITCH + CHUNK, :] = jnp.where(
                strict, kk * jnp.exp(jnp.where(strict, diff, 0.0)), 0.0)
            m_ref[slab * M_PITCH + CHUNK:(slab + 1) * M_PITCH, :] = jnp.zeros(
                (M_PITCH - CHUNK, 128), F32)
            aq_ref[slab * CHUNK:(slab + 1) * CHUNK, :] = jnp.where(
                tril, qk * jnp.exp(jnp.where(tril, diff, 0.0)), 0.0)


def _gdn_pre(qkv, ab, cw, alog, dtb):
    tile = lambda w: pl.BlockSpec((SEQ_TILE, w), lambda i: (i, 0))
    vec = pl.BlockSpec((1, 128), lambda i: (0, 0))
    return pl.pallas_call(
        _gdn_pre_kernel,
        grid=(PROMPT_TILES,),
        in_specs=[tile(CONV_DIM), tile(128),
                  pl.BlockSpec((QK_CONV, CONV_DIM), lambda i: (0, 0)), vec, vec],
        out_specs=[tile(CONV_DIM), tile(128),
                   pl.BlockSpec((M_ROWS_TILE, 128), lambda i: (i, 0)),
                   pl.BlockSpec((SLABS_TILE * CHUNK, 128), lambda i: (i, 0)),
                   pl.BlockSpec((None, QK_CONV - 1, CONV_DIM), lambda i: (i // SEQ_TILES, 0, 0))],
        out_shape=[jax.ShapeDtypeStruct((N_PROMPT, CONV_DIM), F32),
                   jax.ShapeDtypeStruct((N_PROMPT, 128), F32),
                   jax.ShapeDtypeStruct((PROMPT_TILES * M_ROWS_TILE, 128), F32),
                   jax.ShapeDtypeStruct((PROMPT_TILES * SLABS_TILE * CHUNK, 128), F32),
                   jax.ShapeDtypeStruct((BATCH, QK_CONV - 1, CONV_DIM), F32)],
        scratch_shapes=[pltpu.VMEM((SEQ_TILE + 8, CONV_DIM), F32)],
        compiler_params=pltpu.CompilerParams(dimension_semantics=("arbitrary",)),
        name="gdn_pre",
    )(qkv, ab, cw, alog, dtb)


def _gdn_solve_kernel(m_ref, t_ref, mt, xt):
    def to_lanes(i, c):
        g = m_ref[pl.ds(i, SOLVE_SLABS, stride=M_PITCH), :]
        mt[pl.ds(pl.multiple_of(i * 128, 128), 128), :] = g.T
        return c

    lax.fori_loop(0, CHUNK, to_lanes, 0, unroll=4)

    row = lax.broadcasted_iota(jnp.int32, (CHUNK, SOLVE_SLABS), 0)

    def solve_row(i, c):
        base = pl.multiple_of(i * 128, 128)
        unit = jnp.where(row == i, 1.0, 0.0).astype(F32)

        def step(j, acc):
            a0, a1 = acc
            xb = pl.multiple_of(j * 128, 128)
            m0 = mt[pl.ds(base + j, 1), :]
            m1 = mt[pl.ds(base + CHUNK + j, 1), :]
            a0 = a0 - m0 * xt[pl.ds(xb, CHUNK), :]
            a1 = a1 - m1 * xt[pl.ds(xb + CHUNK, CHUNK), :]
            return a0, a1

        a0, a1 = lax.fori_loop(0, i, step, (unit, unit))
        xt[pl.ds(base, CHUNK), :] = a0
        xt[pl.ds(base + CHUNK, CHUNK), :] = a1
        return c

    lax.fori_loop(0, CHUNK, solve_row, 0)

    def from_lanes(i, c):
        x = xt[pl.ds(pl.multiple_of(i * 128, 128), 128), :]
        t_ref[pl.ds(i, SOLVE_SLABS, stride=M_PITCH), :] = x.T
        return c

    lax.fori_loop(0, CHUNK, from_lanes, 0, unroll=4)
    for k in range(CHUNK, M_PITCH):
        t_ref[pl.ds(k, SOLVE_SLABS, stride=M_PITCH), :] = jnp.zeros((SOLVE_SLABS, 128), F32)


def _gdn_solve(m):
    rows = SOLVE_SLABS * M_PITCH
    return pl.pallas_call(
        _gdn_solve_kernel,
        grid=(PROMPT_TILES * SLABS_TILE // SOLVE_SLABS,),
        in_specs=[pl.BlockSpec((rows, 128), lambda i: (i, 0))],
        out_specs=pl.BlockSpec((rows, 128), lambda i: (i, 0)),
        out_shape=jax.ShapeDtypeStruct(m.shape, F32),
        scratch_shapes=[pltpu.VMEM((CHUNK * 128, SOLVE_SLABS), F32),
                        pltpu.VMEM((CHUNK * 128, SOLVE_SLABS), F32)],
        compiler_params=pltpu.CompilerParams(dimension_semantics=("arbitrary",)),
        name="gdn_solve",
    )(m)


def _gdn_scan_kernel(qa_ref, gb_ref, z0_ref, z1_ref, tm_ref, aq_ref, ys_ref, ong_ref, wout_ref,
                     yb_ref, ss_ref, obuf, s_scr):
    i = pl.program_id(0)
    pl.when(i < HALF_TILES)(functools.partial(
        _gdn_scan_tile, i % SEQ_TILES, qa_ref, gb_ref, (z0_ref, z1_ref), tm_ref, aq_ref, ong_ref,
        wout_ref, yb_ref, ss_ref, obuf, s_scr))

    @pl.when(i == HALF_TILES)
    def _():
        yb_ref[0] = jnp.zeros((TOK_TILE, D_MODEL), F32)
        _sample_rows_tile(ys_ref, yb_ref.at[1])


def _gdn_scan_tile(t, qa_ref, gb_ref, z_refs, tm_ref, aq_ref, ong_ref, wout_ref,
                   yb_ref, ss_ref, obuf, s_scr):
    @pl.when(t == 0)
    def _():
        s_scr[...] = jnp.zeros((2, N_HEADS, HEAD_K, HEAD_V), F32)

    left = lax.broadcasted_iota(jnp.int32, (CHUNK, 128), 1) < CHUNK
    ong = ong_ref[...]

    def pair_diag(x):
        return jnp.concatenate([jnp.where(left, x, 0.0), jnp.where(left, 0.0, x)], axis=0)

    for c in range(SEQ_TILE // CHUNK):
        r0 = c * CHUNK
        for half in range(2):
            gbc = gb_ref[half, r0:r0 + CHUNK, :]
            eg = jnp.exp(gbc)
            glast = gbc[CHUNK - 1:CHUNK, :]
            kdec = jnp.exp(glast - gbc)
            dl = jnp.exp(glast)
            for hp in range(N_HEADS // 2):
                heads = (2 * hp, 2 * hp + 1)
                slab = 2 * c + hp
                tm = pair_diag(tm_ref[half, slab * M_PITCH:slab * M_PITCH + CHUNK, :])
                aq = pair_diag(aq_ref[half, slab * CHUNK:(slab + 1) * CHUNK, :])
                qg, kd, rhs = [], [], []
                for h in heads:
                    q = qa_ref[half, r0:r0 + CHUNK, h * HEAD_K:(h + 1) * HEAD_K]
                    k = qa_ref[half, r0:r0 + CHUNK, D_DELTA_K + h * HEAD_K:D_DELTA_K + (h + 1) * HEAD_K]
                    v = qa_ref[half, r0:r0 + CHUNK,
                               2 * D_DELTA_K + h * HEAD_V:2 * D_DELTA_K + (h + 1) * HEAD_V]
                    beta = gbc[:, N_HEADS + h:N_HEADS + h + 1]
                    egh = eg[:, GC_LANE + h:GC_LANE + h + 1]
                    qg.append(q * egh)
                    kd.append(k * kdec[:, GC_LANE + h:GC_LANE + h + 1])
                    rhs.append(jnp.concatenate([v * beta, k * beta * egh], axis=1))
                uw = _bdot(tm, jnp.concatenate(rhs, axis=0))
                vnews, qss = [], []
                for n, h in enumerate(heads):
                    sl = slice(n * CHUNK, (n + 1) * CHUNK)
                    s = s_scr[half, h]
                    wq = _bdot(jnp.concatenate([uw[sl, HEAD_V:], qg[n]], axis=0), s)
                    vnew = uw[sl, :HEAD_V] - wq[:CHUNK]
                    vnews.append(vnew)
                    qss.append(wq[CHUNK:])
                    s_scr[half, h] = s * dl[:, GC_LANE + h:GC_LANE + h + 1] + lax.dot_general(
                        kd[n].astype(BF16), vnew.astype(BF16), (((0,), (0,)), ((), ())),
                        preferred_element_type=F32)
                o = jnp.concatenate(qss, axis=0) + _bdot(aq, jnp.concatenate(vnews, axis=0))
                for n, h in enumerate(heads):
                    zh = z_refs[half][r0:r0 + CHUNK, h * HEAD_V:(h + 1) * HEAD_V]
                    obuf[half, r0:r0 + CHUNK, h * HEAD_V:(h + 1) * HEAD_V] = _gate_norm(
                        o[n * CHUNK:(n + 1) * CHUNK], ong, zh)

    for half in range(2):
        yb_ref[half] = _bdot(obuf[half], wout_ref[...])

    @pl.when(t == SEQ_TILES - 1)
    def _():
        ss_ref[...] = s_scr[...]


def _gdn_scan(qa, gb, z, tm, aq, yb_sample, ong, wout):
    last = HALF_TILES - 1
    halves = lambda a: a.reshape((2, a.shape[0] // 2) + a.shape[1:])
    both = lambda rows, w: pl.BlockSpec((2, rows, w), lambda i: (0, jnp.minimum(i, last), 0))
    ztile = lambda off: pl.BlockSpec((SEQ_TILE, D_DELTA_V),
                                     lambda i: (jnp.minimum(i, last) + off, 0))
    yb, ss = pl.pallas_call(
        _gdn_scan_kernel,
        grid=(HALF_TILES + 1,),
        in_specs=[both(SEQ_TILE, CONV_DIM), both(SEQ_TILE, 128), ztile(0), ztile(HALF_TILES),
                  both(M_ROWS_TILE, 128), both(SLABS_TILE * CHUNK, 128),
                  pl.BlockSpec((DEC_BATCH, D_MODEL), lambda i: (0, 0)),
                  pl.BlockSpec((1, 128), lambda i: (0, 0)),
                  pl.BlockSpec((D_DELTA_V, D_MODEL), lambda i: (0, 0))],
        out_specs=[pl.BlockSpec((2, SEQ_TILE, D_MODEL), lambda i: (0, i, 0)),
                   pl.BlockSpec((2, None, N_HEADS, HEAD_K, HEAD_V),
                                lambda i: (0, jnp.minimum(i // SEQ_TILES, HALF_SEQS - 1), 0, 0, 0))],
        out_shape=[jax.ShapeDtypeStruct((2, (HALF_TILES + 1) * TOK_TILE, D_MODEL), F32),
                   jax.ShapeDtypeStruct((2, HALF_SEQS, N_HEADS, HEAD_K, HEAD_V), F32)],
        scratch_shapes=[pltpu.VMEM((2, SEQ_TILE, D_DELTA_V), F32),
                        pltpu.VMEM((2, N_HEADS, HEAD_K, HEAD_V), F32)],
        compiler_params=pltpu.CompilerParams(
            dimension_semantics=("arbitrary",), vmem_limit_bytes=VMEM_BIG),
        name="gdn_scan",
    )(halves(qa), halves(gb), z, z, halves(tm), halves(aq), yb_sample, ong, wout)
    return yb, ss.reshape(BATCH, N_HEADS, HEAD_K, HEAD_V)


def _gdn_sample_kernel(qkv_ref, z_ref, ab_ref, st_ref, s_ref, cw_ref, alog_ref, dtb_ref,
                       ong_ref, wout_ref, yb_ref, nst_ref, ns_ref):
    x = qkv_ref[...]
    cw = cw_ref[...]
    nbuf = QK_CONV - 1
    acc = cw[nbuf:nbuf + 1, :] * x
    for j in range(nbuf):
        acc = acc + cw[j:j + 1, :] * st_ref[:, j * CONV_DIM:(j + 1) * CONV_DIM]
    nst_ref[:, 0:(nbuf - 1) * CONV_DIM] = st_ref[:, CONV_DIM:nbuf * CONV_DIM]
    nst_ref[:, (nbuf - 1) * CONV_DIM:nbuf * CONV_DIM] = x
    qs, ks, vs = _qkv_act(acc)
    ab = ab_ref[...]
    g = -jnp.exp(alog_ref[...]) * jax.nn.softplus(ab + dtb_ref[...])
    beta = jax.nn.sigmoid(ab)
    eg = jnp.exp(g)
    z = z_ref[...]
    outs = []
    for h in range(N_HEADS):
        bh = beta[:, N_HEADS + h:N_HEADS + h + 1]
        egh = eg[:, h:h + 1]
        kb = ks[h] * bh
        w = kb * egh
        qg = qs[h] * egh
        u = vs[h] * bh
        qk = jnp.sum(qs[h] * ks[h], axis=-1, keepdims=True)
        o_rows = []
        for s in range(DEC_BLK):
            st = s_ref[s, h]
            wq = _bdot(jnp.concatenate([w[s:s + 1], qg[s:s + 1]], axis=0), st)
            vnew = u[s:s + 1] - wq[0:1]
            o_rows.append(wq[1:2] + qk[s:s + 1] * vnew)
            ns_ref[s, h] = st * egh[s:s + 1] + lax.dot_general(
                ks[h][s:s + 1].astype(BF16), vnew.astype(BF16), (((0,), (0,)), ((), ())),
                preferred_element_type=F32)
        o = jnp.concatenate(o_rows, axis=0)
        outs.append(_gate_norm(o, ong_ref[...], z[:, h * HEAD_V:(h + 1) * HEAD_V]))
    yb_ref[...] = _bdot(jnp.concatenate(outs, axis=1), wout_ref[...])


def _gdn_sample(qkv, z, ab, st, state_delta, layer, cw, alog, dtb, ong, wout):
    nbuf = QK_CONV - 1
    blk0 = N_PROMPT // DEC_BLK
    rows = lambda w: pl.BlockSpec((DEC_BLK, w), lambda i: (blk0 + i, 0))
    vec = pl.BlockSpec((1, 128), lambda i: (0, 0))
    sblk = pl.BlockSpec((DEC_BLK, N_HEADS, HEAD_K, HEAD_V), lambda i: (i, 0, 0, 0))
    return pl.pallas_call(
        _gdn_sample_kernel,
        grid=(DEC_BATCH // DEC_BLK,),
        in_specs=[rows(CONV_DIM), rows(D_DELTA_V), rows(128),
                  pl.BlockSpec((DEC_BLK, nbuf * CONV_DIM), lambda i: (i, 0)),
                  pl.BlockSpec((None, DEC_BLK, N_HEADS, HEAD_K, HEAD_V),
                               lambda i: (layer, i, 0, 0, 0)),
                  pl.BlockSpec((QK_CONV, CONV_DIM), lambda i: (0, 0)),
                  vec, vec, vec,
                  pl.BlockSpec((D_DELTA_V, D_MODEL), lambda i: (0, 0))],
        out_specs=[pl.BlockSpec((DEC_BLK, D_MODEL), lambda i: (i, 0)),
                   pl.BlockSpec((DEC_BLK, nbuf * CONV_DIM), lambda i: (i, 0)),
                   sblk],
        out_shape=[jax.ShapeDtypeStruct((DEC_BATCH, D_MODEL), F32),
                   jax.ShapeDtypeStruct((DEC_BATCH, nbuf * CONV_DIM), F32),
                   jax.ShapeDtypeStruct((DEC_BATCH, N_HEADS, HEAD_K, HEAD_V), F32)],
        compiler_params=pltpu.CompilerParams(dimension_semantics=("arbitrary",)),
        name="gdn_sample",
    )(qkv, z, ab, st, state_delta, cw, alog, dtb, ong, wout)


def _router_kernel(x_ref, g_ref, r_ref, h_ref, info_ref, cnt_ref):
    h = _rms(x_ref[...], g_ref[...])
    h_ref[...] = h
    logits = _hdot(h, r_ref[...])
    lane = lax.broadcasted_iota(jnp.int32, logits.shape, 1)
    neg = jnp.float32(-jnp.inf)
    logits = jnp.where(lane < N_EXPERTS, logits, neg)
    m1 = jnp.max(logits, axis=-1, keepdims=True)
    i1 = jnp.min(jnp.where(logits == m1, lane, 128), axis=-1, keepdims=True)
    rest = jnp.where(lane == i1, neg, logits)
    m2 = jnp.max(rest, axis=-1, keepdims=True)
    i2 = jnp.min(jnp.where(rest == m2, lane, 128), axis=-1, keepdims=True)
    e = jnp.exp(m2 - m1)
    p1 = 1.0 / (1.0 + e)
    p2 = e / (1.0 + e)
    oh1 = jnp.where(lane == i1, 1.0, 0.0).astype(F32)
    oh2 = jnp.where(lane == i2, 1.0, 0.0).astype(F32)
    tr = lax.broadcasted_iota(jnp.int32, (TOK_TILE, TOK_TILE), 0)
    tc = lax.broadcasted_iota(jnp.int32, (TOK_TILE, TOK_TILE), 1)
    cum = _bdot(jnp.where(tr >= tc, 1.0, 0.0), oh1 + oh2)
    rank = cum - (oh1 + oh2)
    r1 = jnp.sum(oh1 * rank, axis=-1, keepdims=True)
    r2 = jnp.sum(oh2 * rank, axis=-1, keepdims=True)
    info = jnp.where(lane == 0, i1.astype(F32), 0.0)
    info = jnp.where(lane == 1, i2.astype(F32), info)
    info = jnp.where(lane == 2, p1, info)
    info = jnp.where(lane == 3, p2, info)
    info = jnp.where(lane == 4, r1, info)
    info = jnp.where(lane == 5, r2, info)
    info_ref[...] = info
    cnt_ref[...] = jnp.broadcast_to(cum[TOK_TILE - 1:TOK_TILE, :], (8, 128))


def _router(x, g, r_pad):
    row = lambda w: pl.BlockSpec((TOK_TILE, w), lambda i: (i, 0))
    return pl.pallas_call(
        _router_kernel,
        grid=(N_TOK // TOK_TILE,),
        in_specs=[row(D_MODEL), pl.BlockSpec((1, D_MODEL), lambda i: (0, 0)),
                  pl.BlockSpec((D_MODEL, 128), lambda i: (0, 0))],
        out_specs=[row(D_MODEL), row(128), pl.BlockSpec((None, 8, 128), lambda i: (i, 0, 0))],
        out_shape=[jax.ShapeDtypeStruct((N_TOK, D_MODEL), F32),
                   jax.ShapeDtypeStruct((N_TOK, 128), F32),
                   jax.ShapeDtypeStruct((N_TOK // TOK_TILE, 8, 128), F32)],
        compiler_params=pltpu.CompilerParams(dimension_semantics=("arbitrary",)),
        name="moe_router",
    )(x, g, r_pad)


def _tile_rows_wait(src_ref, dst_ref, sem):
    pltpu.make_async_copy(src_ref.at[pl.ds(0, TOK_TILE), :], dst_ref, sem).wait()


def _dispatch_kernel(dest_ref, xs_in_ref, h_ref, xs_ref, sem):
    del xs_in_ref
    base = pl.program_id(0) * (2 * TOK_TILE)

    def start(r, c):
        for k in range(2):
            pltpu.make_async_copy(
                h_ref.at[pl.ds(r, 1), :],
                xs_ref.at[pl.ds(dest_ref[base + 2 * r + k], 1), :], sem).start()
        return c

    lax.fori_loop(0, TOK_TILE, start, 0, unroll=8)
    for _ in range(2):
        _tile_rows_wait(h_ref, xs_ref.at[pl.ds(0, TOK_TILE), :], sem)


def _dispatch(dest, xs0, h):
    return pl.pallas_call(
        _dispatch_kernel,
        grid_spec=pltpu.PrefetchScalarGridSpec(
            num_scalar_prefetch=1,
            grid=(N_TOK // TOK_TILE,),
            in_specs=[pl.BlockSpec(memory_space=pl.ANY),
                      pl.BlockSpec((TOK_TILE, D_MODEL), lambda i, d: (i, 0))],
            out_specs=pl.BlockSpec(memory_space=pl.ANY),
            scratch_shapes=[pltpu.SemaphoreType.DMA(())]),
        out_shape=jax.ShapeDtypeStruct((MOE_ROWS, D_MODEL), F32),
        input_output_aliases={1: 0},
        compiler_params=pltpu.CompilerParams(dimension_semantics=("arbitrary",)),
        name="moe_dispatch",
    )(dest, xs0, h)


def _experts_kernel(te_ref, tv_ref, x_ref, wg_ref, wu_ref, wd_ref, o_ref):
    i = pl.program_id(0)

    @pl.when(tv_ref[i] > 0)
    def _():
        o_ref[...] = _swiglu_rows(x_ref[...].astype(BF16), wg_ref, wu_ref, wd_ref)

    @pl.when(tv_ref[i] == 0)
    def _():
        o_ref[...] = jnp.zeros((MOE_TILE, D_MODEL), F32)


def _experts(tile_expert, tile_valid, xs, wg, wu, wd):
    row = pl.BlockSpec((MOE_TILE, D_MODEL), lambda i, te, tv: (i, 0))
    return pl.pallas_call(
        _experts_kernel,
        grid_spec=pltpu.PrefetchScalarGridSpec(
            num_scalar_prefetch=2,
            grid=(MOE_ROWS // MOE_TILE,),
            in_specs=[row,
                      pl.BlockSpec((None, D_MODEL, D_FF), lambda i, te, tv: (te[i], 0, 0)),
                      pl.BlockSpec((None, D_MODEL, D_FF), lambda i, te, tv: (te[i], 0, 0)),
                      pl.BlockSpec((None, D_FF, D_MODEL), lambda i, te, tv: (te[i], 0, 0))],
            out_specs=row),
        out_shape=jax.ShapeDtypeStruct((MOE_ROWS, D_MODEL), F32),
        compiler_params=pltpu.CompilerParams(
            dimension_semantics=("arbitrary",), vmem_limit_bytes=VMEM_BIG),
        name="moe_experts",
    )(tile_expert, tile_valid, xs, wg, wu, wd)


def _combine_kernel(dest_ref, x_ref, info_ref, g_ref, ys_ref, yp_ref, ysm_ref, ybuf, sem):
    i = pl.program_id(0)
    n = pl.num_programs(0)

    def start_tile(tile):
        slot = tile % 2
        base = tile * (2 * TOK_TILE)

        def start(r, c):
            for k in range(2):
                pltpu.make_async_copy(
                    ys_ref.at[pl.ds(dest_ref[base + 2 * r + k], 1), :],
                    ybuf.at[slot, k, pl.ds(r, 1), :], sem.at[slot]).start()
            return c

        lax.fori_loop(0, TOK_TILE, start, 0, unroll=8)

    @pl.when(i == 0)
    def _():
        start_tile(i)

    @pl.when(i + 1 < n)
    def _():
        start_tile(i + 1)

    slot = i % 2
    for k in range(2):
        _tile_rows_wait(ys_ref, ybuf.at[slot, k], sem.at[slot])
    info = info_ref[...]
    x = x_ref[...] + info[:, 2:3] * ybuf[slot, 0] + info[:, 3:4] * ybuf[slot, 1]
    y = _rms(x, g_ref[...])

    @pl.when(i < PROMPT_TILES)
    def _():
        yp_ref[...] = y

    @pl.when(i == PROMPT_TILES)
    def _():
        ysm_ref[...] = y[0:DEC_BATCH, :]


def _combine(dest, x, info, g, ys):
    last = PROMPT_TILES - 1
    return pl.pallas_call(
        _combine_kernel,
        grid_spec=pltpu.PrefetchScalarGridSpec(
            num_scalar_prefetch=1,
            grid=(N_TOK // TOK_TILE,),
            in_specs=[pl.BlockSpec((TOK_TILE, D_MODEL), lambda i, d: (i, 0)),
                      pl.BlockSpec((TOK_TILE, 128), lambda i, d: (i, 0)),
                      pl.BlockSpec((1, D_MODEL), lambda i, d: (0, 0)),
                      pl.BlockSpec(memory_space=pl.ANY)],
            out_specs=[pl.BlockSpec((TOK_TILE, D_MODEL), lambda i, d: (jnp.minimum(i, last), 0)),
                       pl.BlockSpec((DEC_BATCH, D_MODEL), lambda i, d: (0, 0))],
            scratch_shapes=[pltpu.VMEM((2, 2, TOK_TILE, D_MODEL), F32),
                            pltpu.SemaphoreType.DMA((2,))]),
        out_shape=[jax.ShapeDtypeStruct((N_PROMPT, D_MODEL), F32),
                   jax.ShapeDtypeStruct((DEC_BATCH, D_MODEL), F32)],
        compiler_params=pltpu.CompilerParams(dimension_semantics=("arbitrary",)),
        name="moe_combine",
    )(dest, x, info, g, ys)


def _moe_layer(x, g2, router_w, wg, wu, wd, final_g):
    r_pad = jnp.zeros((D_MODEL, 128), F32).at[:, :N_EXPERTS].set(router_w)
    h, info, cnt = _router(x, g2, r_pad)
    n_tiles = N_TOK // TOK_TILE
    counts = cnt[:, 0, :N_EXPERTS].astype(jnp.int32)
    total = jnp.sum(counts, axis=0)
    padded = ((total + MOE_TILE - 1) // MOE_TILE) * MOE_TILE
    ends = jnp.cumsum(padded)
    starts = ends - padded
    base = starts[None, :] + jnp.cumsum(counts, axis=0) - counts
    choice = info[:, 0:2].astype(jnp.int32).reshape(n_tiles, TOK_TILE, 2)
    rank = info[:, 4:6].astype(jnp.int32).reshape(n_tiles, TOK_TILE, 2)
    onehot = choice[..., None] == jnp.arange(N_EXPERTS, dtype=jnp.int32)
    dest = (jnp.sum(jnp.where(onehot, base[:, None, None, :], 0), axis=-1) + rank).reshape(-1)
    tile_start = jnp.arange(MOE_ROWS // MOE_TILE, dtype=jnp.int32) * MOE_TILE
    tile_expert = jnp.minimum(
        jnp.sum((tile_start[:, None] >= ends[None, :]).astype(jnp.int32), axis=1), N_EXPERTS - 1)
    tile_valid = (tile_start < ends[-1]).astype(jnp.int32)
    last_used = jnp.max(jnp.where(tile_valid > 0, tile_expert, 0))
    tile_expert = jnp.where(tile_valid > 0, tile_expert, last_used).astype(jnp.int32)

    xs = _dispatch(dest, jnp.zeros((MOE_ROWS, D_MODEL), F32), h)
    ys = _experts(tile_expert, tile_valid, xs, wg, wu, wd)
    return _combine(dest, x, info, final_g, ys)


def _rearranged_w_in(w_in):
    off_qkv = 2 * D_CONV
    off_z = off_qkv + CONV_DIM
    off_a = off_z + D_DELTA_V
    off_ga = off_a + 2 * N_HEADS
    off_gb = off_ga + D_MODEL
    return jnp.concatenate(
        [w_in[:, :off_a], w_in[:, off_ga:off_gb], w_in[:, off_gb:], w_in[:, off_a:off_ga],
         jnp.zeros((D_MODEL, 128 - 2 * N_HEADS), w_in.dtype)], axis=1).astype(BF16)


def _lane_vec(v, offset=0):
    return jnp.zeros((1, 128), F32).at[0, offset:offset + v.shape[0]].set(v)


def kernel(x_prompt, x_sample, state_delta, state_qkv_conv, state_dwconv, norm1_g, w_in, qkv_conv_w, a_log, dt_bias, o_norm_g, w_delta_out, dw_w, dw_b, ln_g, ln_b, w_conv_out, w_o, norm2_g, ffn_w_gate, ffn_w_up, ffn_w_down, router_w, exp_w_gate, exp_w_up, exp_w_down, final_norm_g):
    x = (x_prompt.reshape(N_PROMPT, D_MODEL), x_sample.reshape(DEC_BATCH, D_MODEL))
    s_p, q_p, d_p, s_s, q_s, d_s = [], [], [], [], [], []
    for l in range(DEPTH):
        glu, qkv, z, ga, gb, ab = _norm_proj(x, norm1_g[l][None, :], _rearranged_w_in(w_in[l]))
        dwb, lng, lnb = dw_b[l][None, :], ln_g[l][None, :], ln_b[l][None, :]
        wco = w_conv_out[l].astype(BF16)
        ya_s, dst_s = _conv_sample(glu, state_dwconv[l].reshape(DEC_BATCH, -1), dw_w[l], dwb, lng, lnb, wco)
        ya, dst_p = _conv_prompt(glu, ya_s, dw_w[l], dwb, lng, lnb, wco)
        alog, dtb, ong = _lane_vec(a_log[l]), _lane_vec(dt_bias[l]), o_norm_g[l][None, :]
        wdo = w_delta_out[l].astype(BF16)
        yb_s, qst_s, sst_s = _gdn_sample(qkv, z, ab, state_qkv_conv[l].reshape(DEC_BATCH, -1),
                                         state_delta, l, qkv_conv_w[l], alog, dtb, ong, wdo)
        qa, gbeta, m, aq, qst_p = _gdn_pre(qkv, ab, qkv_conv_w[l], alog, dtb)
        yb, sst_p = _gdn_scan(qa, gbeta, z, _gdn_solve(m), aq, yb_s, ong, wdo)
        x = _merge(x, ya, yb, ga, gb, w_o[l].astype(BF16))
        i = l // 2
        if l % 2 == 0:
            x = _ffn(x, norm2_g[l][None, :], ffn_w_gate[i].astype(BF16), ffn_w_up[i].astype(BF16),
                     ffn_w_down[i].astype(BF16))
        else:
            y_prompt, y_sample = _moe_layer(
                x, norm2_g[l][None, :], router_w[i], exp_w_gate[i].astype(BF16),
                exp_w_up[i].astype(BF16), exp_w_down[i].astype(BF16), final_norm_g[None, :])
        s_p.append(sst_p)
        q_p.append(qst_p)
        d_p.append(dst_p)
        s_s.append(sst_s)
        q_s.append(qst_s.reshape(DEC_BATCH, QK_CONV - 1, CONV_DIM))
        d_s.append(dst_s.reshape(DEC_BATCH, DW_WIDTH - 1, D_CONV))
    return (y_prompt.reshape(BATCH, SEQ, D_MODEL), y_sample.reshape(DEC_BATCH, 1, D_MODEL), jnp.stack(s_p), jnp.stack(q_p), jnp.stack(d_p),
            jnp.stack(s_s), jnp.stack(q_s), jnp.stack(d_s))
```

```python
import functools

import jax
import jax.numpy as jnp
from jax import lax
from jax.experimental import pallas as pl
from jax.experimental.pallas import tpu as pltpu

F32 = jnp.float32
BF16 = jnp.bfloat16
HIGHEST = lax.Precision.HIGHEST

D_MODEL = 1024
BATCH = 8
SEQ = 2048
DEPTH = 2
DEC_BATCH = 128
D_CONV = 512
DW_WIDTH = 31
N_HEADS = 4
HEAD_K = 128
HEAD_V = 128
QK_CONV = 4
D_DELTA_K = N_HEADS * HEAD_K
D_DELTA_V = N_HEADS * HEAD_V
CONV_DIM = 2 * D_DELTA_K + D_DELTA_V
D_FF = 2816
N_EXPERTS = 8
EPS = 1e-6

N_PROMPT = BATCH * SEQ
TOK_TILE = 512
SEQ_TILE = TOK_TILE
N_TOK = N_PROMPT + TOK_TILE
SEQ_TILES = SEQ // SEQ_TILE
PROMPT_TILES = N_PROMPT // TOK_TILE
HALF_TILES = PROMPT_TILES // 2
HALF_SEQS = BATCH // 2
CHUNK = 64
M_PITCH = 68
SLABS_TILE = 2 * (SEQ_TILE // CHUNK)
M_ROWS_TILE = SLABS_TILE * M_PITCH
SOLVE_SLABS = 128
ROW_BLK = 64
FF_HALF = D_FF // 2
MOE_TILE = 256
MOE_ROWS = ((2 * N_TOK + N_EXPERTS * (MOE_TILE - 1)) // MOE_TILE + 1) * MOE_TILE
DEC_BLK = 8

C_GLU, C_QKV, C_Z, C_GA, C_GB, C_AB = 0, 1024, 2560, 3072, 4096, 5120
N_PROJ = 5248

VMEM_BIG = 56 * 1024 * 1024


def _bdot(a, b):
    return jnp.dot(a.astype(BF16), b.astype(BF16), preferred_element_type=F32)


def _hdot(a, b):
    return jnp.dot(a, b, preferred_element_type=F32, precision=HIGHEST)


def _rms(x, g):
    return x * lax.rsqrt(jnp.mean(x * x, axis=-1, keepdims=True) + EPS) * g


def _silu(x):
    return x * jax.nn.sigmoid(x)


def _resident(shape):
    return pl.BlockSpec(shape, lambda *_: (0,) * len(shape), pipeline_mode=pl.Buffered(1))


def _stream_x_specs(x):
    if isinstance(x, tuple):
        last = PROMPT_TILES - 1
        return ([pl.BlockSpec((TOK_TILE, D_MODEL), lambda i: (jnp.minimum(i, last), 0)),
                 pl.BlockSpec((DEC_BATCH, D_MODEL), lambda i: (0, 0))], list(x))
    return [pl.BlockSpec((TOK_TILE, D_MODEL), lambda i: (i, 0))], [x]


def _stream_x_tile(x_refs, xbuf):
    if len(x_refs) == 1:
        return x_refs[0][...]
    xp_ref, xs_ref = x_refs
    i = pl.program_id(0)

    @pl.when(i < PROMPT_TILES)
    def _():
        xbuf[...] = xp_ref[...]

    @pl.when(i == PROMPT_TILES)
    def _():
        xbuf[0:DEC_BATCH, :] = xs_ref[...]
        xbuf[DEC_BATCH:, :] = jnp.zeros((TOK_TILE - DEC_BATCH, D_MODEL), F32)

    return xbuf[...]


def _norm_proj_kernel(n_x, *refs):
    x_refs = refs[:n_x]
    g_ref, w_ref, glu_ref, qkv_ref, z_ref, ga_ref, gb_ref, ab_ref, xbuf = refs[n_x:]
    h = _rms(_stream_x_tile(x_refs, xbuf), g_ref[...]).astype(BF16)

    def proj(lo, hi):
        return jnp.dot(h, w_ref[:, lo:hi], preferred_element_type=F32)

    glu_ref[...] = proj(C_GLU, C_QKV)
    qkv_ref[...] = proj(C_QKV, C_Z)
    z_ref[...] = proj(C_Z, C_GA)
    ga_ref[...] = proj(C_GA, C_GB)
    gb_ref[...] = proj(C_GB, C_AB)
    ab_ref[...] = proj(C_AB, N_PROJ)


def _norm_proj(x, g, w_all):
    row = lambda w: pl.BlockSpec((TOK_TILE, w), lambda i: (i, 0))
    widths = (1024, CONV_DIM, D_DELTA_V, D_MODEL, D_MODEL, 128)
    x_specs, x_ops = _stream_x_specs(x)
    return pl.pallas_call(
        functools.partial(_norm_proj_kernel, len(x_ops)),
        grid=(N_TOK // TOK_TILE,),
        in_specs=x_specs + [pl.BlockSpec((1, D_MODEL), lambda i: (0, 0)),
                            _resident((D_MODEL, N_PROJ))],
        out_specs=[row(w) for w in widths],
        out_shape=[jax.ShapeDtypeStruct((N_TOK, w), F32) for w in widths],
        scratch_shapes=[pltpu.VMEM((TOK_TILE, D_MODEL), F32)],
        compiler_params=pltpu.CompilerParams(
            dimension_semantics=("arbitrary",), vmem_limit_bytes=VMEM_BIG),
        name="norm_proj",
    )(*x_ops, g, w_all)


def _merge_kernel(n_x, *refs):
    x_refs = refs[:n_x]
    ya_ref, yb_ref, ga_ref, gb_ref, wo_ref, o_ref, xbuf = refs[n_x:]
    m = jax.nn.sigmoid(ga_ref[...]) * ya_ref[...] + jax.nn.sigmoid(gb_ref[...]) * yb_ref[...]
    o_ref[...] = _stream_x_tile(x_refs, xbuf) + _bdot(m, wo_ref[...])


def _merge(x, ya, yb, ga, gb, wo):
    row = pl.BlockSpec((TOK_TILE, D_MODEL), lambda i: (i, 0))
    x_specs, x_ops = _stream_x_specs(x)
    yb_spec = pl.BlockSpec((None, TOK_TILE, D_MODEL),
                           lambda i: (i // HALF_TILES - i // (2 * HALF_TILES),
                                      i - HALF_TILES * (i // HALF_TILES - i // (2 * HALF_TILES)), 0))
    return pl.pallas_call(
        functools.partial(_merge_kernel, len(x_ops)),
        grid=(N_TOK // TOK_TILE,),
        in_specs=x_specs + [row, yb_spec, row, row,
                            pl.BlockSpec((D_MODEL, D_MODEL), lambda i: (0, 0))],
        out_specs=row,
        out_shape=jax.ShapeDtypeStruct((N_TOK, D_MODEL), F32),
        scratch_shapes=[pltpu.VMEM((TOK_TILE, D_MODEL), F32)],
        compiler_params=pltpu.CompilerParams(dimension_semantics=("arbitrary",)),
        name="merge",
    )(*x_ops, ya, yb, ga, gb, wo)


def _swiglu_rows(h, wg_ref, wu_ref, wd_ref):
    acc = None
    for lo in (0, FF_HALF):
        a = jnp.dot(h, wg_ref[:, lo:lo + FF_HALF], preferred_element_type=F32)
        b = jnp.dot(h, wu_ref[:, lo:lo + FF_HALF], preferred_element_type=F32)
        part = jnp.dot((_silu(a) * b).astype(BF16), wd_ref[lo:lo + FF_HALF, :],
                       preferred_element_type=F32)
        acc = part if acc is None else acc + part
    return acc


def _ffn_kernel(x_ref, g_ref, wg_ref, wu_ref, wd_ref, o_ref):
    x = x_ref[...]
    h = _rms(x, g_ref[...]).astype(BF16)
    o_ref[...] = x + _swiglu_rows(h, wg_ref, wu_ref, wd_ref)


def _ffn(x, g, wg, wu, wd):
    row = pl.BlockSpec((TOK_TILE, D_MODEL), lambda i: (i, 0))
    return pl.pallas_call(
        _ffn_kernel,
        grid=(N_TOK // TOK_TILE,),
        in_specs=[row, pl.BlockSpec((1, D_MODEL), lambda i: (0, 0)),
                  _resident((D_MODEL, D_FF)), _resident((D_MODEL, D_FF)),
                  _resident((D_FF, D_MODEL))],
        out_specs=row,
        out_shape=jax.ShapeDtypeStruct((N_TOK, D_MODEL), F32),
        compiler_params=pltpu.CompilerParams(
            dimension_semantics=("arbitrary",), vmem_limit_bytes=VMEM_BIG),
        name="ffn_dense",
    )(x, g, wg, wu, wd)


def _conv_tail(c, dwb, lng, lnb):
    c = c + dwb
    mu = jnp.mean(c, axis=-1, keepdims=True)
    var = jnp.mean(jnp.square(c - mu), axis=-1, keepdims=True)
    c = (c - mu) * lax.rsqrt(var + EPS) * lng + lnb
    return _silu(c)


def _sample_rows_tile(ys_ref, y_ref):
    y_ref[0:DEC_BATCH, :] = ys_ref[...]
    y_ref[DEC_BATCH:, :] = jnp.zeros((TOK_TILE - DEC_BATCH, D_MODEL), F32)


def _conv_prompt_kernel(glu_ref, ys_ref, dww_ref, dwb_ref, lng_ref, lnb_ref, wout_ref,
                        ya_ref, st_ref, ubuf, ushift, cbuf):
    i = pl.program_id(0)
    pl.when(i < PROMPT_TILES)(functools.partial(
        _conv_prompt_tile, i % SEQ_TILES, glu_ref, dww_ref, dwb_ref, lng_ref, lnb_ref, wout_ref,
        ya_ref, st_ref, ubuf, ushift, cbuf))
    pl.when(i == PROMPT_TILES)(functools.partial(_sample_rows_tile, ys_ref, ya_ref))


def _conv_prompt_tile(t, glu_ref, dww_ref, dwb_ref, lng_ref, lnb_ref, wout_ref,
                      ya_ref, st_ref, ubuf, ushift, cbuf):
    halo = 32

    @pl.when(t == 0)
    def _():
        ubuf[0:halo, :] = jnp.zeros((halo, D_CONV), F32)

    @pl.when(t > 0)
    def _():
        ubuf[0:halo, :] = ubuf[SEQ_TILE:SEQ_TILE + halo, :]

    glu = glu_ref[...]
    ubuf[halo:halo + SEQ_TILE, :] = glu[:, :D_CONV] * jax.nn.sigmoid(glu[:, D_CONV:])
    for r in range(1, 8):
        ushift[r - 1, 0:SEQ_TILE + halo - 8, :] = ubuf[r:r + SEQ_TILE + halo - 8, :]
    w = dww_ref[...]
    first = halo - (DW_WIDTH - 1)
    for rb in range(SEQ_TILE // ROW_BLK):
        r0 = rb * ROW_BLK
        acc = jnp.zeros((ROW_BLK, D_CONV), F32)
        for j in range(DW_WIDTH):
            a, r = divmod(first + j, 8)
            if r == 0:
                win = ubuf[r0 + 8 * a:r0 + 8 * a + ROW_BLK, :]
            else:
                win = ushift[r - 1, r0 + 8 * a:r0 + 8 * a + ROW_BLK, :]
            acc = acc + w[j:j + 1, :] * win
        c = _conv_tail(acc, dwb_ref[...], lng_ref[...], lnb_ref[...])
        cbuf[r0:r0 + ROW_BLK, :] = c.astype(BF16)
    ya_ref[...] = jnp.dot(cbuf[...], wout_ref[...], preferred_element_type=F32)

    @pl.when(t == SEQ_TILES - 1)
    def _():
        st_ref[...] = ubuf[halo + SEQ_TILE - (DW_WIDTH - 1):halo + SEQ_TILE, :]


def _seq_of_tile(i):
    return jnp.minimum(i // SEQ_TILES, BATCH - 1)


def _conv_prompt(glu, ya_sample, dww, dwb, lng, lnb, wout):
    vec = pl.BlockSpec((1, D_CONV), lambda i: (0, 0))
    return pl.pallas_call(
        _conv_prompt_kernel,
        grid=(PROMPT_TILES + 1,),
        in_specs=[pl.BlockSpec((SEQ_TILE, 2 * D_CONV), lambda i: (i, 0)),
                  pl.BlockSpec((DEC_BATCH, D_MODEL), lambda i: (0, 0)),
                  pl.BlockSpec((DW_WIDTH, D_CONV), lambda i: (0, 0)),
                  vec, vec, vec,
                  pl.BlockSpec((D_CONV, D_MODEL), lambda i: (0, 0))],
        out_specs=[pl.BlockSpec((SEQ_TILE, D_MODEL), lambda i: (i, 0)),
                   pl.BlockSpec((None, DW_WIDTH - 1, D_CONV), lambda i: (_seq_of_tile(i), 0, 0))],
        out_shape=[jax.ShapeDtypeStruct((N_TOK, D_MODEL), F32),
                   jax.ShapeDtypeStruct((BATCH, DW_WIDTH - 1, D_CONV), F32)],
        scratch_shapes=[pltpu.VMEM((SEQ_TILE + 32, D_CONV), F32),
                        pltpu.VMEM((7, SEQ_TILE + 24, D_CONV), F32),
                        pltpu.VMEM((SEQ_TILE, D_CONV), BF16)],
        compiler_params=pltpu.CompilerParams(dimension_semantics=("arbitrary",)),
        name="conv_prompt",
    )(glu, ya_sample, dww, dwb, lng, lnb, wout)


def _conv_sample_kernel(glu_ref, st_ref, dww_ref, dwb_ref, lng_ref, lnb_ref, wout_ref,
                        ya_ref, nst_ref):
    glu = glu_ref[...]
    u = glu[:, :D_CONV] * jax.nn.sigmoid(glu[:, D_CONV:])
    w = dww_ref[...]
    nbuf = DW_WIDTH - 1
    acc = w[nbuf:nbuf + 1, :] * u
    for j in range(nbuf):
        acc = acc + w[j:j + 1, :] * st_ref[:, j * D_CONV:(j + 1) * D_CONV]
    c = _conv_tail(acc, dwb_ref[...], lng_ref[...], lnb_ref[...])
    ya_ref[...] = _bdot(c, wout_ref[...])
    nst_ref[:, 0:(nbuf - 1) * D_CONV] = st_ref[:, D_CONV:nbuf * D_CONV]
    nst_ref[:, (nbuf - 1) * D_CONV:nbuf * D_CONV] = u


def _conv_sample(glu, st, dww, dwb, lng, lnb, wout):
    nbuf = DW_WIDTH - 1
    blk = N_PROMPT // DEC_BATCH
    vec = pl.BlockSpec((1, D_CONV), lambda i: (0, 0))
    return pl.pallas_call(
        _conv_sample_kernel,
        grid=(1,),
        in_specs=[pl.BlockSpec((DEC_BATCH, 2 * D_CONV), lambda i: (blk, 0)),
                  pl.BlockSpec((DEC_BATCH, nbuf * D_CONV), lambda i: (0, 0)),
                  pl.BlockSpec((DW_WIDTH, D_CONV), lambda i: (0, 0)),
                  vec, vec, vec,
                  pl.BlockSpec((D_CONV, D_MODEL), lambda i: (0, 0))],
        out_specs=[pl.BlockSpec((DEC_BATCH, D_MODEL), lambda i: (0, 0)),
                   pl.BlockSpec((DEC_BATCH, nbuf * D_CONV), lambda i: (0, 0))],
        out_shape=[jax.ShapeDtypeStruct((DEC_BATCH, D_MODEL), F32),
                   jax.ShapeDtypeStruct((DEC_BATCH, nbuf * D_CONV), F32)],
        compiler_params=pltpu.CompilerParams(dimension_semantics=("arbitrary",)),
        name="conv_sample",
    )(glu, st, dww, dwb, lng, lnb, wout)


def _qkv_act(c):
    c = _silu(c)
    qs, ks, vs = [], [], []
    for h in range(N_HEADS):
        q = c[:, h * HEAD_K:(h + 1) * HEAD_K]
        k = c[:, D_DELTA_K + h * HEAD_K:D_DELTA_K + (h + 1) * HEAD_K]
        qs.append(q * lax.rsqrt(jnp.sum(q * q, axis=-1, keepdims=True) + EPS) * (HEAD_K ** -0.5))
        ks.append(k * lax.rsqrt(jnp.sum(k * k, axis=-1, keepdims=True) + EPS))
        vs.append(c[:, 2 * D_DELTA_K + h * HEAD_V:2 * D_DELTA_K + (h + 1) * HEAD_V])
    return qs, ks, vs


def _gate_norm(o, ong, z):
    o = o * lax.rsqrt(jnp.mean(o * o, axis=-1, keepdims=True) + EPS) * ong
    return o * _silu(z)


def _gdn_pre_kernel(qkv_ref, ab_ref, cw_ref, alog_ref, dtb_ref,
                    rp_ref, qg_ref, kt_ref, dl_ref, m_ref, aq_ref, sq_ref, xbuf):
    t = pl.program_id(0) % SEQ_TILES
    halo = 8

    @pl.when(t == 0)
    def _():
        xbuf[0:halo, :] = jnp.zeros((halo, CONV_DIM), F32)

    @pl.when(t > 0)
    def _():
        xbuf[0:halo, :] = xbuf[SEQ_TILE:SEQ_TILE + halo, :]

    xbuf[halo:halo + SEQ_TILE, :] = qkv_ref[...]

    @pl.when(t == SEQ_TILES - 1)
    def _():
        sq_ref[...] = xbuf[halo + SEQ_TILE - (QK_CONV - 1):halo + SEQ_TILE, :]

    ab = ab_ref[...]
    lane = lax.broadcasted_iota(jnp.int32, (SEQ_TILE, 128), 1)
    gb = jnp.where(lane < N_HEADS,
                   -jnp.exp(alog_ref[...]) * jax.nn.softplus(ab + dtb_ref[...]),
                   jax.nn.sigmoid(ab))
    cw = cw_ref[...]
    first = halo - (QK_CONV - 1)
    ri = lax.broadcasted_iota(jnp.int32, (CHUNK, 128), 0)
    li = lax.broadcasted_iota(jnp.int32, (CHUNK, 128), 1)
    left = li < CHUNK
    jj = jnp.where(left, li, li - CHUNK)
    strict = ri > jj
    tril = ri >= jj
    r64 = lax.broadcasted_iota(jnp.int32, (CHUNK, CHUNK), 0)
    c64 = lax.broadcasted_iota(jnp.int32, (CHUNK, CHUNK), 1)
    csum = jnp.where(r64 >= c64, 1.0, 0.0).astype(F32)
    for c in range(SEQ_TILE // CHUNK):
        r0 = c * CHUNK
        acc = cw[0:1, :] * xbuf[r0 + first:r0 + first + CHUNK, :]
        for j in range(1, QK_CONV):
            acc = acc + cw[j:j + 1, :] * xbuf[r0 + first + j:r0 + first + j + CHUNK, :]
        qs, ks, vs = _qkv_act(acc)
        gbc = gb[r0:r0 + CHUNK, :]
        gc = _hdot(csum, gbc)
        gct = gc.T
        eg = jnp.exp(gc)
        glast = gc[CHUNK - 1:CHUNK, :]
        kdec = jnp.exp(glast - gc)
        dl = jnp.exp(glast)
        for hp in range(N_HEADS // 2):
            h0, h1 = 2 * hp, 2 * hp + 1
            slab = 2 * c + hp
            kb, rhs, qg, kd = [], [], [], []
            for h in (h0, h1):
                beta = gbc[:, N_HEADS + h:N_HEADS + h + 1]
                egh = eg[:, h:h + 1]
                kb.append(ks[h] * beta)
                rhs.append(jnp.concatenate([vs[h] * beta, kb[-1] * egh], axis=1))
                qg.append(qs[h] * egh)
                kd.append(ks[h] * kdec[:, h:h + 1])
            rows = slice(slab * 2 * CHUNK, (slab + 1) * 2 * CHUNK)
            rp_ref[rows, :] = jnp.concatenate(rhs, axis=0).astype(BF16)
            qg_ref[rows, :] = jnp.concatenate(qg, axis=0).astype(BF16)
            kt_ref[rows, :] = jnp.concatenate(kd, axis=0).T.astype(BF16)
            dl_ref[slab * 8:(slab + 1) * 8, :] = jnp.concatenate(
                [jnp.broadcast_to(dl[:, h0:h0 + 1], (8, HEAD_V)),
                 jnp.broadcast_to(dl[:, h1:h1 + 1], (8, HEAD_V))], axis=1)
            lhs = jnp.concatenate(kb + [qs[h0], qs[h1]], axis=0)
            keys = jnp.concatenate([ks[h0], ks[h1]], axis=0)
            prod = lax.dot_general(lhs.astype(BF16), keys.astype(BF16), (((1,), (1,)), ((), ())),
                                   preferred_element_type=F32)
            kk = jnp.where(left, prod[0:CHUNK], prod[CHUNK:2 * CHUNK])
            qk = jnp.where(left, prod[2 * CHUNK:3 * CHUNK], prod[3 * CHUNK:4 * CHUNK])
            gcol = jnp.where(left, gc[:, h0:h0 + 1], gc[:, h1:h1 + 1])
            grow = jnp.concatenate([gct[h0:h0 + 1, :], gct[h1:h1 + 1, :]], axis=1)
            diff = gcol - grow
            m_ref[slab * M_PITCH:slab * M_PITCH + CHUNK, :] = jnp.where(
                strict, kk * jnp.exp(jnp.where(strict, diff, 0.0)), 0.0)
            m_ref[slab * M_PITCH + CHUNK:(slab + 1) * M_PITCH, :] = jnp.zeros(
                (M_PITCH - CHUNK, 128), F32)
            aq_ref[slab * CHUNK:(slab + 1) * CHUNK, :] = jnp.where(
                tril, qk * jnp.exp(jnp.where(tril, diff, 0.0)), 0.0).astype(BF16)


def _gdn_pre(qkv, ab, cw, alog, dtb):
    tile = lambda w: pl.BlockSpec((SEQ_TILE, w), lambda i: (i, 0))
    vec = pl.BlockSpec((1, 128), lambda i: (0, 0))
    slab_rows = SLABS_TILE * 2 * CHUNK
    per_tile = lambda rows, w: pl.BlockSpec((rows, w), lambda i: (i, 0))
    return pl.pallas_call(
        _gdn_pre_kernel,
        grid=(PROMPT_TILES,),
        in_specs=[tile(CONV_DIM), tile(128),
                  pl.BlockSpec((QK_CONV, CONV_DIM), lambda i: (0, 0)), vec, vec],
        out_specs=[per_tile(slab_rows, 2 * HEAD_V), per_tile(slab_rows, HEAD_K),
                   per_tile(slab_rows, 128), per_tile(SLABS_TILE * 8, 2 * HEAD_V),
                   per_tile(M_ROWS_TILE, 128), per_tile(SLABS_TILE * CHUNK, 128),
                   pl.BlockSpec((None, QK_CONV - 1, CONV_DIM), lambda i: (i // SEQ_TILES, 0, 0))],
        out_shape=[jax.ShapeDtypeStruct((PROMPT_TILES * slab_rows, 2 * HEAD_V), BF16),
                   jax.ShapeDtypeStruct((PROMPT_TILES * slab_rows, HEAD_K), BF16),
                   jax.ShapeDtypeStruct((PROMPT_TILES * slab_rows, 128), BF16),
                   jax.ShapeDtypeStruct((PROMPT_TILES * SLABS_TILE * 8, 2 * HEAD_V), F32),
                   jax.ShapeDtypeStruct((PROMPT_TILES * M_ROWS_TILE, 128), F32),
                   jax.ShapeDtypeStruct((PROMPT_TILES * SLABS_TILE * CHUNK, 128), BF16),
                   jax.ShapeDtypeStruct((BATCH, QK_CONV - 1, CONV_DIM), F32)],
        scratch_shapes=[pltpu.VMEM((SEQ_TILE + 8, CONV_DIM), F32)],
        compiler_params=pltpu.CompilerParams(dimension_semantics=("arbitrary",)),
        name="gdn_pre",
    )(qkv, ab, cw, alog, dtb)


def _gdn_solve_kernel(m_ref, t_ref, mt, xt):
    def to_lanes(i, c):
        g = m_ref[pl.ds(i, SOLVE_SLABS, stride=M_PITCH), :]
        mt[pl.ds(pl.multiple_of(i * 128, 128), 128), :] = g.T
        return c

    lax.fori_loop(0, CHUNK, to_lanes, 0, unroll=4)

    row = lax.broadcasted_iota(jnp.int32, (CHUNK, SOLVE_SLABS), 0)

    def solve_row(i, c):
        base = pl.multiple_of(i * 128, 128)
        unit = jnp.where(row == i, 1.0, 0.0).astype(F32)

        def step(j, acc):
            a0, a1 = acc
            xb = pl.multiple_of(j * 128, 128)
            m0 = mt[pl.ds(base + j, 1), :]
            m1 = mt[pl.ds(base + CHUNK + j, 1), :]
            a0 = a0 - m0 * xt[pl.ds(xb, CHUNK), :]
            a1 = a1 - m1 * xt[pl.ds(xb + CHUNK, CHUNK), :]
            return a0, a1

        a0, a1 = lax.fori_loop(0, i, step, (unit, unit))
        xt[pl.ds(base, CHUNK), :] = a0
        xt[pl.ds(base + CHUNK, CHUNK), :] = a1
        return c

    lax.fori_loop(0, CHUNK, solve_row, 0)

    def from_lanes(i, c):
        x = xt[pl.ds(pl.multiple_of(i * 128, 128), 128), :]
        t_ref[pl.ds(i, SOLVE_SLABS, stride=M_PITCH), :] = x.T
        return c

    lax.fori_loop(0, CHUNK, from_lanes, 0, unroll=4)
    for k in range(CHUNK, M_PITCH):
        t_ref[pl.ds(k, SOLVE_SLABS, stride=M_PITCH), :] = jnp.zeros((SOLVE_SLABS, 128), F32)


def _gdn_solve(m):
    rows = SOLVE_SLABS * M_PITCH
    return pl.pallas_call(
        _gdn_solve_kernel,
        grid=(PROMPT_TILES * SLABS_TILE // SOLVE_SLABS,),
        in_specs=[pl.BlockSpec((rows, 128), lambda i: (i, 0))],
        out_specs=pl.BlockSpec((rows, 128), lambda i: (i, 0)),
        out_shape=jax.ShapeDtypeStruct(m.shape, F32),
        scratch_shapes=[pltpu.VMEM((CHUNK * 128, SOLVE_SLABS), F32),
                        pltpu.VMEM((CHUNK * 128, SOLVE_SLABS), F32)],
        compiler_params=pltpu.CompilerParams(dimension_semantics=("arbitrary",)),
        name="gdn_solve",
    )(m)


def _gdn_scan_kernel(rp_ref, qg_ref, kt_ref, dl_ref, tm_ref, aq_ref, z0_ref, z1_ref, ys_ref,
                     ong_ref, wout_ref, yb_ref, ss_ref, obuf, s_scr):
    i = pl.program_id(0)
    pl.when(i < HALF_TILES)(functools.partial(
        _gdn_scan_tile, i % SEQ_TILES, rp_ref, qg_ref, kt_ref, dl_ref, tm_ref, aq_ref,
        (z0_ref, z1_ref), ong_ref, wout_ref, yb_ref, ss_ref, obuf, s_scr))

    @pl.when(i == HALF_TILES)
    def _():
        yb_ref[0] = jnp.zeros((TOK_TILE, D_MODEL), F32)
        _sample_rows_tile(ys_ref, yb_ref.at[1])


def _gdn_scan_tile(t, rp_ref, qg_ref, kt_ref, dl_ref, tm_ref, aq_ref, z_refs, ong_ref, wout_ref,
                   yb_ref, ss_ref, obuf, s_scr):
    @pl.when(t == 0)
    def _():
        s_scr[...] = jnp.zeros((2, N_HEADS // 2, HEAD_K, 2 * HEAD_V), F32)

    left = lax.broadcasted_iota(jnp.int32, (CHUNK, 128), 1) < CHUNK
    ong = ong_ref[...]
    zero = jnp.zeros((CHUNK, HEAD_V), BF16)

    def pair_diag(x):
        x = x.astype(BF16)
        none = jnp.zeros_like(x)
        return jnp.concatenate([jnp.where(left, x, none), jnp.where(left, none, x)], axis=0)

    for c in range(SEQ_TILE // CHUNK):
        r0 = c * CHUNK
        for half in range(2):
            for hp in range(N_HEADS // 2):
                slab = 2 * c + hp
                rows = slice(slab * 2 * CHUNK, (slab + 1) * 2 * CHUNK)
                tm = pair_diag(tm_ref[half, slab * M_PITCH:slab * M_PITCH + CHUNK, :])
                aq = pair_diag(aq_ref[half, slab * CHUNK:(slab + 1) * CHUNK, :])
                uw = jnp.dot(tm, rp_ref[half, rows, :], preferred_element_type=F32)
                s = s_scr[half, hp]
                lhs = jnp.concatenate([uw[:, HEAD_V:].astype(BF16), qg_ref[half, rows, :]], axis=0)
                wq = jnp.dot(lhs, s.astype(BF16), preferred_element_type=F32)
                ws = jnp.concatenate([wq[0:CHUNK, 0:HEAD_V], wq[CHUNK:2 * CHUNK, HEAD_V:]], axis=0)
                qs = jnp.concatenate([wq[2 * CHUNK:3 * CHUNK, 0:HEAD_V], wq[3 * CHUNK:, HEAD_V:]],
                                     axis=0)
                vnew = (uw[:, :HEAD_V] - ws).astype(BF16)
                vn_bd = jnp.concatenate(
                    [jnp.concatenate([vnew[0:CHUNK], zero], axis=1),
                     jnp.concatenate([zero, vnew[CHUNK:]], axis=1)], axis=0)
                s_scr[half, hp] = s * dl_ref[half, slab * 8:slab * 8 + 1, :] + jnp.dot(
                    kt_ref[half, rows, :], vn_bd, preferred_element_type=F32)
                o = qs + jnp.dot(aq, vnew, preferred_element_type=F32)
                for n in range(2):
                    h = 2 * hp + n
                    zh = z_refs[half][r0:r0 + CHUNK, h * HEAD_V:(h + 1) * HEAD_V]
                    obuf[half, r0:r0 + CHUNK, h * HEAD_V:(h + 1) * HEAD_V] = _gate_norm(
                        o[n * CHUNK:(n + 1) * CHUNK], ong, zh)

    for half in range(2):
        yb_ref[half] = _bdot(obuf[half], wout_ref[...])

    @pl.when(t == SEQ_TILES - 1)
    def _():
        for half in range(2):
            for h in range(N_HEADS):
                ss_ref[half, h] = s_scr[half, h // 2, :, (h % 2) * HEAD_V:(h % 2 + 1) * HEAD_V]


def _gdn_scan(rp, qg, kt, dl, tm, aq, z, yb_sample, ong, wout):
    last = HALF_TILES - 1
    halves = lambda a: a.reshape((2, a.shape[0] // 2) + a.shape[1:])
    both = lambda a: pl.BlockSpec((2, a.shape[0] // PROMPT_TILES, a.shape[1]),
                                  lambda i: (0, jnp.minimum(i, last), 0))
    ztile = lambda off: pl.BlockSpec((SEQ_TILE, D_DELTA_V),
                                     lambda i: (jnp.minimum(i, last) + off, 0))
    ops = (rp, qg, kt, dl, tm, aq)
    yb, ss = pl.pallas_call(
        _gdn_scan_kernel,
        grid=(HALF_TILES + 1,),
        in_specs=[both(a) for a in ops] + [
            ztile(0), ztile(HALF_TILES),
            pl.BlockSpec((DEC_BATCH, D_MODEL), lambda i: (0, 0)),
            pl.BlockSpec((1, 128), lambda i: (0, 0)),
            pl.BlockSpec((D_DELTA_V, D_MODEL), lambda i: (0, 0))],
        out_specs=[pl.BlockSpec((2, SEQ_TILE, D_MODEL), lambda i: (0, i, 0)),
                   pl.BlockSpec((2, None, N_HEADS, HEAD_K, HEAD_V),
                                lambda i: (0, jnp.minimum(i // SEQ_TILES, HALF_SEQS - 1), 0, 0, 0))],
        out_shape=[jax.ShapeDtypeStruct((2, (HALF_TILES + 1) * TOK_TILE, D_MODEL), F32),
                   jax.ShapeDtypeStruct((2, HALF_SEQS, N_HEADS, HEAD_K, HEAD_V), F32)],
        scratch_shapes=[pltpu.VMEM((2, SEQ_TILE, D_DELTA_V), F32),
                        pltpu.VMEM((2, N_HEADS // 2, HEAD_K, 2 * HEAD_V), F32)],
        compiler_params=pltpu.CompilerParams(
            dimension_semantics=("arbitrary",), vmem_limit_bytes=VMEM_BIG),
        name="gdn_scan",
    )(*[halves(a) for a in ops], z, z, yb_sample, ong, wout)
    return yb, ss.reshape(BATCH, N_HEADS, HEAD_K, HEAD_V)


def _gdn_sample_kernel(qkv_ref, z_ref, ab_ref, st_ref, s_ref, cw_ref, alog_ref, dtb_ref,
                       ong_ref, wout_ref, yb_ref, nst_ref, ns_ref):
    x = qkv_ref[...]
    cw = cw_ref[...]
    nbuf = QK_CONV - 1
    acc = cw[nbuf:nbuf + 1, :] * x
    for j in range(nbuf):
        acc = acc + cw[j:j + 1, :] * st_ref[:, j * CONV_DIM:(j + 1) * CONV_DIM]
    nst_ref[:, 0:(nbuf - 1) * CONV_DIM] = st_ref[:, CONV_DIM:nbuf * CONV_DIM]
    nst_ref[:, (nbuf - 1) * CONV_DIM:nbuf * CONV_DIM] = x
    qs, ks, vs = _qkv_act(acc)
    ab = ab_ref[...]
    g = -jnp.exp(alog_ref[...]) * jax.nn.softplus(ab + dtb_ref[...])
    beta = jax.nn.sigmoid(ab)
    eg = jnp.exp(g)
    z = z_ref[...]
    outs = []
    for h in range(N_HEADS):
        bh = beta[:, N_HEADS + h:N_HEADS + h + 1]
        egh = eg[:, h:h + 1]
        kb = ks[h] * bh
        w = kb * egh
        qg = qs[h] * egh
        u = vs[h] * bh
        qk = jnp.sum(qs[h] * ks[h], axis=-1, keepdims=True)
        o_rows = []
        for s in range(DEC_BLK):
            st = s_ref[s, h]
            wq = _bdot(jnp.concatenate([w[s:s + 1], qg[s:s + 1]], axis=0), st)
            vnew = u[s:s + 1] - wq[0:1]
            o_rows.append(wq[1:2] + qk[s:s + 1] * vnew)
            ns_ref[s, h] = st * egh[s:s + 1] + lax.dot_general(
                ks[h][s:s + 1].astype(BF16), vnew.astype(BF16), (((0,), (0,)), ((), ())),
                preferred_element_type=F32)
        o = jnp.concatenate(o_rows, axis=0)
        outs.append(_gate_norm(o, ong_ref[...], z[:, h * HEAD_V:(h + 1) * HEAD_V]))
    yb_ref[...] = _bdot(jnp.concatenate(outs, axis=1), wout_ref[...])


def _gdn_sample(qkv, z, ab, st, state_delta, layer, cw, alog, dtb, ong, wout):
    nbuf = QK_CONV - 1
    blk0 = N_PROMPT // DEC_BLK
    rows = lambda w: pl.BlockSpec((DEC_BLK, w), lambda i: (blk0 + i, 0))
    vec = pl.BlockSpec((1, 128), lambda i: (0, 0))
    sblk = pl.BlockSpec((DEC_BLK, N_HEADS, HEAD_K, HEAD_V), lambda i: (i, 0, 0, 0))
    return pl.pallas_call(
        _gdn_sample_kernel,
        grid=(DEC_BATCH // DEC_BLK,),
        in_specs=[rows(CONV_DIM), rows(D_DELTA_V), rows(128),
                  pl.BlockSpec((DEC_BLK, nbuf * CONV_DIM), lambda i: (i, 0)),
                  pl.BlockSpec((None, DEC_BLK, N_HEADS, HEAD_K, HEAD_V),
                               lambda i: (layer, i, 0, 0, 0)),
                  pl.BlockSpec((QK_CONV, CONV_DIM), lambda i: (0, 0)),
                  vec, vec, vec,
                  pl.BlockSpec((D_DELTA_V, D_MODEL), lambda i: (0, 0))],
        out_specs=[pl.BlockSpec((DEC_BLK, D_MODEL), lambda i: (i, 0)),
                   pl.BlockSpec((DEC_BLK, nbuf * CONV_DIM), lambda i: (i, 0)),
                   sblk],
        out_shape=[jax.ShapeDtypeStruct((DEC_BATCH, D_MODEL), F32),
                   jax.ShapeDtypeStruct((DEC_BATCH, nbuf * CONV_DIM), F32),
                   jax.ShapeDtypeStruct((DEC_BATCH, N_HEADS, HEAD_K, HEAD_V), F32)],
        compiler_params=pltpu.CompilerParams(dimension_semantics=("arbitrary",)),
        name="gdn_sample",
    )(qkv, z, ab, st, state_delta, cw, alog, dtb, ong, wout)


def _router_kernel(x_ref, g_ref, r_ref, h_ref, info_ref, cnt_ref):
    h = _rms(x_ref[...], g_ref[...])
    h_ref[...] = h
    logits = _hdot(h, r_ref[...])
    lane = lax.broadcasted_iota(jnp.int32, logits.shape, 1)
    neg = jnp.float32(-jnp.inf)
    logits = jnp.where(lane < N_EXPERTS, logits, neg)
    m1 = jnp.max(logits, axis=-1, keepdims=True)
    i1 = jnp.min(jnp.where(logits == m1, lane, 128), axis=-1, keepdims=True)
    rest = jnp.where(lane == i1, neg, logits)
    m2 = jnp.max(rest, axis=-1, keepdims=True)
    i2 = jnp.min(jnp.where(rest == m2, lane, 128), axis=-1, keepdims=True)
    e = jnp.exp(m2 - m1)
    p1 = 1.0 / (1.0 + e)
    p2 = e / (1.0 + e)
    oh1 = jnp.where(lane == i1, 1.0, 0.0).astype(F32)
    oh2 = jnp.where(lane == i2, 1.0, 0.0).astype(F32)
    tr = lax.broadcasted_iota(jnp.int32, (TOK_TILE, TOK_TILE), 0)
    tc = lax.broadcasted_iota(jnp.int32, (TOK_TILE, TOK_TILE), 1)
    cum = _bdot(jnp.where(tr >= tc, 1.0, 0.0), oh1 + oh2)
    rank = cum - (oh1 + oh2)
    r1 = jnp.sum(oh1 * rank, axis=-1, keepdims=True)
    r2 = jnp.sum(oh2 * rank, axis=-1, keepdims=True)
    info = jnp.where(lane == 0, i1.astype(F32), 0.0)
    info = jnp.where(lane == 1, i2.astype(F32), info)
    info = jnp.where(lane == 2, p1, info)
    info = jnp.where(lane == 3, p2, info)
    info = jnp.where(lane == 4, r1, info)
    info = jnp.where(lane == 5, r2, info)
    info_ref[...] = info
    cnt_ref[...] = jnp.broadcast_to(cum[TOK_TILE - 1:TOK_TILE, :], (8, 128))


def _router(x, g, r_pad):
    row = lambda w: pl.BlockSpec((TOK_TILE, w), lambda i: (i, 0))
    return pl.pallas_call(
        _router_kernel,
        grid=(N_TOK // TOK_TILE,),
        in_specs=[row(D_MODEL), pl.BlockSpec((1, D_MODEL), lambda i: (0, 0)),
                  pl.BlockSpec((D_MODEL, 128), lambda i: (0, 0))],
        out_specs=[row(D_MODEL), row(128), pl.BlockSpec((None, 8, 128), lambda i: (i, 0, 0))],
        out_shape=[jax.ShapeDtypeStruct((N_TOK, D_MODEL), F32),
                   jax.ShapeDtypeStruct((N_TOK, 128), F32),
                   jax.ShapeDtypeStruct((N_TOK // TOK_TILE, 8, 128), F32)],
        compiler_params=pltpu.CompilerParams(dimension_semantics=("arbitrary",)),
        name="moe_router",
    )(x, g, r_pad)


def _tile_rows_wait(src_ref, dst_ref, sem):
    pltpu.make_async_copy(src_ref.at[pl.ds(0, TOK_TILE), :], dst_ref, sem).wait()


def _dispatch_kernel(dest_ref, xs_in_ref, h_ref, xs_ref, sem):
    del xs_in_ref
    base = pl.program_id(0) * (2 * TOK_TILE)

    def start(r, c):
        for k in range(2):
            pltpu.make_async_copy(
                h_ref.at[pl.ds(r, 1), :],
                xs_ref.at[pl.ds(dest_ref[base + 2 * r + k], 1), :], sem).start()
        return c

    lax.fori_loop(0, TOK_TILE, start, 0, unroll=8)
    for _ in range(2):
        _tile_rows_wait(h_ref, xs_ref.at[pl.ds(0, TOK_TILE), :], sem)


def _dispatch(dest, xs0, h):
    return pl.pallas_call(
        _dispatch_kernel,
        grid_spec=pltpu.PrefetchScalarGridSpec(
            num_scalar_prefetch=1,
            grid=(N_TOK // TOK_TILE,),
            in_specs=[pl.BlockSpec(memory_space=pl.ANY),
                      pl.BlockSpec((TOK_TILE, D_MODEL), lambda i, d: (i, 0))],
            out_specs=pl.BlockSpec(memory_space=pl.ANY),
            scratch_shapes=[pltpu.SemaphoreType.DMA(())]),
        out_shape=jax.ShapeDtypeStruct((MOE_ROWS, D_MODEL), F32),
        input_output_aliases={1: 0},
        compiler_params=pltpu.CompilerParams(dimension_semantics=("arbitrary",)),
        name="moe_dispatch",
    )(dest, xs0, h)


def _experts_kernel(te_ref, tv_ref, x_ref, wg_ref, wu_ref, wd_ref, o_ref):
    i = pl.program_id(0)

    @pl.when(tv_ref[i] > 0)
    def _():
        o_ref[...] = _swiglu_rows(x_ref[...].astype(BF16), wg_ref, wu_ref, wd_ref)

    @pl.when(tv_ref[i] == 0)
    def _():
        o_ref[...] = jnp.zeros((MOE_TILE, D_MODEL), F32)


def _experts(tile_expert, tile_valid, xs, wg, wu, wd):
    row = pl.BlockSpec((MOE_TILE, D_MODEL), lambda i, te, tv: (i, 0))
    return pl.pallas_call(
        _experts_kernel,
        grid_spec=pltpu.PrefetchScalarGridSpec(
            num_scalar_prefetch=2,
            grid=(MOE_ROWS // MOE_TILE,),
            in_specs=[row,
                      pl.BlockSpec((None, D_MODEL, D_FF), lambda i, te, tv: (te[i], 0, 0)),
                      pl.BlockSpec((None, D_MODEL, D_FF), lambda i, te, tv: (te[i], 0, 0)),
                      pl.BlockSpec((None, D_FF, D_MODEL), lambda i, te, tv: (te[i], 0, 0))],
            out_specs=row),
        out_shape=jax.ShapeDtypeStruct((MOE_ROWS, D_MODEL), F32),
        compiler_params=pltpu.CompilerParams(
            dimension_semantics=("arbitrary",), vmem_limit_bytes=VMEM_BIG),
        name="moe_experts",
    )(tile_expert, tile_valid, xs, wg, wu, wd)


def _combine_kernel(dest_ref, x_ref, info_ref, g_ref, ys_ref, yp_ref, ysm_ref, ybuf, sem):
    i = pl.program_id(0)
    n = pl.num_programs(0)

    def start_tile(tile):
        slot = tile % 2
        base = tile * (2 * TOK_TILE)

        def start(r, c):
            for k in range(2):
                pltpu.make_async_copy(
                    ys_ref.at[pl.ds(dest_ref[base + 2 * r + k], 1), :],
                    ybuf.at[slot, k, pl.ds(r, 1), :], sem.at[slot]).start()
            return c

        lax.fori_loop(0, TOK_TILE, start, 0, unroll=8)

    @pl.when(i == 0)
    def _():
        start_tile(i)

    @pl.when(i + 1 < n)
    def _():
        start_tile(i + 1)

    slot = i % 2
    for k in range(2):
        _tile_rows_wait(ys_ref, ybuf.at[slot, k], sem.at[slot])
    info = info_ref[...]
    x = x_ref[...] + info[:, 2:3] * ybuf[slot, 0] + info[:, 3:4] * ybuf[slot, 1]
    y = _rms(x, g_ref[...])

    @pl.when(i < PROMPT_TILES)
    def _():
        yp_ref[...] = y

    @pl.when(i == PROMPT_TILES)
    def _():
        ysm_ref[...] = y[0:DEC_BATCH, :]


def _combine(dest, x, info, g, ys):
    last = PROMPT_TILES - 1
    return pl.pallas_call(
        _combine_kernel,
        grid_spec=pltpu.PrefetchScalarGridSpec(
            num_scalar_prefetch=1,
            grid=(N_TOK // TOK_TILE,),
            in_specs=[pl.BlockSpec((TOK_TILE, D_MODEL), lambda i, d: (i, 0)),
                      pl.BlockSpec((TOK_TILE, 128), lambda i, d: (i, 0)),
                      pl.BlockSpec((1, D_MODEL), lambda i, d: (0, 0)),
                      pl.BlockSpec(memory_space=pl.ANY)],
            out_specs=[pl.BlockSpec((TOK_TILE, D_MODEL), lambda i, d: (jnp.minimum(i, last), 0)),
                       pl.BlockSpec((DEC_BATCH, D_MODEL), lambda i, d: (0, 0))],
            scratch_shapes=[pltpu.VMEM((2, 2, TOK_TILE, D_MODEL), F32),
                            pltpu.SemaphoreType.DMA((2,))]),
        out_shape=[jax.ShapeDtypeStruct((N_PROMPT, D_MODEL), F32),
                   jax.ShapeDtypeStruct((DEC_BATCH, D_MODEL), F32)],
        compiler_params=pltpu.CompilerParams(dimension_semantics=("arbitrary",)),
        name="moe_combine",
    )(dest, x, info, g, ys)


def _moe_layer(x, g2, router_w, wg, wu, wd, final_g):
    r_pad = jnp.zeros((D_MODEL, 128), F32).at[:, :N_EXPERTS].set(router_w)
    h, info, cnt = _router(x, g2, r_pad)
    n_tiles = N_TOK // TOK_TILE
    counts = cnt[:, 0, :N_EXPERTS].astype(jnp.int32)
    total = jnp.sum(counts, axis=0)
    padded = ((total + MOE_TILE - 1) // MOE_TILE) * MOE_TILE
    ends = jnp.cumsum(padded)
    starts = ends - padded
    base = starts[None, :] + jnp.cumsum(counts, axis=0) - counts
    choice = info[:, 0:2].astype(jnp.int32).reshape(n_tiles, TOK_TILE, 2)
    rank = info[:, 4:6].astype(jnp.int32).reshape(n_tiles, TOK_TILE, 2)
    onehot = choice[..., None] == jnp.arange(N_EXPERTS, dtype=jnp.int32)
    dest = (jnp.sum(jnp.where(onehot, base[:, None, None, :], 0), axis=-1) + rank).reshape(-1)
    tile_start = jnp.arange(MOE_ROWS // MOE_TILE, dtype=jnp.int32) * MOE_TILE
    tile_expert = jnp.minimum(
        jnp.sum((tile_start[:, None] >= ends[None, :]).astype(jnp.int32), axis=1), N_EXPERTS - 1)
    tile_valid = (tile_start < ends[-1]).astype(jnp.int32)
    last_used = jnp.max(jnp.where(tile_valid > 0, tile_expert, 0))
    tile_expert = jnp.where(tile_valid > 0, tile_expert, last_used).astype(jnp.int32)

    xs = _dispatch(dest, jnp.zeros((MOE_ROWS, D_MODEL), F32), h)
    ys = _experts(tile_expert, tile_valid, xs, wg, wu, wd)
    return _combine(dest, x, info, final_g, ys)


def _rearranged_w_in(w_in):
    off_qkv = 2 * D_CONV
    off_z = off_qkv + CONV_DIM
    off_a = off_z + D_DELTA_V
    off_ga = off_a + 2 * N_HEADS
    off_gb = off_ga + D_MODEL
    w_in = w_in.astype(BF16)
    return jnp.concatenate(
        [w_in[:, :off_a], w_in[:, off_ga:off_gb], w_in[:, off_gb:], w_in[:, off_a:off_ga],
         jnp.zeros((D_MODEL, 128 - 2 * N_HEADS), BF16)], axis=1)


def _lane_vec(v, offset=0):
    return jnp.zeros((1, 128), F32).at[0, offset:offset + v.shape[0]].set(v)


def kernel(x_prompt, x_sample, state_delta, state_qkv_conv, state_dwconv, norm1_g, w_in, qkv_conv_w, a_log, dt_bias, o_norm_g, w_delta_out, dw_w, dw_b, ln_g, ln_b, w_conv_out, w_o, norm2_g, ffn_w_gate, ffn_w_up, ffn_w_down, router_w, exp_w_gate, exp_w_up, exp_w_down, final_norm_g):
    x = (x_prompt.reshape(N_PROMPT, D_MODEL), x_sample.reshape(DEC_BATCH, D_MODEL))
    s_p, q_p, d_p, s_s, q_s, d_s = [], [], [], [], [], []
    for l in range(DEPTH):
        glu, qkv, z, ga, gb, ab = _norm_proj(x, norm1_g[l][None, :], _rearranged_w_in(w_in[l]))
        dwb, lng, lnb = dw_b[l][None, :], ln_g[l][None, :], ln_b[l][None, :]
        wco = w_conv_out[l].astype(BF16)
        ya_s, dst_s = _conv_sample(glu, state_dwconv[l].reshape(DEC_BATCH, -1), dw_w[l], dwb, lng, lnb, wco)
        ya, dst_p = _conv_prompt(glu, ya_s, dw_w[l], dwb, lng, lnb, wco)
        alog, dtb, ong = _lane_vec(a_log[l]), _lane_vec(dt_bias[l]), o_norm_g[l][None, :]
        wdo = w_delta_out[l].astype(BF16)
        yb_s, qst_s, sst_s = _gdn_sample(qkv, z, ab, state_qkv_conv[l].reshape(DEC_BATCH, -1),
                                         state_delta, l, qkv_conv_w[l], alog, dtb, ong, wdo)
        rp, qg, kt, dl, m, aq, qst_p = _gdn_pre(qkv, ab, qkv_conv_w[l], alog, dtb)
        yb, sst_p = _gdn_scan(rp, qg, kt, dl, _gdn_solve(m), aq, z, yb_s, ong, wdo)
        x = _merge(x, ya, yb, ga, gb, w_o[l].astype(BF16))
        i = l // 2
        if l % 2 == 0:
            x = _ffn(x, norm2_g[l][None, :], ffn_w_gate[i].astype(BF16), ffn_w_up[i].astype(BF16),
                     ffn_w_down[i].astype(BF16))
        else:
            y_prompt, y_sample = _moe_layer(
                x, norm2_g[l][None, :], router_w[i], exp_w_gate[i].astype(BF16),
                exp_w_up[i].astype(BF16), exp_w_down[i].astype(BF16), final_norm_g[None, :])
        s_p.append(sst_p)
        q_p.append(qst_p)
        d_p.append(dst_p)
        s_s.append(sst_s)
        q_s.append(qst_s.reshape(DEC_BATCH, QK_CONV - 1, CONV_DIM))
        d_s.append(dst_s.reshape(DEC_BATCH, DW_WIDTH - 1, D_CONV))
    return (y_prompt.reshape(BATCH, SEQ, D_MODEL), y_sample.reshape(DEC_BATCH, 1, D_MODEL), jnp.stack(s_p), jnp.stack(q_p), jnp.stack(d_p),
            jnp.stack(s_s), jnp.stack(q_s), jnp.stack(d_s))
```

```python
import functools

import jax
import jax.numpy as jnp
from jax import lax
from jax.experimental import pallas as pl
from jax.experimental.pallas import tpu as pltpu

F32 = jnp.float32
BF16 = jnp.bfloat16
HIGHEST = lax.Precision.HIGHEST

D_MODEL = 1024
BATCH = 8
SEQ = 2048
DEPTH = 2
DEC_BATCH = 128
D_CONV = 512
DW_WIDTH = 31
N_HEADS = 4
HEAD_K = 128
HEAD_V = 128
QK_CONV = 4
D_DELTA_K = N_HEADS * HEAD_K
D_DELTA_V = N_HEADS * HEAD_V
CONV_DIM = 2 * D_DELTA_K + D_DELTA_V
D_FF = 2816
N_EXPERTS = 8
EPS = 1e-6

N_PROMPT = BATCH * SEQ
TOK_TILE = 512
SEQ_TILE = TOK_TILE
N_TOK = N_PROMPT + TOK_TILE
SEQ_TILES = SEQ // SEQ_TILE
PROMPT_TILES = N_PROMPT // TOK_TILE
HALF_TILES = PROMPT_TILES // 2
HALF_SEQS = BATCH // 2
CHUNK = 64
M_PITCH = 68
SLABS_TILE = 2 * (SEQ_TILE // CHUNK)
M_ROWS_TILE = SLABS_TILE * M_PITCH
SOLVE_SLABS = 128
ROW_BLK = 64
FF_HALF = D_FF // 2
MOE_TILE = 256
MOE_ROWS = ((2 * N_TOK + N_EXPERTS * (MOE_TILE - 1)) // MOE_TILE + 1) * MOE_TILE
DEC_BLK = 8

C_GLU, C_QKV, C_Z, C_GA, C_GB, C_AB = 0, 1024, 2560, 3072, 4096, 5120
N_PROJ = 5248

VMEM_BIG = 56 * 1024 * 1024


def _bdot(a, b):
    return jnp.dot(a.astype(BF16), b.astype(BF16), preferred_element_type=F32)


def _hdot(a, b):
    return jnp.dot(a, b, preferred_element_type=F32, precision=HIGHEST)


def _rms(x, g):
    return x * lax.rsqrt(jnp.mean(x * x, axis=-1, keepdims=True) + EPS) * g


def _silu(x):
    return x * jax.nn.sigmoid(x)


def _resident(shape):
    return pl.BlockSpec(shape, lambda *_: (0,) * len(shape), pipeline_mode=pl.Buffered(1))


def _stream_x_specs(x):
    if isinstance(x, tuple):
        last = PROMPT_TILES - 1
        return ([pl.BlockSpec((TOK_TILE, D_MODEL), lambda i: (jnp.minimum(i, last), 0)),
                 pl.BlockSpec((DEC_BATCH, D_MODEL), lambda i: (0, 0))], list(x))
    return [pl.BlockSpec((TOK_TILE, D_MODEL), lambda i: (i, 0))], [x]


def _stream_x_tile(x_refs, xbuf):
    if len(x_refs) == 1:
        return x_refs[0][...]
    xp_ref, xs_ref = x_refs
    i = pl.program_id(0)

    @pl.when(i < PROMPT_TILES)
    def _():
        xbuf[...] = xp_ref[...]

    @pl.when(i == PROMPT_TILES)
    def _():
        xbuf[0:DEC_BATCH, :] = xs_ref[...]
        xbuf[DEC_BATCH:, :] = jnp.zeros((TOK_TILE - DEC_BATCH, D_MODEL), F32)

    return xbuf[...]


def _norm_proj_kernel(n_x, *refs):
    x_refs = refs[:n_x]
    g_ref, w_ref, glu_ref, qkv_ref, z_ref, ga_ref, gb_ref, ab_ref, xbuf = refs[n_x:]
    h = _rms(_stream_x_tile(x_refs, xbuf), g_ref[...]).astype(BF16)

    def proj(lo, hi):
        return jnp.dot(h, w_ref[:, lo:hi], preferred_element_type=F32)

    glu_ref[...] = proj(C_GLU, C_QKV)
    qkv_ref[...] = proj(C_QKV, C_Z)
    z_ref[...] = proj(C_Z, C_GA)
    ga_ref[...] = proj(C_GA, C_GB)
    gb_ref[...] = proj(C_GB, C_AB)
    ab_ref[...] = proj(C_AB, N_PROJ)


def _norm_proj(x, g, w_all):
    row = lambda w: pl.BlockSpec((TOK_TILE, w), lambda i: (i, 0))
    widths = (1024, CONV_DIM, D_DELTA_V, D_MODEL, D_MODEL, 128)
    x_specs, x_ops = _stream_x_specs(x)
    return pl.pallas_call(
        functools.partial(_norm_proj_kernel, len(x_ops)),
        grid=(N_TOK // TOK_TILE,),
        in_specs=x_specs + [pl.BlockSpec((1, D_MODEL), lambda i: (0, 0)),
                            _resident((D_MODEL, N_PROJ))],
        out_specs=[row(w) for w in widths],
        out_shape=[jax.ShapeDtypeStruct((N_TOK, w), F32) for w in widths],
        scratch_shapes=[pltpu.VMEM((TOK_TILE, D_MODEL), F32)],
        compiler_params=pltpu.CompilerParams(
            dimension_semantics=("arbitrary",), vmem_limit_bytes=VMEM_BIG),
        name="norm_proj",
    )(*x_ops, g, w_all)


def _gated_branch(x, gate_ref, y, wo_ref):
    return x + _bdot(jax.nn.sigmoid(gate_ref[...]) * y, wo_ref[...])


def _halves_tile_spec():
    half = lambda i: i // HALF_TILES - i // (2 * HALF_TILES)
    return pl.BlockSpec((None, TOK_TILE, D_MODEL),
                        lambda i, *_: (half(i), i - HALF_TILES * half(i), 0))


def _swiglu_rows(h, wg_ref, wu_ref, wd_ref):
    acc = None
    for lo in (0, FF_HALF):
        a = jnp.dot(h, wg_ref[:, lo:lo + FF_HALF], preferred_element_type=F32)
        b = jnp.dot(h, wu_ref[:, lo:lo + FF_HALF], preferred_element_type=F32)
        part = jnp.dot((_silu(a) * b).astype(BF16), wd_ref[lo:lo + FF_HALF, :],
                       preferred_element_type=F32)
        acc = part if acc is None else acc + part
    return acc


def _ffn_kernel(x_ref, yb_ref, gb_ref, wo_ref, g_ref, wg_ref, wu_ref, wd_ref, o_ref):
    x = _gated_branch(x_ref[...], gb_ref, yb_ref[...], wo_ref)
    h = _rms(x, g_ref[...]).astype(BF16)
    o_ref[...] = x + _swiglu_rows(h, wg_ref, wu_ref, wd_ref)


def _ffn(x, yb, gb, wo, g, wg, wu, wd):
    row = pl.BlockSpec((TOK_TILE, D_MODEL), lambda i: (i, 0))
    return pl.pallas_call(
        _ffn_kernel,
        grid=(N_TOK // TOK_TILE,),
        in_specs=[row, _halves_tile_spec(), row, _resident((D_MODEL, D_MODEL)),
                  pl.BlockSpec((1, D_MODEL), lambda i: (0, 0)),
                  _resident((D_MODEL, D_FF)), _resident((D_MODEL, D_FF)),
                  _resident((D_FF, D_MODEL))],
        out_specs=row,
        out_shape=jax.ShapeDtypeStruct((N_TOK, D_MODEL), F32),
        compiler_params=pltpu.CompilerParams(
            dimension_semantics=("arbitrary",), vmem_limit_bytes=VMEM_BIG),
        name="ffn_dense",
    )(x, yb, gb, wo, g, wg, wu, wd)


def _conv_tail(c, dwb, lng, lnb):
    c = c + dwb
    mu = jnp.mean(c, axis=-1, keepdims=True)
    var = jnp.mean(jnp.square(c - mu), axis=-1, keepdims=True)
    c = (c - mu) * lax.rsqrt(var + EPS) * lng + lnb
    return _silu(c)


def _sample_rows_tile(ys_ref, y_ref):
    y_ref[0:DEC_BATCH, :] = ys_ref[...]
    y_ref[DEC_BATCH:, :] = jnp.zeros((TOK_TILE - DEC_BATCH, D_MODEL), F32)


def _conv_prompt_kernel(n_x, *refs):
    x_refs = refs[:n_x]
    (glu_ref, ga_ref, ys_ref, dww_ref, dwb_ref, lng_ref, lnb_ref, wout_ref, wo_ref,
     o_ref, st_ref, ubuf, ushift, cbuf, xbuf) = refs[n_x:]
    i = pl.program_id(0)
    if n_x == 2:
        _stream_x_tile(x_refs, xbuf)
        x_ref = xbuf
    else:
        x_ref = x_refs[0]

    @pl.when(i < PROMPT_TILES)
    def _():
        _conv_prompt_tile(i % SEQ_TILES, glu_ref, dww_ref, dwb_ref, lng_ref, lnb_ref,
                          st_ref, ubuf, ushift, cbuf)
        ya = jnp.dot(cbuf[...], wout_ref[...], preferred_element_type=F32)
        o_ref[...] = _gated_branch(x_ref[...], ga_ref, ya, wo_ref)

    @pl.when(i == PROMPT_TILES)
    def _():
        ya = jnp.concatenate(
            [ys_ref[...], jnp.zeros((TOK_TILE - DEC_BATCH, D_MODEL), F32)], axis=0)
        o_ref[...] = _gated_branch(x_ref[...], ga_ref, ya, wo_ref)


def _conv_prompt_tile(t, glu_ref, dww_ref, dwb_ref, lng_ref, lnb_ref, st_ref, ubuf, ushift, cbuf):
    halo = 32

    @pl.when(t == 0)
    def _():
        ubuf[0:halo, :] = jnp.zeros((halo, D_CONV), F32)

    @pl.when(t > 0)
    def _():
        ubuf[0:halo, :] = ubuf[SEQ_TILE:SEQ_TILE + halo, :]

    glu = glu_ref[...]
    ubuf[halo:halo + SEQ_TILE, :] = glu[:, :D_CONV] * jax.nn.sigmoid(glu[:, D_CONV:])
    for r in range(1, 8):
        ushift[r - 1, 0:SEQ_TILE + halo - 8, :] = ubuf[r:r + SEQ_TILE + halo - 8, :]
    w = dww_ref[...]
    first = halo - (DW_WIDTH - 1)
    for rb in range(SEQ_TILE // ROW_BLK):
        r0 = rb * ROW_BLK
        acc = jnp.zeros((ROW_BLK, D_CONV), F32)
        for j in range(DW_WIDTH):
            a, r = divmod(first + j, 8)
            if r == 0:
                win = ubuf[r0 + 8 * a:r0 + 8 * a + ROW_BLK, :]
            else:
                win = ushift[r - 1, r0 + 8 * a:r0 + 8 * a + ROW_BLK, :]
            acc = acc + w[j:j + 1, :] * win
        c = _conv_tail(acc, dwb_ref[...], lng_ref[...], lnb_ref[...])
        cbuf[r0:r0 + ROW_BLK, :] = c.astype(BF16)

    @pl.when(t == SEQ_TILES - 1)
    def _():
        st_ref[...] = ubuf[halo + SEQ_TILE - (DW_WIDTH - 1):halo + SEQ_TILE, :]


def _seq_of_tile(i):
    return jnp.minimum(i // SEQ_TILES, BATCH - 1)


def _conv_prompt(x, glu, ga, ya_sample, dww, dwb, lng, lnb, wout, wo):
    vec = pl.BlockSpec((1, D_CONV), lambda i: (0, 0))
    row = pl.BlockSpec((SEQ_TILE, D_MODEL), lambda i: (i, 0))
    x_specs, x_ops = _stream_x_specs(x)
    return pl.pallas_call(
        functools.partial(_conv_prompt_kernel, len(x_ops)),
        grid=(PROMPT_TILES + 1,),
        in_specs=x_specs + [
            pl.BlockSpec((SEQ_TILE, 2 * D_CONV), lambda i: (i, 0)), row,
            pl.BlockSpec((DEC_BATCH, D_MODEL), lambda i: (0, 0)),
            pl.BlockSpec((DW_WIDTH, D_CONV), lambda i: (0, 0)),
            vec, vec, vec,
            pl.BlockSpec((D_CONV, D_MODEL), lambda i: (0, 0)),
            pl.BlockSpec((D_MODEL, D_MODEL), lambda i: (0, 0))],
        out_specs=[row,
                   pl.BlockSpec((None, DW_WIDTH - 1, D_CONV), lambda i: (_seq_of_tile(i), 0, 0))],
        out_shape=[jax.ShapeDtypeStruct((N_TOK, D_MODEL), F32),
                   jax.ShapeDtypeStruct((BATCH, DW_WIDTH - 1, D_CONV), F32)],
        scratch_shapes=[pltpu.VMEM((SEQ_TILE + 32, D_CONV), F32),
                        pltpu.VMEM((7, SEQ_TILE + 24, D_CONV), F32),
                        pltpu.VMEM((SEQ_TILE, D_CONV), BF16),
                        pltpu.VMEM((TOK_TILE, D_MODEL), F32)],
        compiler_params=pltpu.CompilerParams(
            dimension_semantics=("arbitrary",), vmem_limit_bytes=VMEM_BIG),
        name="conv_prompt",
    )(*x_ops, glu, ga, ya_sample, dww, dwb, lng, lnb, wout, wo)


def _conv_sample_kernel(glu_ref, st_ref, dww_ref, dwb_ref, lng_ref, lnb_ref, wout_ref,
                        ya_ref, nst_ref):
    glu = glu_ref[...]
    u = glu[:, :D_CONV] * jax.nn.sigmoid(glu[:, D_CONV:])
    w = dww_ref[...]
    nbuf = DW_WIDTH - 1
    acc = w[nbuf:nbuf + 1, :] * u
    for j in range(nbuf):
        acc = acc + w[j:j + 1, :] * st_ref[:, j * D_CONV:(j + 1) * D_CONV]
    c = _conv_tail(acc, dwb_ref[...], lng_ref[...], lnb_ref[...])
    ya_ref[...] = _bdot(c, wout_ref[...])
    nst_ref[:, 0:(nbuf - 1) * D_CONV] = st_ref[:, D_CONV:nbuf * D_CONV]
    nst_ref[:, (nbuf - 1) * D_CONV:nbuf * D_CONV] = u


def _conv_sample(glu, st, dww, dwb, lng, lnb, wout):
    nbuf = DW_WIDTH - 1
    blk = N_PROMPT // DEC_BATCH
    vec = pl.BlockSpec((1, D_CONV), lambda i: (0, 0))
    return pl.pallas_call(
        _conv_sample_kernel,
        grid=(1,),
        in_specs=[pl.BlockSpec((DEC_BATCH, 2 * D_CONV), lambda i: (blk, 0)),
                  pl.BlockSpec((DEC_BATCH, nbuf * D_CONV), lambda i: (0, 0)),
                  pl.BlockSpec((DW_WIDTH, D_CONV), lambda i: (0, 0)),
                  vec, vec, vec,
                  pl.BlockSpec((D_CONV, D_MODEL), lambda i: (0, 0))],
        out_specs=[pl.BlockSpec((DEC_BATCH, D_MODEL), lambda i: (0, 0)),
                   pl.BlockSpec((DEC_BATCH, nbuf * D_CONV), lambda i: (0, 0))],
        out_shape=[jax.ShapeDtypeStruct((DEC_BATCH, D_MODEL), F32),
                   jax.ShapeDtypeStruct((DEC_BATCH, nbuf * D_CONV), F32)],
        compiler_params=pltpu.CompilerParams(dimension_semantics=("arbitrary",)),
        name="conv_sample",
    )(glu, st, dww, dwb, lng, lnb, wout)


def _qkv_act(c):
    c = _silu(c)
    qs, ks, vs = [], [], []
    for h in range(N_HEADS):
        q = c[:, h * HEAD_K:(h + 1) * HEAD_K]
        k = c[:, D_DELTA_K + h * HEAD_K:D_DELTA_K + (h + 1) * HEAD_K]
        qs.append(q * lax.rsqrt(jnp.sum(q * q, axis=-1, keepdims=True) + EPS) * (HEAD_K ** -0.5))
        ks.append(k * lax.rsqrt(jnp.sum(k * k, axis=-1, keepdims=True) + EPS))
        vs.append(c[:, 2 * D_DELTA_K + h * HEAD_V:2 * D_DELTA_K + (h + 1) * HEAD_V])
    return qs, ks, vs


def _gate_norm(o, ong, z):
    o = o * lax.rsqrt(jnp.mean(o * o, axis=-1, keepdims=True) + EPS) * ong
    return o * _silu(z)


def _gdn_pre_kernel(qkv_ref, ab_ref, cw_ref, alog_ref, dtb_ref,
                    rp_ref, qg_ref, kt_ref, dl_ref, m_ref, aq_ref, sq_ref, xbuf):
    t = pl.program_id(0) % SEQ_TILES
    halo = 8

    @pl.when(t == 0)
    def _():
        xbuf[0:halo, :] = jnp.zeros((halo, CONV_DIM), F32)

    @pl.when(t > 0)
    def _():
        xbuf[0:halo, :] = xbuf[SEQ_TILE:SEQ_TILE + halo, :]

    xbuf[halo:halo + SEQ_TILE, :] = qkv_ref[...]

    @pl.when(t == SEQ_TILES - 1)
    def _():
        sq_ref[...] = xbuf[halo + SEQ_TILE - (QK_CONV - 1):halo + SEQ_TILE, :]

    ab = ab_ref[...]
    lane = lax.broadcasted_iota(jnp.int32, (SEQ_TILE, 128), 1)
    gb = jnp.where(lane < N_HEADS,
                   -jnp.exp(alog_ref[...]) * jax.nn.softplus(ab + dtb_ref[...]),
                   jax.nn.sigmoid(ab))
    cw = cw_ref[...]
    first = halo - (QK_CONV - 1)
    ri = lax.broadcasted_iota(jnp.int32, (CHUNK, 128), 0)
    li = lax.broadcasted_iota(jnp.int32, (CHUNK, 128), 1)
    left = li < CHUNK
    jj = jnp.where(left, li, li - CHUNK)
    strict = ri > jj
    tril = ri >= jj
    r64 = lax.broadcasted_iota(jnp.int32, (CHUNK, CHUNK), 0)
    c64 = lax.broadcasted_iota(jnp.int32, (CHUNK, CHUNK), 1)
    csum = jnp.where(r64 >= c64, 1.0, 0.0).astype(F32)
    for c in range(SEQ_TILE // CHUNK):
        r0 = c * CHUNK
        acc = cw[0:1, :] * xbuf[r0 + first:r0 + first + CHUNK, :]
        for j in range(1, QK_CONV):
            acc = acc + cw[j:j + 1, :] * xbuf[r0 + first + j:r0 + first + j + CHUNK, :]
        qs, ks, vs = _qkv_act(acc)
        gbc = gb[r0:r0 + CHUNK, :]
        gc = _hdot(csum, gbc)
        gct = gc.T
        eg = jnp.exp(gc)
        glast = gc[CHUNK - 1:CHUNK, :]
        kdec = jnp.exp(glast - gc)
        dl = jnp.exp(glast)
        for hp in range(N_HEADS // 2):
            h0, h1 = 2 * hp, 2 * hp + 1
            slab = 2 * c + hp
            kb, rhs, qg, kd = [], [], [], []
            for h in (h0, h1):
                beta = gbc[:, N_HEADS + h:N_HEADS + h + 1]
                egh = eg[:, h:h + 1]
                kb.append(ks[h] * beta)
                rhs.append(jnp.concatenate([vs[h] * beta, kb[-1] * egh], axis=1))
                qg.append(qs[h] * egh)
                kd.append(ks[h] * kdec[:, h:h + 1])
            rows = slice(slab * 2 * CHUNK, (slab + 1) * 2 * CHUNK)
            rp_ref[rows, :] = jnp.concatenate(rhs, axis=0).astype(BF16)
            qg_ref[rows, :] = jnp.concatenate(qg, axis=0).astype(BF16)
            kt_ref[rows, :] = jnp.concatenate(kd, axis=0).T.astype(BF16)
            dl_ref[slab * 8:(slab + 1) * 8, :] = jnp.concatenate(
                [jnp.broadcast_to(dl[:, h0:h0 + 1], (8, HEAD_V)),
                 jnp.broadcast_to(dl[:, h1:h1 + 1], (8, HEAD_V))], axis=1)
            lhs = jnp.concatenate(kb + [qs[h0], qs[h1]], axis=0)
            keys = jnp.concatenate([ks[h0], ks[h1]], axis=0)
            prod = lax.dot_general(lhs.astype(BF16), keys.astype(BF16), (((1,), (1,)), ((), ())),
                                   preferred_element_type=F32)
            kk = jnp.where(left, prod[0:CHUNK], prod[CHUNK:2 * CHUNK])
            qk = jnp.where(left, prod[2 * CHUNK:3 * CHUNK], prod[3 * CHUNK:4 * CHUNK])
            gcol = jnp.where(left, gc[:, h0:h0 + 1], gc[:, h1:h1 + 1])
            grow = jnp.concatenate([gct[h0:h0 + 1, :], gct[h1:h1 + 1, :]], axis=1)
            diff = gcol - grow
            m_ref[slab * M_PITCH:slab * M_PITCH + CHUNK, :] = jnp.where(
                strict, kk * jnp.exp(jnp.where(strict, diff, 0.0)), 0.0)
            m_ref[slab * M_PITCH + CHUNK:(slab + 1) * M_PITCH, :] = jnp.zeros(
                (M_PITCH - CHUNK, 128), F32)
            aq_ref[slab * CHUNK:(slab + 1) * CHUNK, :] = jnp.where(
                tril, qk * jnp.exp(jnp.where(tril, diff, 0.0)), 0.0).astype(BF16)


def _gdn_pre(qkv, ab, cw, alog, dtb):
    tile = lambda w: pl.BlockSpec((SEQ_TILE, w), lambda i: (i, 0))
    vec = pl.BlockSpec((1, 128), lambda i: (0, 0))
    slab_rows = SLABS_TILE * 2 * CHUNK
    per_tile = lambda rows, w: pl.BlockSpec((rows, w), lambda i: (i, 0))
    return pl.pallas_call(
        _gdn_pre_kernel,
        grid=(PROMPT_TILES,),
        in_specs=[tile(CONV_DIM), tile(128),
                  pl.BlockSpec((QK_CONV, CONV_DIM), lambda i: (0, 0)), vec, vec],
        out_specs=[per_tile(slab_rows, 2 * HEAD_V), per_tile(slab_rows, HEAD_K),
                   per_tile(slab_rows, 128), per_tile(SLABS_TILE * 8, 2 * HEAD_V),
                   per_tile(M_ROWS_TILE, 128), per_tile(SLABS_TILE * CHUNK, 128),
                   pl.BlockSpec((None, QK_CONV - 1, CONV_DIM), lambda i: (i // SEQ_TILES, 0, 0))],
        out_shape=[jax.ShapeDtypeStruct((PROMPT_TILES * slab_rows, 2 * HEAD_V), BF16),
                   jax.ShapeDtypeStruct((PROMPT_TILES * slab_rows, HEAD_K), BF16),
                   jax.ShapeDtypeStruct((PROMPT_TILES * slab_rows, 128), BF16),
                   jax.ShapeDtypeStruct((PROMPT_TILES * SLABS_TILE * 8, 2 * HEAD_V), F32),
                   jax.ShapeDtypeStruct((PROMPT_TILES * M_ROWS_TILE, 128), F32),
                   jax.ShapeDtypeStruct((PROMPT_TILES * SLABS_TILE * CHUNK, 128), BF16),
                   jax.ShapeDtypeStruct((BATCH, QK_CONV - 1, CONV_DIM), F32)],
        scratch_shapes=[pltpu.VMEM((SEQ_TILE + 8, CONV_DIM), F32)],
        compiler_params=pltpu.CompilerParams(dimension_semantics=("arbitrary",)),
        name="gdn_pre",
    )(qkv, ab, cw, alog, dtb)


def _gdn_solve_kernel(m_ref, t_ref, mt, xt):
    def to_lanes(i, c):
        g = m_ref[pl.ds(i, SOLVE_SLABS, stride=M_PITCH), :]
        mt[pl.ds(pl.multiple_of(i * 128, 128), 128), :] = g.T
        return c

    lax.fori_loop(0, CHUNK, to_lanes, 0, unroll=4)

    row = lax.broadcasted_iota(jnp.int32, (CHUNK, SOLVE_SLABS), 0)

    def solve_row(i, c):
        base = pl.multiple_of(i * 128, 128)
        unit = jnp.where(row == i, 1.0, 0.0).astype(F32)

        def step(j, acc):
            a0, a1 = acc
            xb = pl.multiple_of(j * 128, 128)
            m0 = mt[pl.ds(base + j, 1), :]
            m1 = mt[pl.ds(base + CHUNK + j, 1), :]
            a0 = a0 - m0 * xt[pl.ds(xb, CHUNK), :]
            a1 = a1 - m1 * xt[pl.ds(xb + CHUNK, CHUNK), :]
            return a0, a1

        a0, a1 = lax.fori_loop(0, i, step, (unit, unit))
        xt[pl.ds(base, CHUNK), :] = a0
        xt[pl.ds(base + CHUNK, CHUNK), :] = a1
        return c

    lax.fori_loop(0, CHUNK, solve_row, 0)

    def from_lanes(i, c):
        x = xt[pl.ds(pl.multiple_of(i * 128, 128), 128), :]
        t_ref[pl.ds(i, SOLVE_SLABS, stride=M_PITCH), :] = x.T
        return c

    lax.fori_loop(0, CHUNK, from_lanes, 0, unroll=4)
    for k in range(CHUNK, M_PITCH):
        t_ref[pl.ds(k, SOLVE_SLABS, stride=M_PITCH), :] = jnp.zeros((SOLVE_SLABS, 128), F32)


def _gdn_solve(m):
    rows = SOLVE_SLABS * M_PITCH
    return pl.pallas_call(
        _gdn_solve_kernel,
        grid=(PROMPT_TILES * SLABS_TILE // SOLVE_SLABS,),
        in_specs=[pl.BlockSpec((rows, 128), lambda i: (i, 0))],
        out_specs=pl.BlockSpec((rows, 128), lambda i: (i, 0)),
        out_shape=jax.ShapeDtypeStruct(m.shape, F32),
        scratch_shapes=[pltpu.VMEM((CHUNK * 128, SOLVE_SLABS), F32),
                        pltpu.VMEM((CHUNK * 128, SOLVE_SLABS), F32)],
        compiler_params=pltpu.CompilerParams(dimension_semantics=("arbitrary",)),
        name="gdn_solve",
    )(m)


def _gdn_scan_kernel(rp_ref, qg_ref, kt_ref, dl_ref, tm_ref, aq_ref, z0_ref, z1_ref, ys_ref,
                     ong_ref, wout_ref, yb_ref, ss_ref, obuf, s_scr):
    i = pl.program_id(0)
    pl.when(i < HALF_TILES)(functools.partial(
        _gdn_scan_tile, i % SEQ_TILES, rp_ref, qg_ref, kt_ref, dl_ref, tm_ref, aq_ref,
        (z0_ref, z1_ref), ong_ref, wout_ref, yb_ref, ss_ref, obuf, s_scr))

    @pl.when(i == HALF_TILES)
    def _():
        yb_ref[0] = jnp.zeros((TOK_TILE, D_MODEL), F32)
        _sample_rows_tile(ys_ref, yb_ref.at[1])


def _gdn_scan_tile(t, rp_ref, qg_ref, kt_ref, dl_ref, tm_ref, aq_ref, z_refs, ong_ref, wout_ref,
                   yb_ref, ss_ref, obuf, s_scr):
    @pl.when(t == 0)
    def _():
        s_scr[...] = jnp.zeros((2, N_HEADS // 2, HEAD_K, 2 * HEAD_V), F32)

    left = lax.broadcasted_iota(jnp.int32, (CHUNK, 128), 1) < CHUNK
    ong = ong_ref[...]
    zero = jnp.zeros((CHUNK, HEAD_V), BF16)

    def pair_diag(x):
        x = x.astype(BF16)
        none = jnp.zeros_like(x)
        return jnp.concatenate([jnp.where(left, x, none), jnp.where(left, none, x)], axis=0)

    for c in range(SEQ_TILE // CHUNK):
        r0 = c * CHUNK
        for half in range(2):
            for hp in range(N_HEADS // 2):
                slab = 2 * c + hp
                rows = slice(slab * 2 * CHUNK, (slab + 1) * 2 * CHUNK)
                tm = pair_diag(tm_ref[half, slab * M_PITCH:slab * M_PITCH + CHUNK, :])
                aq = pair_diag(aq_ref[half, slab * CHUNK:(slab + 1) * CHUNK, :])
                uw = jnp.dot(tm, rp_ref[half, rows, :], preferred_element_type=F32)
                s = s_scr[half, hp]
                lhs = jnp.concatenate([uw[:, HEAD_V:].astype(BF16), qg_ref[half, rows, :]], axis=0)
                wq = jnp.dot(lhs, s.astype(BF16), preferred_element_type=F32)
                ws = jnp.concatenate([wq[0:CHUNK, 0:HEAD_V], wq[CHUNK:2 * CHUNK, HEAD_V:]], axis=0)
                qs = jnp.concatenate([wq[2 * CHUNK:3 * CHUNK, 0:HEAD_V], wq[3 * CHUNK:, HEAD_V:]],
                                     axis=0)
                vnew = (uw[:, :HEAD_V] - ws).astype(BF16)
                vn_bd = jnp.concatenate(
                    [jnp.concatenate([vnew[0:CHUNK], zero], axis=1),
                     jnp.concatenate([zero, vnew[CHUNK:]], axis=1)], axis=0)
                s_scr[half, hp] = s * dl_ref[half, slab * 8:slab * 8 + 1, :] + jnp.dot(
                    kt_ref[half, rows, :], vn_bd, preferred_element_type=F32)
                o = qs + jnp.dot(aq, vnew, preferred_element_type=F32)
                for n in range(2):
                    h = 2 * hp + n
                    zh = z_refs[half][r0:r0 + CHUNK, h * HEAD_V:(h + 1) * HEAD_V]
                    obuf[half, r0:r0 + CHUNK, h * HEAD_V:(h + 1) * HEAD_V] = _gate_norm(
                        o[n * CHUNK:(n + 1) * CHUNK], ong, zh)

    for half in range(2):
        yb_ref[half] = _bdot(obuf[half], wout_ref[...])

    @pl.when(t == SEQ_TILES - 1)
    def _():
        for half in range(2):
            for h in range(N_HEADS):
                ss_ref[half, h] = s_scr[half, h // 2, :, (h % 2) * HEAD_V:(h % 2 + 1) * HEAD_V]


def _gdn_scan(rp, qg, kt, dl, tm, aq, z, yb_sample, ong, wout):
    last = HALF_TILES - 1
    halves = lambda a: a.reshape((2, a.shape[0] // 2) + a.shape[1:])
    both = lambda a: pl.BlockSpec((2, a.shape[0] // PROMPT_TILES, a.shape[1]),
                                  lambda i: (0, jnp.minimum(i, last), 0))
    ztile = lambda off: pl.BlockSpec((SEQ_TILE, D_DELTA_V),
                                     lambda i: (jnp.minimum(i, last) + off, 0))
    ops = (rp, qg, kt, dl, tm, aq)
    yb, ss = pl.pallas_call(
        _gdn_scan_kernel,
        grid=(HALF_TILES + 1,),
        in_specs=[both(a) for a in ops] + [
            ztile(0), ztile(HALF_TILES),
            pl.BlockSpec((DEC_BATCH, D_MODEL), lambda i: (0, 0)),
            pl.BlockSpec((1, 128), lambda i: (0, 0)),
            pl.BlockSpec((D_DELTA_V, D_MODEL), lambda i: (0, 0))],
        out_specs=[pl.BlockSpec((2, SEQ_TILE, D_MODEL), lambda i: (0, i, 0)),
                   pl.BlockSpec((2, None, N_HEADS, HEAD_K, HEAD_V),
                                lambda i: (0, jnp.minimum(i // SEQ_TILES, HALF_SEQS - 1), 0, 0, 0))],
        out_shape=[jax.ShapeDtypeStruct((2, (HALF_TILES + 1) * TOK_TILE, D_MODEL), F32),
                   jax.ShapeDtypeStruct((2, HALF_SEQS, N_HEADS, HEAD_K, HEAD_V), F32)],
        scratch_shapes=[pltpu.VMEM((2, SEQ_TILE, D_DELTA_V), F32),
                        pltpu.VMEM((2, N_HEADS // 2, HEAD_K, 2 * HEAD_V), F32)],
        compiler_params=pltpu.CompilerParams(
            dimension_semantics=("arbitrary",), vmem_limit_bytes=VMEM_BIG),
        name="gdn_scan",
    )(*[halves(a) for a in ops], z, z, yb_sample, ong, wout)
    return yb, ss.reshape(BATCH, N_HEADS, HEAD_K, HEAD_V)


def _gdn_sample_kernel(qkv_ref, z_ref, ab_ref, st_ref, s_ref, cw_ref, alog_ref, dtb_ref,
                       ong_ref, wout_ref, yb_ref, nst_ref, ns_ref):
    x = qkv_ref[...]
    cw = cw_ref[...]
    nbuf = QK_CONV - 1
    acc = cw[nbuf:nbuf + 1, :] * x
    for j in range(nbuf):
        acc = acc + cw[j:j + 1, :] * st_ref[:, j * CONV_DIM:(j + 1) * CONV_DIM]
    nst_ref[:, 0:(nbuf - 1) * CONV_DIM] = st_ref[:, CONV_DIM:nbuf * CONV_DIM]
    nst_ref[:, (nbuf - 1) * CONV_DIM:nbuf * CONV_DIM] = x
    qs, ks, vs = _qkv_act(acc)
    ab = ab_ref[...]
    g = -jnp.exp(alog_ref[...]) * jax.nn.softplus(ab + dtb_ref[...])
    beta = jax.nn.sigmoid(ab)
    eg = jnp.exp(g)
    z = z_ref[...]
    outs = []
    for h in range(N_HEADS):
        bh = beta[:, N_HEADS + h:N_HEADS + h + 1]
        egh = eg[:, h:h + 1]
        kb = ks[h] * bh
        w = kb * egh
        qg = qs[h] * egh
        u = vs[h] * bh
        qk = jnp.sum(qs[h] * ks[h], axis=-1, keepdims=True)
        o_rows = []
        for s in range(DEC_BLK):
            st = s_ref[s, h]
            wq = _bdot(jnp.concatenate([w[s:s + 1], qg[s:s + 1]], axis=0), st)
            vnew = u[s:s + 1] - wq[0:1]
            o_rows.append(wq[1:2] + qk[s:s + 1] * vnew)
            ns_ref[s, h] = st * egh[s:s + 1] + lax.dot_general(
                ks[h][s:s + 1].astype(BF16), vnew.astype(BF16), (((0,), (0,)), ((), ())),
                preferred_element_type=F32)
        o = jnp.concatenate(o_rows, axis=0)
        outs.append(_gate_norm(o, ong_ref[...], z[:, h * HEAD_V:(h + 1) * HEAD_V]))
    yb_ref[...] = _bdot(jnp.concatenate(outs, axis=1), wout_ref[...])


def _gdn_sample(qkv, z, ab, st, state_delta, layer, cw, alog, dtb, ong, wout):
    nbuf = QK_CONV - 1
    blk0 = N_PROMPT // DEC_BLK
    rows = lambda w: pl.BlockSpec((DEC_BLK, w), lambda i: (blk0 + i, 0))
    vec = pl.BlockSpec((1, 128), lambda i: (0, 0))
    sblk = pl.BlockSpec((DEC_BLK, N_HEADS, HEAD_K, HEAD_V), lambda i: (i, 0, 0, 0))
    return pl.pallas_call(
        _gdn_sample_kernel,
        grid=(DEC_BATCH // DEC_BLK,),
        in_specs=[rows(CONV_DIM), rows(D_DELTA_V), rows(128),
                  pl.BlockSpec((DEC_BLK, nbuf * CONV_DIM), lambda i: (i, 0)),
                  pl.BlockSpec((None, DEC_BLK, N_HEADS, HEAD_K, HEAD_V),
                               lambda i: (layer, i, 0, 0, 0)),
                  pl.BlockSpec((QK_CONV, CONV_DIM), lambda i: (0, 0)),
                  vec, vec, vec,
                  pl.BlockSpec((D_DELTA_V, D_MODEL), lambda i: (0, 0))],
        out_specs=[pl.BlockSpec((DEC_BLK, D_MODEL), lambda i: (i, 0)),
                   pl.BlockSpec((DEC_BLK, nbuf * CONV_DIM), lambda i: (i, 0)),
                   sblk],
        out_shape=[jax.ShapeDtypeStruct((DEC_BATCH, D_MODEL), F32),
                   jax.ShapeDtypeStruct((DEC_BATCH, nbuf * CONV_DIM), F32),
                   jax.ShapeDtypeStruct((DEC_BATCH, N_HEADS, HEAD_K, HEAD_V), F32)],
        compiler_params=pltpu.CompilerParams(dimension_semantics=("arbitrary",)),
        name="gdn_sample",
    )(qkv, z, ab, st, state_delta, cw, alog, dtb, ong, wout)


def _router_kernel(x_ref, yb_ref, gb_ref, wo_ref, g_ref, r_ref, xo_ref, h_ref, info_ref, cnt_ref):
    x = _gated_branch(x_ref[...], gb_ref, yb_ref[...], wo_ref)
    xo_ref[...] = x
    h = _rms(x, g_ref[...])
    h_ref[...] = h
    logits = _hdot(h, r_ref[...])
    lane = lax.broadcasted_iota(jnp.int32, logits.shape, 1)
    neg = jnp.float32(-jnp.inf)
    logits = jnp.where(lane < N_EXPERTS, logits, neg)
    m1 = jnp.max(logits, axis=-1, keepdims=True)
    i1 = jnp.min(jnp.where(logits == m1, lane, 128), axis=-1, keepdims=True)
    rest = jnp.where(lane == i1, neg, logits)
    m2 = jnp.max(rest, axis=-1, keepdims=True)
    i2 = jnp.min(jnp.where(rest == m2, lane, 128), axis=-1, keepdims=True)
    e = jnp.exp(m2 - m1)
    p1 = 1.0 / (1.0 + e)
    p2 = e / (1.0 + e)
    oh1 = jnp.where(lane == i1, 1.0, 0.0).astype(F32)
    oh2 = jnp.where(lane == i2, 1.0, 0.0).astype(F32)
    tr = lax.broadcasted_iota(jnp.int32, (TOK_TILE, TOK_TILE), 0)
    tc = lax.broadcasted_iota(jnp.int32, (TOK_TILE, TOK_TILE), 1)
    cum = _bdot(jnp.where(tr >= tc, 1.0, 0.0), oh1 + oh2)
    rank = cum - (oh1 + oh2)
    r1 = jnp.sum(oh1 * rank, axis=-1, keepdims=True)
    r2 = jnp.sum(oh2 * rank, axis=-1, keepdims=True)
    info = jnp.where(lane == 0, i1.astype(F32), 0.0)
    info = jnp.where(lane == 1, i2.astype(F32), info)
    info = jnp.where(lane == 2, p1, info)
    info = jnp.where(lane == 3, p2, info)
    info = jnp.where(lane == 4, r1, info)
    info = jnp.where(lane == 5, r2, info)
    info_ref[...] = info
    cnt_ref[...] = jnp.broadcast_to(cum[TOK_TILE - 1:TOK_TILE, :], (8, 128))


def _router(x, yb, gb, wo, g, r_pad):
    row = lambda w: pl.BlockSpec((TOK_TILE, w), lambda i: (i, 0))
    return pl.pallas_call(
        _router_kernel,
        grid=(N_TOK // TOK_TILE,),
        in_specs=[row(D_MODEL), _halves_tile_spec(), row(D_MODEL),
                  pl.BlockSpec((D_MODEL, D_MODEL), lambda i: (0, 0)),
                  pl.BlockSpec((1, D_MODEL), lambda i: (0, 0)),
                  pl.BlockSpec((D_MODEL, 128), lambda i: (0, 0))],
        out_specs=[row(D_MODEL), row(D_MODEL), row(128),
                   pl.BlockSpec((None, 8, 128), lambda i: (i, 0, 0))],
        out_shape=[jax.ShapeDtypeStruct((N_TOK, D_MODEL), F32),
                   jax.ShapeDtypeStruct((N_TOK, D_MODEL), F32),
                   jax.ShapeDtypeStruct((N_TOK, 128), F32),
                   jax.ShapeDtypeStruct((N_TOK // TOK_TILE, 8, 128), F32)],
        compiler_params=pltpu.CompilerParams(
            dimension_semantics=("arbitrary",), vmem_limit_bytes=VMEM_BIG),
        name="moe_router",
    )(x, yb, gb, wo, g, r_pad)


def _tile_rows_wait(src_ref, dst_ref, sem):
    pltpu.make_async_copy(src_ref.at[pl.ds(0, TOK_TILE), :], dst_ref, sem).wait()


def _dispatch_kernel(dest_ref, xs_in_ref, h_ref, xs_ref, sem):
    del xs_in_ref
    base = pl.program_id(0) * (2 * TOK_TILE)

    def start(r, c):
        for k in range(2):
            pltpu.make_async_copy(
                h_ref.at[pl.ds(r, 1), :],
                xs_ref.at[pl.ds(dest_ref[base + 2 * r + k], 1), :], sem).start()
        return c

    lax.fori_loop(0, TOK_TILE, start, 0, unroll=8)
    for _ in range(2):
        _tile_rows_wait(h_ref, xs_ref.at[pl.ds(0, TOK_TILE), :], sem)


def _dispatch(dest, xs0, h):
    return pl.pallas_call(
        _dispatch_kernel,
        grid_spec=pltpu.PrefetchScalarGridSpec(
            num_scalar_prefetch=1,
            grid=(N_TOK // TOK_TILE,),
            in_specs=[pl.BlockSpec(memory_space=pl.ANY),
                      pl.BlockSpec((TOK_TILE, D_MODEL), lambda i, d: (i, 0))],
            out_specs=pl.BlockSpec(memory_space=pl.ANY),
            scratch_shapes=[pltpu.SemaphoreType.DMA(())]),
        out_shape=jax.ShapeDtypeStruct((MOE_ROWS, D_MODEL), F32),
        input_output_aliases={1: 0},
        compiler_params=pltpu.CompilerParams(dimension_semantics=("arbitrary",)),
        name="moe_dispatch",
    )(dest, xs0, h)


def _experts_kernel(te_ref, tv_ref, x_ref, wg_ref, wu_ref, wd_ref, o_ref):
    i = pl.program_id(0)

    @pl.when(tv_ref[i] > 0)
    def _():
        o_ref[...] = _swiglu_rows(x_ref[...].astype(BF16), wg_ref, wu_ref, wd_ref)

    @pl.when(tv_ref[i] == 0)
    def _():
        o_ref[...] = jnp.zeros((MOE_TILE, D_MODEL), F32)


def _experts(tile_expert, tile_valid, xs, wg, wu, wd):
    row = pl.BlockSpec((MOE_TILE, D_MODEL), lambda i, te, tv: (i, 0))
    return pl.pallas_call(
        _experts_kernel,
        grid_spec=pltpu.PrefetchScalarGridSpec(
            num_scalar_prefetch=2,
            grid=(MOE_ROWS // MOE_TILE,),
            in_specs=[row,
                      pl.BlockSpec((None, D_MODEL, D_FF), lambda i, te, tv: (te[i], 0, 0)),
                      pl.BlockSpec((None, D_MODEL, D_FF), lambda i, te, tv: (te[i], 0, 0)),
                      pl.BlockSpec((None, D_FF, D_MODEL), lambda i, te, tv: (te[i], 0, 0))],
            out_specs=row),
        out_shape=jax.ShapeDtypeStruct((MOE_ROWS, D_MODEL), F32),
        compiler_params=pltpu.CompilerParams(
            dimension_semantics=("arbitrary",), vmem_limit_bytes=VMEM_BIG),
        name="moe_experts",
    )(tile_expert, tile_valid, xs, wg, wu, wd)


def _combine_kernel(dest_ref, x_ref, info_ref, g_ref, ys_ref, yp_ref, ysm_ref, ybuf, sem):
    i = pl.program_id(0)
    n = pl.num_programs(0)

    def start_tile(tile):
        slot = tile % 2
        base = tile * (2 * TOK_TILE)

        def start(r, c):
            for k in range(2):
                pltpu.make_async_copy(
                    ys_ref.at[pl.ds(dest_ref[base + 2 * r + k], 1), :],
                    ybuf.at[slot, k, pl.ds(r, 1), :], sem.at[slot]).start()
            return c

        lax.fori_loop(0, TOK_TILE, start, 0, unroll=8)

    @pl.when(i == 0)
    def _():
        start_tile(i)

    @pl.when(i + 1 < n)
    def _():
        start_tile(i + 1)

    slot = i % 2
    for k in range(2):
        _tile_rows_wait(ys_ref, ybuf.at[slot, k], sem.at[slot])
    info = info_ref[...]
    x = x_ref[...] + info[:, 2:3] * ybuf[slot, 0] + info[:, 3:4] * ybuf[slot, 1]
    y = _rms(x, g_ref[...])

    @pl.when(i < PROMPT_TILES)
    def _():
        yp_ref[...] = y

    @pl.when(i == PROMPT_TILES)
    def _():
        ysm_ref[...] = y[0:DEC_BATCH, :]


def _combine(dest, x, info, g, ys):
    last = PROMPT_TILES - 1
    return pl.pallas_call(
        _combine_kernel,
        grid_spec=pltpu.PrefetchScalarGridSpec(
            num_scalar_prefetch=1,
            grid=(N_TOK // TOK_TILE,),
            in_specs=[pl.BlockSpec((TOK_TILE, D_MODEL), lambda i, d: (i, 0)),
                      pl.BlockSpec((TOK_TILE, 128), lambda i, d: (i, 0)),
                      pl.BlockSpec((1, D_MODEL), lambda i, d: (0, 0)),
                      pl.BlockSpec(memory_space=pl.ANY)],
            out_specs=[pl.BlockSpec((TOK_TILE, D_MODEL), lambda i, d: (jnp.minimum(i, last), 0)),
                       pl.BlockSpec((DEC_BATCH, D_MODEL), lambda i, d: (0, 0))],
            scratch_shapes=[pltpu.VMEM((2, 2, TOK_TILE, D_MODEL), F32),
                            pltpu.SemaphoreType.DMA((2,))]),
        out_shape=[jax.ShapeDtypeStruct((N_PROMPT, D_MODEL), F32),
                   jax.ShapeDtypeStruct((DEC_BATCH, D_MODEL), F32)],
        compiler_params=pltpu.CompilerParams(dimension_semantics=("arbitrary",)),
        name="moe_combine",
    )(dest, x, info, g, ys)


def _moe_layer(x, yb, gb, wo, g2, router_w, wg, wu, wd, final_g):
    r_pad = jnp.zeros((D_MODEL, 128), F32).at[:, :N_EXPERTS].set(router_w)
    x, h, info, cnt = _router(x, yb, gb, wo, g2, r_pad)
    n_tiles = N_TOK // TOK_TILE
    counts = cnt[:, 0, :N_EXPERTS].astype(jnp.int32)
    total = jnp.sum(counts, axis=0)
    padded = ((total + MOE_TILE - 1) // MOE_TILE) * MOE_TILE
    ends = jnp.cumsum(padded)
    starts = ends - padded
    base = starts[None, :] + jnp.cumsum(counts, axis=0) - counts
    choice = info[:, 0:2].astype(jnp.int32).reshape(n_tiles, TOK_TILE, 2)
    rank = info[:, 4:6].astype(jnp.int32).reshape(n_tiles, TOK_TILE, 2)
    onehot = choice[..., None] == jnp.arange(N_EXPERTS, dtype=jnp.int32)
    dest = (jnp.sum(jnp.where(onehot, base[:, None, None, :], 0), axis=-1) + rank).reshape(-1)
    tile_start = jnp.arange(MOE_ROWS // MOE_TILE, dtype=jnp.int32) * MOE_TILE
    tile_expert = jnp.minimum(
        jnp.sum((tile_start[:, None] >= ends[None, :]).astype(jnp.int32), axis=1), N_EXPERTS - 1)
    tile_valid = (tile_start < ends[-1]).astype(jnp.int32)
    last_used = jnp.max(jnp.where(tile_valid > 0, tile_expert, 0))
    tile_expert = jnp.where(tile_valid > 0, tile_expert, last_used).astype(jnp.int32)

    xs = _dispatch(dest, jnp.zeros((MOE_ROWS, D_MODEL), F32), h)
    ys = _experts(tile_expert, tile_valid, xs, wg, wu, wd)
    return _combine(dest, x, info, final_g, ys)


def _rearranged_w_in(w_in):
    off_qkv = 2 * D_CONV
    off_z = off_qkv + CONV_DIM
    off_a = off_z + D_DELTA_V
    off_ga = off_a + 2 * N_HEADS
    off_gb = off_ga + D_MODEL
    w_in = w_in.astype(BF16)
    return jnp.concatenate(
        [w_in[:, :off_a], w_in[:, off_ga:off_gb], w_in[:, off_gb:], w_in[:, off_a:off_ga],
         jnp.zeros((D_MODEL, 128 - 2 * N_HEADS), BF16)], axis=1)


def _lane_vec(v, offset=0):
    return jnp.zeros((1, 128), F32).at[0, offset:offset + v.shape[0]].set(v)


def kernel(x_prompt, x_sample, state_delta, state_qkv_conv, state_dwconv, norm1_g, w_in, qkv_conv_w, a_log, dt_bias, o_norm_g, w_delta_out, dw_w, dw_b, ln_g, ln_b, w_conv_out, w_o, norm2_g, ffn_w_gate, ffn_w_up, ffn_w_down, router_w, exp_w_gate, exp_w_up, exp_w_down, final_norm_g):
    x = (x_prompt.reshape(N_PROMPT, D_MODEL), x_sample.reshape(DEC_BATCH, D_MODEL))
    s_p, q_p, d_p, s_s, q_s, d_s = [], [], [], [], [], []
    for l in range(DEPTH):
        glu, qkv, z, ga, gb, ab = _norm_proj(x, norm1_g[l][None, :], _rearranged_w_in(w_in[l]))
        dwb, lng, lnb = dw_b[l][None, :], ln_g[l][None, :], ln_b[l][None, :]
        wco = w_conv_out[l].astype(BF16)
        ya_s, dst_s = _conv_sample(glu, state_dwconv[l].reshape(DEC_BATCH, -1), dw_w[l], dwb, lng, lnb, wco)
        wo = w_o[l].astype(BF16)
        x, dst_p = _conv_prompt(x, glu, ga, ya_s, dw_w[l], dwb, lng, lnb, wco, wo)
        alog, dtb, ong = _lane_vec(a_log[l]), _lane_vec(dt_bias[l]), o_norm_g[l][None, :]
        wdo = w_delta_out[l].astype(BF16)
        yb_s, qst_s, sst_s = _gdn_sample(qkv, z, ab, state_qkv_conv[l].reshape(DEC_BATCH, -1),
                                         state_delta, l, qkv_conv_w[l], alog, dtb, ong, wdo)
        rp, qg, kt, dl, m, aq, qst_p = _gdn_pre(qkv, ab, qkv_conv_w[l], alog, dtb)
        yb, sst_p = _gdn_scan(rp, qg, kt, dl, _gdn_solve(m), aq, z, yb_s, ong, wdo)
        i = l // 2
        if l % 2 == 0:
            x = _ffn(x, yb, gb, wo, norm2_g[l][None, :], ffn_w_gate[i].astype(BF16),
                     ffn_w_up[i].astype(BF16), ffn_w_down[i].astype(BF16))
        else:
            y_prompt, y_sample = _moe_layer(
                x, yb, gb, wo, norm2_g[l][None, :], router_w[i], exp_w_gate[i].astype(BF16),
                exp_w_up[i].astype(BF16), exp_w_down[i].astype(BF16), final_norm_g[None, :])
        s_p.append(sst_p)
        q_p.append(qst_p)
        d_p.append(dst_p)
        s_s.append(sst_s)
        q_s.append(qst_s.reshape(DEC_BATCH, QK_CONV - 1, CONV_DIM))
        d_s.append(dst_s.reshape(DEC_BATCH, DW_WIDTH - 1, D_CONV))
    return (y_prompt.reshape(BATCH, SEQ, D_MODEL), y_sample.reshape(DEC_BATCH, 1, D_MODEL), jnp.stack(s_p), jnp.stack(q_p), jnp.stack(d_p),
            jnp.stack(s_s), jnp.stack(q_s), jnp.stack(d_s))
```

```python
import functools

import jax
import jax.numpy as jnp
from jax import lax
from jax.experimental import pallas as pl
from jax.experimental.pallas import tpu as pltpu

F32 = jnp.float32
BF16 = jnp.bfloat16
HIGHEST = lax.Precision.HIGHEST

D_MODEL = 1024
BATCH = 8
SEQ = 2048
DEPTH = 2
DEC_BATCH = 128
D_CONV = 512
DW_WIDTH = 31
N_HEADS = 4
HEAD_K = 128
HEAD_V = 128
QK_CONV = 4
D_DELTA_K = N_HEADS * HEAD_K
D_DELTA_V = N_HEADS * HEAD_V
CONV_DIM = 2 * D_DELTA_K + D_DELTA_V
D_FF = 2816
N_EXPERTS = 8
EPS = 1e-6

N_PROMPT = BATCH * SEQ
TOK_TILE = 512
SEQ_TILE = TOK_TILE
N_TOK = N_PROMPT + TOK_TILE
SEQ_TILES = SEQ // SEQ_TILE
PROMPT_TILES = N_PROMPT // TOK_TILE
HALF_TILES = PROMPT_TILES // 2
HALF_SEQS = BATCH // 2
CHUNK = 64
M_PITCH = 68
SLABS_TILE = 2 * (SEQ_TILE // CHUNK)
M_ROWS_TILE = SLABS_TILE * M_PITCH
SOLVE_SLABS = 128
ROW_BLK = 64
FF_HALF = D_FF // 2
MOE_TILE = 256
MOE_ROWS = ((2 * N_TOK + N_EXPERTS * (MOE_TILE - 1)) // MOE_TILE + 1) * MOE_TILE
MOE_PAD_ROWS = MOE_ROWS - 2 * N_TOK
assert MOE_PAD_ROWS % TOK_TILE == 0
DEC_BLK = 8

C_GLU, C_QKV, C_Z, C_GA, C_GB, C_AB = 0, 1024, 2560, 3072, 4096, 5120
N_PROJ = 5248

VMEM_BIG = 56 * 1024 * 1024


def _bdot(a, b):
    return jnp.dot(a.astype(BF16), b.astype(BF16), preferred_element_type=F32)


def _hdot(a, b):
    return jnp.dot(a, b, preferred_element_type=F32, precision=HIGHEST)


def _rms(x, g):
    return x * lax.rsqrt(jnp.mean(x * x, axis=-1, keepdims=True) + EPS) * g


def _silu(x):
    return x * jax.nn.sigmoid(x)


def _resident(shape):
    return pl.BlockSpec(shape, lambda *_: (0,) * len(shape), pipeline_mode=pl.Buffered(1))


def _stream_x_specs(x):
    if isinstance(x, tuple):
        last = PROMPT_TILES - 1
        return ([pl.BlockSpec((TOK_TILE, D_MODEL), lambda i: (jnp.minimum(i, last), 0)),
                 pl.BlockSpec((DEC_BATCH, D_MODEL), lambda i: (0, 0))], list(x))
    return [pl.BlockSpec((TOK_TILE, D_MODEL), lambda i: (i, 0))], [x]


def _stream_x_tile(x_refs, xbuf):
    if len(x_refs) == 1:
        return x_refs[0][...]
    xp_ref, xs_ref = x_refs
    i = pl.program_id(0)

    @pl.when(i < PROMPT_TILES)
    def _():
        xbuf[...] = xp_ref[...]

    @pl.when(i == PROMPT_TILES)
    def _():
        xbuf[0:DEC_BATCH, :] = xs_ref[...]
        xbuf[DEC_BATCH:, :] = jnp.zeros((TOK_TILE - DEC_BATCH, D_MODEL), F32)

    return xbuf[...]


def _norm_proj_kernel(n_x, *refs):
    x_refs = refs[:n_x]
    g_ref, w_ref, glu_ref, qkv_ref, z_ref, ga_ref, gb_ref, ab_ref, xbuf = refs[n_x:]
    h = _rms(_stream_x_tile(x_refs, xbuf), g_ref[...]).astype(BF16)

    def proj(lo, hi):
        return jnp.dot(h, w_ref[:, lo:hi], preferred_element_type=F32)

    glu_ref[...] = proj(C_GLU, C_QKV)
    qkv_ref[...] = proj(C_QKV, C_Z)
    z_ref[...] = proj(C_Z, C_GA)
    ga_ref[...] = proj(C_GA, C_GB)
    gb_ref[...] = proj(C_GB, C_AB)
    ab_ref[...] = proj(C_AB, N_PROJ)


def _norm_proj(x, g, w_all):
    row = lambda w: pl.BlockSpec((TOK_TILE, w), lambda i: (i, 0))
    widths = (1024, CONV_DIM, D_DELTA_V, D_MODEL, D_MODEL, 128)
    x_specs, x_ops = _stream_x_specs(x)
    return pl.pallas_call(
        functools.partial(_norm_proj_kernel, len(x_ops)),
        grid=(N_TOK // TOK_TILE,),
        in_specs=x_specs + [pl.BlockSpec((1, D_MODEL), lambda i: (0, 0)),
                            _resident((D_MODEL, N_PROJ))],
        out_specs=[row(w) for w in widths],
        out_shape=[jax.ShapeDtypeStruct((N_TOK, w), F32) for w in widths],
        scratch_shapes=[pltpu.VMEM((TOK_TILE, D_MODEL), F32)],
        compiler_params=pltpu.CompilerParams(
            dimension_semantics=("arbitrary",), vmem_limit_bytes=VMEM_BIG),
        name="norm_proj",
    )(*x_ops, g, w_all)


def _gated_branch(x, gate_ref, y, wo_ref):
    return x + _bdot(jax.nn.sigmoid(gate_ref[...]) * y, wo_ref[...])


def _halves_tile_spec():
    half = lambda i: i // HALF_TILES - i // (2 * HALF_TILES)
    return pl.BlockSpec((None, TOK_TILE, D_MODEL),
                        lambda i, *_: (half(i), i - HALF_TILES * half(i), 0))


def _swiglu_rows(h, wg_ref, wu_ref, wd_ref):
    acc = None
    for lo in (0, FF_HALF):
        a = jnp.dot(h, wg_ref[:, lo:lo + FF_HALF], preferred_element_type=F32)
        b = jnp.dot(h, wu_ref[:, lo:lo + FF_HALF], preferred_element_type=F32)
        part = jnp.dot((_silu(a) * b).astype(BF16), wd_ref[lo:lo + FF_HALF, :],
                       preferred_element_type=F32)
        acc = part if acc is None else acc + part
    return acc


def _ffn_kernel(x_ref, yb_ref, gb_ref, wo_ref, g_ref, wg_ref, wu_ref, wd_ref, o_ref):
    x = _gated_branch(x_ref[...], gb_ref, yb_ref[...], wo_ref)
    h = _rms(x, g_ref[...]).astype(BF16)
    o_ref[...] = x + _swiglu_rows(h, wg_ref, wu_ref, wd_ref)


def _ffn(x, yb, gb, wo, g, wg, wu, wd):
    row = pl.BlockSpec((TOK_TILE, D_MODEL), lambda i: (i, 0))
    return pl.pallas_call(
        _ffn_kernel,
        grid=(N_TOK // TOK_TILE,),
        in_specs=[row, _halves_tile_spec(), row, _resident((D_MODEL, D_MODEL)),
                  pl.BlockSpec((1, D_MODEL), lambda i: (0, 0)),
                  _resident((D_MODEL, D_FF)), _resident((D_MODEL, D_FF)),
                  _resident((D_FF, D_MODEL))],
        out_specs=row,
        out_shape=jax.ShapeDtypeStruct((N_TOK, D_MODEL), F32),
        compiler_params=pltpu.CompilerParams(
            dimension_semantics=("arbitrary",), vmem_limit_bytes=VMEM_BIG),
        name="ffn_dense",
    )(x, yb, gb, wo, g, wg, wu, wd)


def _conv_tail(c, dwb, lng, lnb):
    c = c + dwb
    mu = jnp.mean(c, axis=-1, keepdims=True)
    var = jnp.mean(jnp.square(c - mu), axis=-1, keepdims=True)
    c = (c - mu) * lax.rsqrt(var + EPS) * lng + lnb
    return _silu(c)


def _sample_rows_tile(ys_ref, y_ref):
    y_ref[0:DEC_BATCH, :] = ys_ref[...]
    y_ref[DEC_BATCH:, :] = jnp.zeros((TOK_TILE - DEC_BATCH, D_MODEL), F32)


def _conv_prompt_kernel(n_x, *refs):
    x_refs = refs[:n_x]
    (glu_ref, ga_ref, ys_ref, dww_ref, dwb_ref, lng_ref, lnb_ref, wout_ref, wo_ref,
     o_ref, st_ref, ubuf, ushift, cbuf, xbuf) = refs[n_x:]
    i = pl.program_id(0)
    if n_x == 2:
        _stream_x_tile(x_refs, xbuf)
        x_ref = xbuf
    else:
        x_ref = x_refs[0]

    @pl.when(i < PROMPT_TILES)
    def _():
        _conv_prompt_tile(i % SEQ_TILES, glu_ref, dww_ref, dwb_ref, lng_ref, lnb_ref,
                          st_ref, ubuf, ushift, cbuf)
        ya = jnp.dot(cbuf[...], wout_ref[...], preferred_element_type=F32)
        o_ref[...] = _gated_branch(x_ref[...], ga_ref, ya, wo_ref)

    @pl.when(i == PROMPT_TILES)
    def _():
        ya = jnp.concatenate(
            [ys_ref[...], jnp.zeros((TOK_TILE - DEC_BATCH, D_MODEL), F32)], axis=0)
        o_ref[...] = _gated_branch(x_ref[...], ga_ref, ya, wo_ref)


def _conv_prompt_tile(t, glu_ref, dww_ref, dwb_ref, lng_ref, lnb_ref, st_ref, ubuf, ushift, cbuf):
    halo = 32

    @pl.when(t == 0)
    def _():
        ubuf[0:halo, :] = jnp.zeros((halo, D_CONV), F32)

    @pl.when(t > 0)
    def _():
        ubuf[0:halo, :] = ubuf[SEQ_TILE:SEQ_TILE + halo, :]

    glu = glu_ref[...]
    ubuf[halo:halo + SEQ_TILE, :] = glu[:, :D_CONV] * jax.nn.sigmoid(glu[:, D_CONV:])
    for r in range(1, 8):
        ushift[r - 1, 0:SEQ_TILE + halo - 8, :] = ubuf[r:r + SEQ_TILE + halo - 8, :]
    w = dww_ref[...]
    first = halo - (DW_WIDTH - 1)
    for rb in range(SEQ_TILE // ROW_BLK):
        r0 = rb * ROW_BLK
        acc = jnp.zeros((ROW_BLK, D_CONV), F32)
        for j in range(DW_WIDTH):
            a, r = divmod(first + j, 8)
            if r == 0:
                win = ubuf[r0 + 8 * a:r0 + 8 * a + ROW_BLK, :]
            else:
                win = ushift[r - 1, r0 + 8 * a:r0 + 8 * a + ROW_BLK, :]
            acc = acc + w[j:j + 1, :] * win
        c = _conv_tail(acc, dwb_ref[...], lng_ref[...], lnb_ref[...])
        cbuf[r0:r0 + ROW_BLK, :] = c.astype(BF16)

    @pl.when(t == SEQ_TILES - 1)
    def _():
        st_ref[...] = ubuf[halo + SEQ_TILE - (DW_WIDTH - 1):halo + SEQ_TILE, :]


def _seq_of_tile(i):
    return jnp.minimum(i // SEQ_TILES, BATCH - 1)


def _conv_prompt(x, glu, ga, ya_sample, dww, dwb, lng, lnb, wout, wo):
    vec = pl.BlockSpec((1, D_CONV), lambda i: (0, 0))
    row = pl.BlockSpec((SEQ_TILE, D_MODEL), lambda i: (i, 0))
    x_specs, x_ops = _stream_x_specs(x)
    return pl.pallas_call(
        functools.partial(_conv_prompt_kernel, len(x_ops)),
        grid=(PROMPT_TILES + 1,),
        in_specs=x_specs + [
            pl.BlockSpec((SEQ_TILE, 2 * D_CONV), lambda i: (i, 0)), row,
            pl.BlockSpec((DEC_BATCH, D_MODEL), lambda i: (0, 0)),
            pl.BlockSpec((DW_WIDTH, D_CONV), lambda i: (0, 0)),
            vec, vec, vec,
            pl.BlockSpec((D_CONV, D_MODEL), lambda i: (0, 0)),
            pl.BlockSpec((D_MODEL, D_MODEL), lambda i: (0, 0))],
        out_specs=[row,
                   pl.BlockSpec((None, DW_WIDTH - 1, D_CONV), lambda i: (_seq_of_tile(i), 0, 0))],
        out_shape=[jax.ShapeDtypeStruct((N_TOK, D_MODEL), F32),
                   jax.ShapeDtypeStruct((BATCH, DW_WIDTH - 1, D_CONV), F32)],
        scratch_shapes=[pltpu.VMEM((SEQ_TILE + 32, D_CONV), F32),
                        pltpu.VMEM((7, SEQ_TILE + 24, D_CONV), F32),
                        pltpu.VMEM((SEQ_TILE, D_CONV), BF16),
                        pltpu.VMEM((TOK_TILE, D_MODEL), F32)],
        compiler_params=pltpu.CompilerParams(
            dimension_semantics=("arbitrary",), vmem_limit_bytes=VMEM_BIG),
        name="conv_prompt",
    )(*x_ops, glu, ga, ya_sample, dww, dwb, lng, lnb, wout, wo)


def _conv_sample_kernel(glu_ref, st_ref, dww_ref, dwb_ref, lng_ref, lnb_ref, wout_ref,
                        ya_ref, nst_ref):
    glu = glu_ref[...]
    u = glu[:, :D_CONV] * jax.nn.sigmoid(glu[:, D_CONV:])
    w = dww_ref[...]
    nbuf = DW_WIDTH - 1
    acc = w[nbuf:nbuf + 1, :] * u
    for j in range(nbuf):
        acc = acc + w[j:j + 1, :] * st_ref[:, j * D_CONV:(j + 1) * D_CONV]
    c = _conv_tail(acc, dwb_ref[...], lng_ref[...], lnb_ref[...])
    ya_ref[...] = _bdot(c, wout_ref[...])
    nst_ref[:, 0:(nbuf - 1) * D_CONV] = st_ref[:, D_CONV:nbuf * D_CONV]
    nst_ref[:, (nbuf - 1) * D_CONV:nbuf * D_CONV] = u


def _conv_sample(glu, st, dww, dwb, lng, lnb, wout):
    nbuf = DW_WIDTH - 1
    blk = N_PROMPT // DEC_BATCH
    vec = pl.BlockSpec((1, D_CONV), lambda i: (0, 0))
    return pl.pallas_call(
        _conv_sample_kernel,
        grid=(1,),
        in_specs=[pl.BlockSpec((DEC_BATCH, 2 * D_CONV), lambda i: (blk, 0)),
                  pl.BlockSpec((DEC_BATCH, nbuf * D_CONV), lambda i: (0, 0)),
                  pl.BlockSpec((DW_WIDTH, D_CONV), lambda i: (0, 0)),
                  vec, vec, vec,
                  pl.BlockSpec((D_CONV, D_MODEL), lambda i: (0, 0))],
        out_specs=[pl.BlockSpec((DEC_BATCH, D_MODEL), lambda i: (0, 0)),
                   pl.BlockSpec((DEC_BATCH, nbuf * D_CONV), lambda i: (0, 0))],
        out_shape=[jax.ShapeDtypeStruct((DEC_BATCH, D_MODEL), F32),
                   jax.ShapeDtypeStruct((DEC_BATCH, nbuf * D_CONV), F32)],
        compiler_params=pltpu.CompilerParams(dimension_semantics=("arbitrary",)),
        name="conv_sample",
    )(glu, st, dww, dwb, lng, lnb, wout)


def _qkv_act(c):
    c = _silu(c)
    qs, ks, vs = [], [], []
    for h in range(N_HEADS):
        q = c[:, h * HEAD_K:(h + 1) * HEAD_K]
        k = c[:, D_DELTA_K + h * HEAD_K:D_DELTA_K + (h + 1) * HEAD_K]
        qs.append(q * lax.rsqrt(jnp.sum(q * q, axis=-1, keepdims=True) + EPS) * (HEAD_K ** -0.5))
        ks.append(k * lax.rsqrt(jnp.sum(k * k, axis=-1, keepdims=True) + EPS))
        vs.append(c[:, 2 * D_DELTA_K + h * HEAD_V:2 * D_DELTA_K + (h + 1) * HEAD_V])
    return qs, ks, vs


def _gate_norm(o, ong, z):
    o = o * lax.rsqrt(jnp.mean(o * o, axis=-1, keepdims=True) + EPS) * ong
    return o * _silu(z)


def _gdn_pre_kernel(qkv_ref, ab_ref, cw_ref, alog_ref, dtb_ref,
                    rp_ref, qg_ref, kt_ref, dl_ref, m_ref, aq_ref, sq_ref, xbuf):
    t = pl.program_id(0) % SEQ_TILES
    halo = 8

    @pl.when(t == 0)
    def _():
        xbuf[0:halo, :] = jnp.zeros((halo, CONV_DIM), F32)

    @pl.when(t > 0)
    def _():
        xbuf[0:halo, :] = xbuf[SEQ_TILE:SEQ_TILE + halo, :]

    xbuf[halo:halo + SEQ_TILE, :] = qkv_ref[...]

    @pl.when(t == SEQ_TILES - 1)
    def _():
        sq_ref[...] = xbuf[halo + SEQ_TILE - (QK_CONV - 1):halo + SEQ_TILE, :]

    ab = ab_ref[...]
    lane = lax.broadcasted_iota(jnp.int32, (SEQ_TILE, 128), 1)
    gb = jnp.where(lane < N_HEADS,
                   -jnp.exp(alog_ref[...]) * jax.nn.softplus(ab + dtb_ref[...]),
                   jax.nn.sigmoid(ab))
    cw = cw_ref[...]
    first = halo - (QK_CONV - 1)
    ri = lax.broadcasted_iota(jnp.int32, (CHUNK, 128), 0)
    li = lax.broadcasted_iota(jnp.int32, (CHUNK, 128), 1)
    left = li < CHUNK
    jj = jnp.where(left, li, li - CHUNK)
    strict = ri > jj
    tril = ri >= jj
    r64 = lax.broadcasted_iota(jnp.int32, (CHUNK, CHUNK), 0)
    c64 = lax.broadcasted_iota(jnp.int32, (CHUNK, CHUNK), 1)
    csum = jnp.where(r64 >= c64, 1.0, 0.0).astype(F32)
    for c in range(SEQ_TILE // CHUNK):
        r0 = c * CHUNK
        acc = cw[0:1, :] * xbuf[r0 + first:r0 + first + CHUNK, :]
        for j in range(1, QK_CONV):
            acc = acc + cw[j:j + 1, :] * xbuf[r0 + first + j:r0 + first + j + CHUNK, :]
        qs, ks, vs = _qkv_act(acc)
        gbc = gb[r0:r0 + CHUNK, :]
        gc = _hdot(csum, gbc)
        gct = gc.T
        eg = jnp.exp(gc)
        glast = gc[CHUNK - 1:CHUNK, :]
        kdec = jnp.exp(glast - gc)
        dl = jnp.exp(glast)
        for hp in range(N_HEADS // 2):
            h0, h1 = 2 * hp, 2 * hp + 1
            slab = 2 * c + hp
            kb, rhs, qg, kd = [], [], [], []
            for h in (h0, h1):
                beta = gbc[:, N_HEADS + h:N_HEADS + h + 1]
                egh = eg[:, h:h + 1]
                kb.append(ks[h] * beta)
                rhs.append(jnp.concatenate([vs[h] * beta, kb[-1] * egh], axis=1))
                qg.append(qs[h] * egh)
                kd.append(ks[h] * kdec[:, h:h + 1])
            rows = slice(slab * 2 * CHUNK, (slab + 1) * 2 * CHUNK)
            rp_ref[rows, :] = jnp.concatenate(rhs, axis=0).astype(BF16)
            qg_ref[rows, :] = jnp.concatenate(qg, axis=0).astype(BF16)
            kt_ref[rows, :] = jnp.concatenate(kd, axis=0).T.astype(BF16)
            dl_ref[slab * 8:(slab + 1) * 8, :] = jnp.concatenate(
                [jnp.broadcast_to(dl[:, h0:h0 + 1], (8, HEAD_V)),
                 jnp.broadcast_to(dl[:, h1:h1 + 1], (8, HEAD_V))], axis=1)
            lhs = jnp.concatenate(kb + [qs[h0], qs[h1]], axis=0)
            keys = jnp.concatenate([ks[h0], ks[h1]], axis=0)
            prod = lax.dot_general(lhs.astype(BF16), keys.astype(BF16), (((1,), (1,)), ((), ())),
                                   preferred_element_type=F32)
            kk = jnp.where(left, prod[0:CHUNK], prod[CHUNK:2 * CHUNK])
            qk = jnp.where(left, prod[2 * CHUNK:3 * CHUNK], prod[3 * CHUNK:4 * CHUNK])
            gcol = jnp.where(left, gc[:, h0:h0 + 1], gc[:, h1:h1 + 1])
            grow = jnp.concatenate([gct[h0:h0 + 1, :], gct[h1:h1 + 1, :]], axis=1)
            diff = gcol - grow
            m_ref[slab * M_PITCH:slab * M_PITCH + CHUNK, :] = jnp.where(
                strict, kk * jnp.exp(jnp.where(strict, diff, 0.0)), 0.0)
            m_ref[slab * M_PITCH + CHUNK:(slab + 1) * M_PITCH, :] = jnp.zeros(
                (M_PITCH - CHUNK, 128), F32)
            aq_ref[slab * CHUNK:(slab + 1) * CHUNK, :] = jnp.where(
                tril, qk * jnp.exp(jnp.where(tril, diff, 0.0)), 0.0).astype(BF16)


def _gdn_pre(qkv, ab, cw, alog, dtb):
    tile = lambda w: pl.BlockSpec((SEQ_TILE, w), lambda i: (i, 0))
    vec = pl.BlockSpec((1, 128), lambda i: (0, 0))
    slab_rows = SLABS_TILE * 2 * CHUNK
    per_tile = lambda rows, w: pl.BlockSpec((rows, w), lambda i: (i, 0))
    return pl.pallas_call(
        _gdn_pre_kernel,
        grid=(PROMPT_TILES,),
        in_specs=[tile(CONV_DIM), tile(128),
                  pl.BlockSpec((QK_CONV, CONV_DIM), lambda i: (0, 0)), vec, vec],
        out_specs=[per_tile(slab_rows, 2 * HEAD_V), per_tile(slab_rows, HEAD_K),
                   per_tile(slab_rows, 128), per_tile(SLABS_TILE * 8, 2 * HEAD_V),
                   per_tile(M_ROWS_TILE, 128), per_tile(SLABS_TILE * CHUNK, 128),
                   pl.BlockSpec((None, QK_CONV - 1, CONV_DIM), lambda i: (i // SEQ_TILES, 0, 0))],
        out_shape=[jax.ShapeDtypeStruct((PROMPT_TILES * slab_rows, 2 * HEAD_V), BF16),
                   jax.ShapeDtypeStruct((PROMPT_TILES * slab_rows, HEAD_K), BF16),
                   jax.ShapeDtypeStruct((PROMPT_TILES * slab_rows, 128), BF16),
                   jax.ShapeDtypeStruct((PROMPT_TILES * SLABS_TILE * 8, 2 * HEAD_V), F32),
                   jax.ShapeDtypeStruct((PROMPT_TILES * M_ROWS_TILE, 128), F32),
                   jax.ShapeDtypeStruct((PROMPT_TILES * SLABS_TILE * CHUNK, 128), BF16),
                   jax.ShapeDtypeStruct((BATCH, QK_CONV - 1, CONV_DIM), F32)],
        scratch_shapes=[pltpu.VMEM((SEQ_TILE + 8, CONV_DIM), F32)],
        compiler_params=pltpu.CompilerParams(dimension_semantics=("arbitrary",)),
        name="gdn_pre",
    )(qkv, ab, cw, alog, dtb)


def _gdn_solve_kernel(m_ref, t_ref, mt, xt):
    def to_lanes(i, c):
        g = m_ref[pl.ds(i, SOLVE_SLABS, stride=M_PITCH), :]
        mt[pl.ds(pl.multiple_of(i * 128, 128), 128), :] = g.T
        return c

    lax.fori_loop(0, CHUNK, to_lanes, 0, unroll=4)

    sub = lax.broadcasted_iota(jnp.int32, (8, SOLVE_SLABS), 0)
    zeros8 = jnp.zeros((8, SOLVE_SLABS), F32)

    for ib in range(CHUNK // 8):
        def solve_row(i, c, ib=ib):
            base = pl.multiple_of(i * 128, 128)
            unit = jnp.where(sub + 8 * ib == i, 1.0, 0.0).astype(F32)
            acc = tuple([zeros8] * ib + [unit]) * 2

            def columns(groups):
                def step(j, acc):
                    xb = pl.multiple_of(j * 128, 128)
                    m0 = mt[pl.ds(base + j, 1), :]
                    m1 = mt[pl.ds(base + CHUNK + j, 1), :]
                    acc = list(acc)
                    for k in range(groups):
                        acc[k] = acc[k] - m0 * xt[pl.ds(xb + 8 * k, 8), :]
                        acc[ib + 1 + k] = acc[ib + 1 + k] - m1 * xt[pl.ds(xb + CHUNK + 8 * k, 8), :]
                    return tuple(acc)
                return step

            for jb in range(ib):
                acc = lax.fori_loop(8 * jb, 8 * jb + 8, columns(jb + 1), acc, unroll=True)
            acc = lax.fori_loop(8 * ib, i, columns(ib + 1), acc)
            for k in range(CHUNK // 8):
                xt[pl.ds(base + 8 * k, 8), :] = acc[k] if k <= ib else zeros8
                xt[pl.ds(base + CHUNK + 8 * k, 8), :] = acc[ib + 1 + k] if k <= ib else zeros8
            return c

        lax.fori_loop(8 * ib, 8 * ib + 8, solve_row, 0)

    def from_lanes(i, c):
        x = xt[pl.ds(pl.multiple_of(i * 128, 128), 128), :]
        t_ref[pl.ds(i, SOLVE_SLABS, stride=M_PITCH), :] = x.T
        return c

    lax.fori_loop(0, CHUNK, from_lanes, 0, unroll=4)
    for k in range(CHUNK, M_PITCH):
        t_ref[pl.ds(k, SOLVE_SLABS, stride=M_PITCH), :] = jnp.zeros((SOLVE_SLABS, 128), F32)


def _gdn_solve(m):
    rows = SOLVE_SLABS * M_PITCH
    return pl.pallas_call(
        _gdn_solve_kernel,
        grid=(PROMPT_TILES * SLABS_TILE // SOLVE_SLABS,),
        in_specs=[pl.BlockSpec((rows, 128), lambda i: (i, 0))],
        out_specs=pl.BlockSpec((rows, 128), lambda i: (i, 0)),
        out_shape=jax.ShapeDtypeStruct(m.shape, F32),
        scratch_shapes=[pltpu.VMEM((CHUNK * 128, SOLVE_SLABS), F32),
                        pltpu.VMEM((CHUNK * 128, SOLVE_SLABS), F32)],
        compiler_params=pltpu.CompilerParams(dimension_semantics=("arbitrary",)),
        name="gdn_solve",
    )(m)


def _gdn_scan_kernel(rp_ref, qg_ref, kt_ref, dl_ref, tm_ref, aq_ref, z0_ref, z1_ref, ys_ref,
                     ong_ref, wout_ref, yb_ref, ss_ref, obuf, s_scr):
    i = pl.program_id(0)
    pl.when(i < HALF_TILES)(functools.partial(
        _gdn_scan_tile, i % SEQ_TILES, rp_ref, qg_ref, kt_ref, dl_ref, tm_ref, aq_ref,
        (z0_ref, z1_ref), ong_ref, wout_ref, yb_ref, ss_ref, obuf, s_scr))

    @pl.when(i == HALF_TILES)
    def _():
        yb_ref[0] = jnp.zeros((TOK_TILE, D_MODEL), F32)
        _sample_rows_tile(ys_ref, yb_ref.at[1])


def _gdn_scan_tile(t, rp_ref, qg_ref, kt_ref, dl_ref, tm_ref, aq_ref, z_refs, ong_ref, wout_ref,
                   yb_ref, ss_ref, obuf, s_scr):
    @pl.when(t == 0)
    def _():
        s_scr[...] = jnp.zeros((2, N_HEADS // 2, HEAD_K, 2 * HEAD_V), F32)

    left = lax.broadcasted_iota(jnp.int32, (CHUNK, 128), 1) < CHUNK
    ong = ong_ref[...]
    zero = jnp.zeros((CHUNK, HEAD_V), BF16)

    def pair_diag(x):
        x = x.astype(BF16)
        none = jnp.zeros_like(x)
        return jnp.concatenate([jnp.where(left, x, none), jnp.where(left, none, x)], axis=0)

    for c in range(SEQ_TILE // CHUNK):
        r0 = c * CHUNK
        for half in range(2):
            for hp in range(N_HEADS // 2):
                slab = 2 * c + hp
                rows = slice(slab * 2 * CHUNK, (slab + 1) * 2 * CHUNK)
                tm = pair_diag(tm_ref[half, slab * M_PITCH:slab * M_PITCH + CHUNK, :])
                aq = pair_diag(aq_ref[half, slab * CHUNK:(slab + 1) * CHUNK, :])
                uw = jnp.dot(tm, rp_ref[half, rows, :], preferred_element_type=F32)
                s = s_scr[half, hp]
                lhs = jnp.concatenate([uw[:, HEAD_V:].astype(BF16), qg_ref[half, rows, :]], axis=0)
                wq = jnp.dot(lhs, s.astype(BF16), preferred_element_type=F32)
                ws = jnp.concatenate([wq[0:CHUNK, 0:HEAD_V], wq[CHUNK:2 * CHUNK, HEAD_V:]], axis=0)
                qs = jnp.concatenate([wq[2 * CHUNK:3 * CHUNK, 0:HEAD_V], wq[3 * CHUNK:, HEAD_V:]],
                                     axis=0)
                vnew = (uw[:, :HEAD_V] - ws).astype(BF16)
                vn_bd = jnp.concatenate(
                    [jnp.concatenate([vnew[0:CHUNK], zero], axis=1),
                     jnp.concatenate([zero, vnew[CHUNK:]], axis=1)], axis=0)
                s_scr[half, hp] = s * dl_ref[half, slab * 8:slab * 8 + 1, :] + jnp.dot(
                    kt_ref[half, rows, :], vn_bd, preferred_element_type=F32)
                o = qs + jnp.dot(aq, vnew, preferred_element_type=F32)
                for n in range(2):
                    h = 2 * hp + n
                    zh = z_refs[half][r0:r0 + CHUNK, h * HEAD_V:(h + 1) * HEAD_V]
                    obuf[half, r0:r0 + CHUNK, h * HEAD_V:(h + 1) * HEAD_V] = _gate_norm(
                        o[n * CHUNK:(n + 1) * CHUNK], ong, zh)

    for half in range(2):
        yb_ref[half] = _bdot(obuf[half], wout_ref[...])

    @pl.when(t == SEQ_TILES - 1)
    def _():
        for half in range(2):
            for h in range(N_HEADS):
                ss_ref[half, h] = s_scr[half, h // 2, :, (h % 2) * HEAD_V:(h % 2 + 1) * HEAD_V]


def _gdn_scan(rp, qg, kt, dl, tm, aq, z, yb_sample, ong, wout):
    last = HALF_TILES - 1
    halves = lambda a: a.reshape((2, a.shape[0] // 2) + a.shape[1:])
    both = lambda a: pl.BlockSpec((2, a.shape[0] // PROMPT_TILES, a.shape[1]),
                                  lambda i: (0, jnp.minimum(i, last), 0))
    ztile = lambda off: pl.BlockSpec((SEQ_TILE, D_DELTA_V),
                                     lambda i: (jnp.minimum(i, last) + off, 0))
    ops = (rp, qg, kt, dl, tm, aq)
    yb, ss = pl.pallas_call(
        _gdn_scan_kernel,
        grid=(HALF_TILES + 1,),
        in_specs=[both(a) for a in ops] + [
            ztile(0), ztile(HALF_TILES),
            pl.BlockSpec((DEC_BATCH, D_MODEL), lambda i: (0, 0)),
            pl.BlockSpec((1, 128), lambda i: (0, 0)),
            pl.BlockSpec((D_DELTA_V, D_MODEL), lambda i: (0, 0))],
        out_specs=[pl.BlockSpec((2, SEQ_TILE, D_MODEL), lambda i: (0, i, 0)),
                   pl.BlockSpec((2, None, N_HEADS, HEAD_K, HEAD_V),
                                lambda i: (0, jnp.minimum(i // SEQ_TILES, HALF_SEQS - 1), 0, 0, 0))],
        out_shape=[jax.ShapeDtypeStruct((2, (HALF_TILES + 1) * TOK_TILE, D_MODEL), F32),
                   jax.ShapeDtypeStruct((2, HALF_SEQS, N_HEADS, HEAD_K, HEAD_V), F32)],
        scratch_shapes=[pltpu.VMEM((2, SEQ_TILE, D_DELTA_V), F32),
                        pltpu.VMEM((2, N_HEADS // 2, HEAD_K, 2 * HEAD_V), F32)],
        compiler_params=pltpu.CompilerParams(
            dimension_semantics=("arbitrary",), vmem_limit_bytes=VMEM_BIG),
        name="gdn_scan",
    )(*[halves(a) for a in ops], z, z, yb_sample, ong, wout)
    return yb, ss.reshape(BATCH, N_HEADS, HEAD_K, HEAD_V)


def _gdn_sample_kernel(qkv_ref, z_ref, ab_ref, st_ref, s_ref, cw_ref, alog_ref, dtb_ref,
                       ong_ref, wout_ref, yb_ref, nst_ref, ns_ref):
    x = qkv_ref[...]
    cw = cw_ref[...]
    nbuf = QK_CONV - 1
    acc = cw[nbuf:nbuf + 1, :] * x
    for j in range(nbuf):
        acc = acc + cw[j:j + 1, :] * st_ref[:, j * CONV_DIM:(j + 1) * CONV_DIM]
    nst_ref[:, 0:(nbuf - 1) * CONV_DIM] = st_ref[:, CONV_DIM:nbuf * CONV_DIM]
    nst_ref[:, (nbuf - 1) * CONV_DIM:nbuf * CONV_DIM] = x
    qs, ks, vs = _qkv_act(acc)
    ab = ab_ref[...]
    g = -jnp.exp(alog_ref[...]) * jax.nn.softplus(ab + dtb_ref[...])
    beta = jax.nn.sigmoid(ab)
    eg = jnp.exp(g)
    z = z_ref[...]
    outs = []
    for h in range(N_HEADS):
        bh = beta[:, N_HEADS + h:N_HEADS + h + 1]
        egh = eg[:, h:h + 1]
        kb = ks[h] * bh
        w = kb * egh
        qg = qs[h] * egh
        u = vs[h] * bh
        qk = jnp.sum(qs[h] * ks[h], axis=-1, keepdims=True)
        o_rows = []
        for s in range(DEC_BLK):
            st = s_ref[s, h]
            wq = _bdot(jnp.concatenate([w[s:s + 1], qg[s:s + 1]], axis=0), st)
            vnew = u[s:s + 1] - wq[0:1]
            o_rows.append(wq[1:2] + qk[s:s + 1] * vnew)
            ns_ref[s, h] = st * egh[s:s + 1] + lax.dot_general(
                ks[h][s:s + 1].astype(BF16), vnew.astype(BF16), (((0,), (0,)), ((), ())),
                preferred_element_type=F32)
        o = jnp.concatenate(o_rows, axis=0)
        outs.append(_gate_norm(o, ong_ref[...], z[:, h * HEAD_V:(h + 1) * HEAD_V]))
    yb_ref[...] = _bdot(jnp.concatenate(outs, axis=1), wout_ref[...])


def _gdn_sample(qkv, z, ab, st, state_delta, layer, cw, alog, dtb, ong, wout):
    nbuf = QK_CONV - 1
    blk0 = N_PROMPT // DEC_BLK
    rows = lambda w: pl.BlockSpec((DEC_BLK, w), lambda i: (blk0 + i, 0))
    vec = pl.BlockSpec((1, 128), lambda i: (0, 0))
    sblk = pl.BlockSpec((DEC_BLK, N_HEADS, HEAD_K, HEAD_V), lambda i: (i, 0, 0, 0))
    return pl.pallas_call(
        _gdn_sample_kernel,
        grid=(DEC_BATCH // DEC_BLK,),
        in_specs=[rows(CONV_DIM), rows(D_DELTA_V), rows(128),
                  pl.BlockSpec((DEC_BLK, nbuf * CONV_DIM), lambda i: (i, 0)),
                  pl.BlockSpec((None, DEC_BLK, N_HEADS, HEAD_K, HEAD_V),
                               lambda i: (layer, i, 0, 0, 0)),
                  pl.BlockSpec((QK_CONV, CONV_DIM), lambda i: (0, 0)),
                  vec, vec, vec,
                  pl.BlockSpec((D_DELTA_V, D_MODEL), lambda i: (0, 0))],
        out_specs=[pl.BlockSpec((DEC_BLK, D_MODEL), lambda i: (i, 0)),
                   pl.BlockSpec((DEC_BLK, nbuf * CONV_DIM), lambda i: (i, 0)),
                   sblk],
        out_shape=[jax.ShapeDtypeStruct((DEC_BATCH, D_MODEL), F32),
                   jax.ShapeDtypeStruct((DEC_BATCH, nbuf * CONV_DIM), F32),
                   jax.ShapeDtypeStruct((DEC_BATCH, N_HEADS, HEAD_K, HEAD_V), F32)],
        compiler_params=pltpu.CompilerParams(dimension_semantics=("arbitrary",)),
        name="gdn_sample",
    )(qkv, z, ab, st, state_delta, cw, alog, dtb, ong, wout)


def _router_kernel(x_ref, yb_ref, gb_ref, wo_ref, g_ref, r_ref, xo_ref, h_ref, info_ref, cnt_ref):
    x = _gated_branch(x_ref[...], gb_ref, yb_ref[...], wo_ref)
    xo_ref[...] = x
    h = _rms(x, g_ref[...])
    h_ref[...] = h
    logits = _hdot(h, r_ref[...])
    lane = lax.broadcasted_iota(jnp.int32, logits.shape, 1)
    neg = jnp.float32(-jnp.inf)
    logits = jnp.where(lane < N_EXPERTS, logits, neg)
    m1 = jnp.max(logits, axis=-1, keepdims=True)
    i1 = jnp.min(jnp.where(logits == m1, lane, 128), axis=-1, keepdims=True)
    rest = jnp.where(lane == i1, neg, logits)
    m2 = jnp.max(rest, axis=-1, keepdims=True)
    i2 = jnp.min(jnp.where(rest == m2, lane, 128), axis=-1, keepdims=True)
    e = jnp.exp(m2 - m1)
    p1 = 1.0 / (1.0 + e)
    p2 = e / (1.0 + e)
    oh1 = jnp.where(lane == i1, 1.0, 0.0).astype(F32)
    oh2 = jnp.where(lane == i2, 1.0, 0.0).astype(F32)
    tr = lax.broadcasted_iota(jnp.int32, (TOK_TILE, TOK_TILE), 0)
    tc = lax.broadcasted_iota(jnp.int32, (TOK_TILE, TOK_TILE), 1)
    cum = _bdot(jnp.where(tr >= tc, 1.0, 0.0), oh1 + oh2)
    rank = cum - (oh1 + oh2)
    r1 = jnp.sum(oh1 * rank, axis=-1, keepdims=True)
    r2 = jnp.sum(oh2 * rank, axis=-1, keepdims=True)
    info = jnp.where(lane == 0, i1.astype(F32), 0.0)
    info = jnp.where(lane == 1, i2.astype(F32), info)
    info = jnp.where(lane == 2, p1, info)
    info = jnp.where(lane == 3, p2, info)
    info = jnp.where(lane == 4, r1, info)
    info = jnp.where(lane == 5, r2, info)
    info_ref[...] = info
    cnt_ref[...] = jnp.broadcast_to(cum[TOK_TILE - 1:TOK_TILE, :], (8, 128))


def _router(x, yb, gb, wo, g, r_pad):
    row = lambda w: pl.BlockSpec((TOK_TILE, w), lambda i: (i, 0))
    return pl.pallas_call(
        _router_kernel,
        grid=(N_TOK // TOK_TILE,),
        in_specs=[row(D_MODEL), _halves_tile_spec(), row(D_MODEL),
                  pl.BlockSpec((D_MODEL, D_MODEL), lambda i: (0, 0)),
                  pl.BlockSpec((1, D_MODEL), lambda i: (0, 0)),
                  pl.BlockSpec((D_MODEL, 128), lambda i: (0, 0))],
        out_specs=[row(D_MODEL), row(D_MODEL), row(128),
                   pl.BlockSpec((None, 8, 128), lambda i: (i, 0, 0))],
        out_shape=[jax.ShapeDtypeStruct((N_TOK, D_MODEL), F32),
                   jax.ShapeDtypeStruct((N_TOK, D_MODEL), F32),
                   jax.ShapeDtypeStruct((N_TOK, 128), F32),
                   jax.ShapeDtypeStruct((N_TOK // TOK_TILE, 8, 128), F32)],
        compiler_params=pltpu.CompilerParams(
            dimension_semantics=("arbitrary",), vmem_limit_bytes=VMEM_BIG),
        name="moe_router",
    )(x, yb, gb, wo, g, r_pad)


def _tile_rows_wait(src_ref, dst_ref, sem):
    pltpu.make_async_copy(src_ref.at[pl.ds(0, TOK_TILE), :], dst_ref, sem).wait()


def _dispatch_kernel(dest_ref, pad_ref, h_ref, xs_ref, zbuf, sem):
    base = pl.program_id(0) * (2 * TOK_TILE)

    @pl.when(pl.program_id(0) == 0)
    def _():
        zbuf[...] = jnp.zeros(zbuf.shape, F32)

        def zero_row(j, c):
            pltpu.make_async_copy(
                zbuf.at[pl.ds(0, 1), :], xs_ref.at[pl.ds(pad_ref[j], 1), :], sem).start()
            return c

        lax.fori_loop(0, MOE_PAD_ROWS, zero_row, 0, unroll=8)
        for _ in range(MOE_PAD_ROWS // TOK_TILE):
            _tile_rows_wait(h_ref, xs_ref.at[pl.ds(0, TOK_TILE), :], sem)

    def start(r, c):
        for k in range(2):
            pltpu.make_async_copy(
                h_ref.at[pl.ds(r, 1), :],
                xs_ref.at[pl.ds(dest_ref[base + 2 * r + k], 1), :], sem).start()
        return c

    lax.fori_loop(0, TOK_TILE, start, 0, unroll=8)
    for _ in range(2):
        _tile_rows_wait(h_ref, xs_ref.at[pl.ds(0, TOK_TILE), :], sem)


def _dispatch(dest, pad_rows, h):
    return pl.pallas_call(
        _dispatch_kernel,
        grid_spec=pltpu.PrefetchScalarGridSpec(
            num_scalar_prefetch=2,
            grid=(N_TOK // TOK_TILE,),
            in_specs=[pl.BlockSpec((TOK_TILE, D_MODEL), lambda i, d, p: (i, 0))],
            out_specs=pl.BlockSpec(memory_space=pl.ANY),
            scratch_shapes=[pltpu.VMEM((8, D_MODEL), F32), pltpu.SemaphoreType.DMA(())]),
        out_shape=jax.ShapeDtypeStruct((MOE_ROWS, D_MODEL), F32),
        compiler_params=pltpu.CompilerParams(dimension_semantics=("arbitrary",)),
        name="moe_dispatch",
    )(dest, pad_rows, h)


def _experts_kernel(te_ref, tv_ref, x_ref, wg_ref, wu_ref, wd_ref, o_ref):
    i = pl.program_id(0)

    @pl.when(tv_ref[i] > 0)
    def _():
        o_ref[...] = _swiglu_rows(x_ref[...].astype(BF16), wg_ref, wu_ref, wd_ref)

    @pl.when(tv_ref[i] == 0)
    def _():
        o_ref[...] = jnp.zeros((MOE_TILE, D_MODEL), F32)


def _experts(tile_expert, tile_valid, xs, wg, wu, wd):
    row = pl.BlockSpec((MOE_TILE, D_MODEL), lambda i, te, tv: (i, 0))
    return pl.pallas_call(
        _experts_kernel,
        grid_spec=pltpu.PrefetchScalarGridSpec(
            num_scalar_prefetch=2,
            grid=(MOE_ROWS // MOE_TILE,),
            in_specs=[row,
                      pl.BlockSpec((None, D_MODEL, D_FF), lambda i, te, tv: (te[i], 0, 0)),
                      pl.BlockSpec((None, D_MODEL, D_FF), lambda i, te, tv: (te[i], 0, 0)),
                      pl.BlockSpec((None, D_FF, D_MODEL), lambda i, te, tv: (te[i], 0, 0))],
            out_specs=row),
        out_shape=jax.ShapeDtypeStruct((MOE_ROWS, D_MODEL), F32),
        compiler_params=pltpu.CompilerParams(
            dimension_semantics=("arbitrary",), vmem_limit_bytes=VMEM_BIG),
        name="moe_experts",
    )(tile_expert, tile_valid, xs, wg, wu, wd)


def _combine_kernel(dest_ref, x_ref, info_ref, g_ref, ys_ref, yp_ref, ysm_ref, ybuf, sem):
    i = pl.program_id(0)
    n = pl.num_programs(0)

    def start_tile(tile):
        slot = tile % 2
        base = tile * (2 * TOK_TILE)

        def start(r, c):
            for k in range(2):
                pltpu.make_async_copy(
                    ys_ref.at[pl.ds(dest_ref[base + 2 * r + k], 1), :],
                    ybuf.at[slot, k, pl.ds(r, 1), :], sem.at[slot]).start()
            return c

        lax.fori_loop(0, TOK_TILE, start, 0, unroll=8)

    @pl.when(i == 0)
    def _():
        start_tile(i)

    @pl.when(i + 1 < n)
    def _():
        start_tile(i + 1)

    slot = i % 2
    for k in range(2):
        _tile_rows_wait(ys_ref, ybuf.at[slot, k], sem.at[slot])
    info = info_ref[...]
    x = x_ref[...] + info[:, 2:3] * ybuf[slot, 0] + info[:, 3:4] * ybuf[slot, 1]
    y = _rms(x, g_ref[...])

    @pl.when(i < PROMPT_TILES)
    def _():
        yp_ref[...] = y

    @pl.when(i == PROMPT_TILES)
    def _():
        ysm_ref[...] = y[0:DEC_BATCH, :]


def _combine(dest, x, info, g, ys):
    last = PROMPT_TILES - 1
    return pl.pallas_call(
        _combine_kernel,
        grid_spec=pltpu.PrefetchScalarGridSpec(
            num_scalar_prefetch=1,
            grid=(N_TOK // TOK_TILE,),
            in_specs=[pl.BlockSpec((TOK_TILE, D_MODEL), lambda i, d: (i, 0)),
                      pl.BlockSpec((TOK_TILE, 128), lambda i, d: (i, 0)),
                      pl.BlockSpec((1, D_MODEL), lambda i, d: (0, 0)),
                      pl.BlockSpec(memory_space=pl.ANY)],
            out_specs=[pl.BlockSpec((TOK_TILE, D_MODEL), lambda i, d: (jnp.minimum(i, last), 0)),
                       pl.BlockSpec((DEC_BATCH, D_MODEL), lambda i, d: (0, 0))],
            scratch_shapes=[pltpu.VMEM((2, 2, TOK_TILE, D_MODEL), F32),
                            pltpu.SemaphoreType.DMA((2,))]),
        out_shape=[jax.ShapeDtypeStruct((N_PROMPT, D_MODEL), F32),
                   jax.ShapeDtypeStruct((DEC_BATCH, D_MODEL), F32)],
        compiler_params=pltpu.CompilerParams(dimension_semantics=("arbitrary",)),
        name="moe_combine",
    )(dest, x, info, g, ys)


def _moe_layer(x, yb, gb, wo, g2, router_w, wg, wu, wd, final_g):
    r_pad = jnp.zeros((D_MODEL, 128), F32).at[:, :N_EXPERTS].set(router_w)
    x, h, info, cnt = _router(x, yb, gb, wo, g2, r_pad)
    n_tiles = N_TOK // TOK_TILE
    counts = cnt[:, 0, :N_EXPERTS].astype(jnp.int32)
    total = jnp.sum(counts, axis=0)
    padded = ((total + MOE_TILE - 1) // MOE_TILE) * MOE_TILE
    ends = jnp.cumsum(padded)
    starts = ends - padded
    base = starts[None, :] + jnp.cumsum(counts, axis=0) - counts
    choice = info[:, 0:2].astype(jnp.int32).reshape(n_tiles, TOK_TILE, 2)
    rank = info[:, 4:6].astype(jnp.int32).reshape(n_tiles, TOK_TILE, 2)
    onehot = choice[..., None] == jnp.arange(N_EXPERTS, dtype=jnp.int32)
    dest = (jnp.sum(jnp.where(onehot, base[:, None, None, :], 0), axis=-1) + rank).reshape(-1)
    tile_start = jnp.arange(MOE_ROWS // MOE_TILE, dtype=jnp.int32) * MOE_TILE
    tile_expert = jnp.minimum(
        jnp.sum((tile_start[:, None] >= ends[None, :]).astype(jnp.int32), axis=1), N_EXPERTS - 1)
    tile_valid = (tile_start < ends[-1]).astype(jnp.int32)
    last_used = jnp.max(jnp.where(tile_valid > 0, tile_expert, 0))
    tile_expert = jnp.where(tile_valid > 0, tile_expert, last_used).astype(jnp.int32)

    gaps = padded - total
    gap_end = jnp.cumsum(gaps)
    j = jnp.arange(MOE_PAD_ROWS, dtype=jnp.int32)
    seg = jnp.sum((j[:, None] >= gap_end[None, :]).astype(jnp.int32), axis=1)
    seg_e = jnp.minimum(seg, N_EXPERTS - 1)
    in_gap = (starts + total)[seg_e] + j - (gap_end - gaps)[seg_e]
    pad_rows = jnp.where(seg < N_EXPERTS, in_gap, ends[-1] + j - gap_end[-1]).astype(jnp.int32)

    xs = _dispatch(dest, pad_rows, h)
    ys = _experts(tile_expert, tile_valid, xs, wg, wu, wd)
    return _combine(dest, x, info, final_g, ys)


def _rearranged_w_in(w_in):
    off_qkv = 2 * D_CONV
    off_z = off_qkv + CONV_DIM
    off_a = off_z + D_DELTA_V
    off_ga = off_a + 2 * N_HEADS
    off_gb = off_ga + D_MODEL
    w_in = w_in.astype(BF16)
    return jnp.concatenate(
        [w_in[:, :off_a], w_in[:, off_ga:off_gb], w_in[:, off_gb:], w_in[:, off_a:off_ga],
         jnp.zeros((D_MODEL, 128 - 2 * N_HEADS), BF16)], axis=1)


def _lane_vec(v, offset=0):
    return jnp.zeros((1, 128), F32).at[0, offset:offset + v.shape[0]].set(v)


def kernel(x_prompt, x_sample, state_delta, state_qkv_conv, state_dwconv, norm1_g, w_in, qkv_conv_w, a_log, dt_bias, o_norm_g, w_delta_out, dw_w, dw_b, ln_g, ln_b, w_conv_out, w_o, norm2_g, ffn_w_gate, ffn_w_up, ffn_w_down, router_w, exp_w_gate, exp_w_up, exp_w_down, final_norm_g):
    x = (x_prompt.reshape(N_PROMPT, D_MODEL), x_sample.reshape(DEC_BATCH, D_MODEL))
    s_p, q_p, d_p, s_s, q_s, d_s = [], [], [], [], [], []
    for l in range(DEPTH):
        glu, qkv, z, ga, gb, ab = _norm_proj(x, norm1_g[l][None, :], _rearranged_w_in(w_in[l]))
        dwb, lng, lnb = dw_b[l][None, :], ln_g[l][None, :], ln_b[l][None, :]
        wco = w_conv_out[l].astype(BF16)
        ya_s, dst_s = _conv_sample(glu, state_dwconv[l].reshape(DEC_BATCH, -1), dw_w[l], dwb, lng, lnb, wco)
        wo = w_o[l].astype(BF16)
        x, dst_p = _conv_prompt(x, glu, ga, ya_s, dw_w[l], dwb, lng, lnb, wco, wo)
        alog, dtb, ong = _lane_vec(a_log[l]), _lane_vec(dt_bias[l]), o_norm_g[l][None, :]
        wdo = w_delta_out[l].astype(BF16)
        yb_s, qst_s, sst_s = _gdn_sample(qkv, z, ab, state_qkv_conv[l].reshape(DEC_BATCH, -1),
                                         state_delta, l, qkv_conv_w[l], alog, dtb, ong, wdo)
        rp, qg, kt, dl, m, aq, qst_p = _gdn_pre(qkv, ab, qkv_conv_w[l], alog, dtb)
        yb, sst_p = _gdn_scan(rp, qg, kt, dl, _gdn_solve(m), aq, z, yb_s, ong, wdo)
        i = l // 2
        if l % 2 == 0:
            x = _ffn(x, yb, gb, wo, norm2_g[l][None, :], ffn_w_gate[i].astype(BF16),
                     ffn_w_up[i].astype(BF16), ffn_w_down[i].astype(BF16))
        else:
            y_prompt, y_sample = _moe_layer(
                x, yb, gb, wo, norm2_g[l][None, :], router_w[i], exp_w_gate[i].astype(BF16),
                exp_w_up[i].astype(BF16), exp_w_down[i].astype(BF16), final_norm_g[None, :])
        s_p.append(sst_p)
        q_p.append(qst_p)
        d_p.append(dst_p)
        s_s.append(sst_s)
        q_s.append(qst_s.reshape(DEC_BATCH, QK_CONV - 1, CONV_DIM))
        d_s.append(dst_s.reshape(DEC_BATCH, DW_WIDTH - 1, D_CONV))
    return (y_prompt.reshape(BATCH, SEQ, D_MODEL), y_sample.reshape(DEC_BATCH, 1, D_MODEL), jnp.stack(s_p), jnp.stack(q_p), jnp.stack(d_p),
            jnp.stack(s_s), jnp.stack(q_s), jnp.stack(d_s))
```

```python
import functools

import jax
import jax.numpy as jnp
from jax import lax
from jax.experimental import pallas as pl
from jax.experimental.pallas import tpu as pltpu

F32 = jnp.float32
BF16 = jnp.bfloat16
HIGHEST = lax.Precision.HIGHEST

D_MODEL = 1024
BATCH = 8
SEQ = 2048
DEPTH = 2
DEC_BATCH = 128
D_CONV = 512
DW_WIDTH = 31
N_HEADS = 4
HEAD_K = 128
HEAD_V = 128
QK_CONV = 4
D_DELTA_K = N_HEADS * HEAD_K
D_DELTA_V = N_HEADS * HEAD_V
CONV_DIM = 2 * D_DELTA_K + D_DELTA_V
D_FF = 2816
N_EXPERTS = 8
EPS = 1e-6

N_PROMPT = BATCH * SEQ
TOK_TILE = 512
SEQ_TILE = TOK_TILE
N_TOK = N_PROMPT + TOK_TILE
SEQ_TILES = SEQ // SEQ_TILE
PROMPT_TILES = N_PROMPT // TOK_TILE
HALF_TILES = PROMPT_TILES // 2
HALF_SEQS = BATCH // 2
CHUNK = 64
M_PITCH = 68
SLABS_TILE = 2 * (SEQ_TILE // CHUNK)
M_ROWS_TILE = SLABS_TILE * M_PITCH
SOLVE_SLABS = 128
ROW_BLK = 64
MXU_DIM = 256
FF_CHUNKS = ((0, 6 * MXU_DIM), (6 * MXU_DIM, 5 * MXU_DIM))
MOE_TILE = 256
MOE_ROWS = ((2 * N_TOK + N_EXPERTS * (MOE_TILE - 1)) // MOE_TILE + 1) * MOE_TILE
MOE_PAD_ROWS = MOE_ROWS - 2 * N_TOK
assert MOE_PAD_ROWS % TOK_TILE == 0
DEC_BLK = 8

C_GLU, C_QKV, C_Z, C_GA, C_GB, C_AB = 0, 1024, 2560, 3072, 4096, 5120
N_PROJ = 5248

VMEM_BIG = 56 * 1024 * 1024


def _bdot(a, b):
    return jnp.dot(a.astype(BF16), b.astype(BF16), preferred_element_type=F32)


def _hdot(a, b):
    return jnp.dot(a, b, preferred_element_type=F32, precision=HIGHEST)


def _dot3(a, b):
    a_hi, b_hi = a.astype(BF16), b.astype(BF16)
    a_lo = (a - a_hi.astype(F32)).astype(BF16)
    b_lo = (b - b_hi.astype(F32)).astype(BF16)
    dot = functools.partial(jnp.dot, preferred_element_type=F32)
    return dot(a_hi, b_hi) + (dot(a_hi, b_lo) + dot(a_lo, b_hi))


def _rms(x, g):
    return x * lax.rsqrt(jnp.mean(x * x, axis=-1, keepdims=True) + EPS) * g


def _silu(x):
    return x * jax.nn.sigmoid(x)


def _resident(shape):
    return pl.BlockSpec(shape, lambda *_: (0,) * len(shape), pipeline_mode=pl.Buffered(1))


def _stream_x_specs(x):
    if isinstance(x, tuple):
        last = PROMPT_TILES - 1
        return ([pl.BlockSpec((TOK_TILE, D_MODEL), lambda i: (jnp.minimum(i, last), 0)),
                 pl.BlockSpec((DEC_BATCH, D_MODEL), lambda i: (0, 0))], list(x))
    return [pl.BlockSpec((TOK_TILE, D_MODEL), lambda i: (i, 0))], [x]


def _stream_x_tile(x_refs, xbuf):
    if len(x_refs) == 1:
        return x_refs[0][...]
    xp_ref, xs_ref = x_refs
    i = pl.program_id(0)

    @pl.when(i < PROMPT_TILES)
    def _():
        xbuf[...] = xp_ref[...]

    @pl.when(i == PROMPT_TILES)
    def _():
        xbuf[0:DEC_BATCH, :] = xs_ref[...]
        xbuf[DEC_BATCH:, :] = jnp.zeros((TOK_TILE - DEC_BATCH, D_MODEL), F32)

    return xbuf[...]


def _norm_proj_kernel(n_x, *refs):
    x_refs = refs[:n_x]
    g_ref, w_ref, glu_ref, qkv_ref, z_ref, ga_ref, gb_ref, ab_ref, xbuf = refs[n_x:]
    h = _rms(_stream_x_tile(x_refs, xbuf), g_ref[...]).astype(BF16)

    def proj(lo, hi):
        return jnp.dot(h, w_ref[:, lo:hi], preferred_element_type=F32)

    glu_ref[...] = proj(C_GLU, C_QKV)
    qkv_ref[...] = proj(C_QKV, C_Z)
    z_ref[...] = proj(C_Z, C_GA)
    ga_ref[...] = proj(C_GA, C_GB)
    gb_ref[...] = proj(C_GB, C_AB)
    ab_ref[...] = proj(C_AB, N_PROJ)


def _norm_proj(x, g, w_all):
    row = lambda w: pl.BlockSpec((TOK_TILE, w), lambda i: (i, 0))
    widths = (1024, CONV_DIM, D_DELTA_V, D_MODEL, D_MODEL, 128)
    x_specs, x_ops = _stream_x_specs(x)
    return pl.pallas_call(
        functools.partial(_norm_proj_kernel, len(x_ops)),
        grid=(N_TOK // TOK_TILE,),
        in_specs=x_specs + [pl.BlockSpec((1, D_MODEL), lambda i: (0, 0)),
                            _resident((D_MODEL, N_PROJ))],
        out_specs=[row(w) for w in widths],
        out_shape=[jax.ShapeDtypeStruct((N_TOK, w), F32) for w in widths],
        scratch_shapes=[pltpu.VMEM((TOK_TILE, D_MODEL), F32)],
        compiler_params=pltpu.CompilerParams(
            dimension_semantics=("arbitrary",), vmem_limit_bytes=VMEM_BIG),
        name="norm_proj",
    )(*x_ops, g, w_all)


def _gated_branch(x, gate_ref, y, wo_ref):
    return x + _bdot(jax.nn.sigmoid(gate_ref[...]) * y, wo_ref[...])


def _halves_tile_spec():
    half = lambda i: i // HALF_TILES - i // (2 * HALF_TILES)
    return pl.BlockSpec((None, TOK_TILE, D_MODEL),
                        lambda i, *_: (half(i), i - HALF_TILES * half(i), 0))


def _swiglu_rows(h, wg_ref, wu_ref, wd_ref):
    acc = None
    for lo, n in FF_CHUNKS:
        a = jnp.dot(h, wg_ref[:, lo:lo + n], preferred_element_type=F32)
        b = jnp.dot(h, wu_ref[:, lo:lo + n], preferred_element_type=F32)
        part = jnp.dot((_silu(a) * b).astype(BF16), wd_ref[lo:lo + n, :],
                       preferred_element_type=F32)
        acc = part if acc is None else acc + part
    return acc


def _ffn_kernel(x_ref, yb_ref, gb_ref, wo_ref, g_ref, wg_ref, wu_ref, wd_ref, o_ref):
    x = _gated_branch(x_ref[...], gb_ref, yb_ref[...], wo_ref)
    h = _rms(x, g_ref[...]).astype(BF16)
    o_ref[...] = x + _swiglu_rows(h, wg_ref, wu_ref, wd_ref)


def _ffn(x, yb, gb, wo, g, wg, wu, wd):
    row = pl.BlockSpec((TOK_TILE, D_MODEL), lambda i: (i, 0))
    return pl.pallas_call(
        _ffn_kernel,
        grid=(N_TOK // TOK_TILE,),
        in_specs=[row, _halves_tile_spec(), row, _resident((D_MODEL, D_MODEL)),
                  pl.BlockSpec((1, D_MODEL), lambda i: (0, 0)),
                  _resident((D_MODEL, D_FF)), _resident((D_MODEL, D_FF)),
                  _resident((D_FF, D_MODEL))],
        out_specs=row,
        out_shape=jax.ShapeDtypeStruct((N_TOK, D_MODEL), F32),
        compiler_params=pltpu.CompilerParams(
            dimension_semantics=("arbitrary",), vmem_limit_bytes=VMEM_BIG),
        name="ffn_dense",
    )(x, yb, gb, wo, g, wg, wu, wd)


def _conv_tail(c, dwb, lng, lnb):
    c = c + dwb
    mu = jnp.mean(c, axis=-1, keepdims=True)
    var = jnp.mean(jnp.square(c - mu), axis=-1, keepdims=True)
    c = (c - mu) * lax.rsqrt(var + EPS) * lng + lnb
    return _silu(c)


def _sample_rows_tile(ys_ref, y_ref):
    y_ref[0:DEC_BATCH, :] = ys_ref[...]
    y_ref[DEC_BATCH:, :] = jnp.zeros((TOK_TILE - DEC_BATCH, D_MODEL), F32)


def _conv_prompt_kernel(n_x, *refs):
    x_refs = refs[:n_x]
    (glu_ref, ga_ref, ys_ref, dww_ref, dwb_ref, lng_ref, lnb_ref, wout_ref, wo_ref,
     o_ref, st_ref, ubuf, ushift, cbuf, xbuf) = refs[n_x:]
    i = pl.program_id(0)
    if n_x == 2:
        _stream_x_tile(x_refs, xbuf)
        x_ref = xbuf
    else:
        x_ref = x_refs[0]

    @pl.when(i < PROMPT_TILES)
    def _():
        _conv_prompt_tile(i % SEQ_TILES, glu_ref, dww_ref, dwb_ref, lng_ref, lnb_ref,
                          st_ref, ubuf, ushift, cbuf)
        ya = jnp.dot(cbuf[...], wout_ref[...], preferred_element_type=F32)
        o_ref[...] = _gated_branch(x_ref[...], ga_ref, ya, wo_ref)

    @pl.when(i == PROMPT_TILES)
    def _():
        ya = jnp.concatenate(
            [ys_ref[...], jnp.zeros((TOK_TILE - DEC_BATCH, D_MODEL), F32)], axis=0)
        o_ref[...] = _gated_branch(x_ref[...], ga_ref, ya, wo_ref)


def _conv_prompt_tile(t, glu_ref, dww_ref, dwb_ref, lng_ref, lnb_ref, st_ref, ubuf, ushift, cbuf):
    halo = 32

    @pl.when(t == 0)
    def _():
        ubuf[0:halo, :] = jnp.zeros((halo, D_CONV), F32)

    @pl.when(t > 0)
    def _():
        ubuf[0:halo, :] = ubuf[SEQ_TILE:SEQ_TILE + halo, :]

    glu = glu_ref[...]
    ubuf[halo:halo + SEQ_TILE, :] = glu[:, :D_CONV] * jax.nn.sigmoid(glu[:, D_CONV:])
    for r in range(1, 8):
        ushift[r - 1, 0:SEQ_TILE + halo - 8, :] = ubuf[r:r + SEQ_TILE + halo - 8, :]
    w = dww_ref[...]
    first = halo - (DW_WIDTH - 1)
    for rb in range(SEQ_TILE // ROW_BLK):
        r0 = rb * ROW_BLK
        acc = jnp.zeros((ROW_BLK, D_CONV), F32)
        for j in range(DW_WIDTH):
            a, r = divmod(first + j, 8)
            if r == 0:
                win = ubuf[r0 + 8 * a:r0 + 8 * a + ROW_BLK, :]
            else:
                win = ushift[r - 1, r0 + 8 * a:r0 + 8 * a + ROW_BLK, :]
            acc = acc + w[j:j + 1, :] * win
        c = _conv_tail(acc, dwb_ref[...], lng_ref[...], lnb_ref[...])
        cbuf[r0:r0 + ROW_BLK, :] = c.astype(BF16)

    @pl.when(t == SEQ_TILES - 1)
    def _():
        st_ref[...] = ubuf[halo + SEQ_TILE - (DW_WIDTH - 1):halo + SEQ_TILE, :]


def _seq_of_tile(i):
    return jnp.minimum(i // SEQ_TILES, BATCH - 1)


def _conv_prompt(x, glu, ga, ya_sample, dww, dwb, lng, lnb, wout, wo):
    vec = pl.BlockSpec((1, D_CONV), lambda i: (0, 0))
    row = pl.BlockSpec((SEQ_TILE, D_MODEL), lambda i: (i, 0))
    x_specs, x_ops = _stream_x_specs(x)
    return pl.pallas_call(
        functools.partial(_conv_prompt_kernel, len(x_ops)),
        grid=(PROMPT_TILES + 1,),
        in_specs=x_specs + [
            pl.BlockSpec((SEQ_TILE, 2 * D_CONV), lambda i: (i, 0)), row,
            pl.BlockSpec((DEC_BATCH, D_MODEL), lambda i: (0, 0)),
            pl.BlockSpec((DW_WIDTH, D_CONV), lambda i: (0, 0)),
            vec, vec, vec,
            pl.BlockSpec((D_CONV, D_MODEL), lambda i: (0, 0)),
            pl.BlockSpec((D_MODEL, D_MODEL), lambda i: (0, 0))],
        out_specs=[row,
                   pl.BlockSpec((None, DW_WIDTH - 1, D_CONV), lambda i: (_seq_of_tile(i), 0, 0))],
        out_shape=[jax.ShapeDtypeStruct((N_TOK, D_MODEL), F32),
                   jax.ShapeDtypeStruct((BATCH, DW_WIDTH - 1, D_CONV), F32)],
        scratch_shapes=[pltpu.VMEM((SEQ_TILE + 32, D_CONV), F32),
                        pltpu.VMEM((7, SEQ_TILE + 24, D_CONV), F32),
                        pltpu.VMEM((SEQ_TILE, D_CONV), BF16),
                        pltpu.VMEM((TOK_TILE, D_MODEL), F32)],
        compiler_params=pltpu.CompilerParams(
            dimension_semantics=("arbitrary",), vmem_limit_bytes=VMEM_BIG),
        name="conv_prompt",
    )(*x_ops, glu, ga, ya_sample, dww, dwb, lng, lnb, wout, wo)


def _conv_sample_kernel(glu_ref, st_ref, dww_ref, dwb_ref, lng_ref, lnb_ref, wout_ref,
                        ya_ref, nst_ref):
    glu = glu_ref[...]
    u = glu[:, :D_CONV] * jax.nn.sigmoid(glu[:, D_CONV:])
    w = dww_ref[...]
    nbuf = DW_WIDTH - 1
    acc = w[nbuf:nbuf + 1, :] * u
    for j in range(nbuf):
        acc = acc + w[j:j + 1, :] * st_ref[:, j * D_CONV:(j + 1) * D_CONV]
    c = _conv_tail(acc, dwb_ref[...], lng_ref[...], lnb_ref[...])
    ya_ref[...] = _bdot(c, wout_ref[...])
    nst_ref[:, 0:(nbuf - 1) * D_CONV] = st_ref[:, D_CONV:nbuf * D_CONV]
    nst_ref[:, (nbuf - 1) * D_CONV:nbuf * D_CONV] = u


def _conv_sample(glu, st, dww, dwb, lng, lnb, wout):
    nbuf = DW_WIDTH - 1
    blk = N_PROMPT // DEC_BATCH
    vec = pl.BlockSpec((1, D_CONV), lambda i: (0, 0))
    return pl.pallas_call(
        _conv_sample_kernel,
        grid=(1,),
        in_specs=[pl.BlockSpec((DEC_BATCH, 2 * D_CONV), lambda i: (blk, 0)),
                  pl.BlockSpec((DEC_BATCH, nbuf * D_CONV), lambda i: (0, 0)),
                  pl.BlockSpec((DW_WIDTH, D_CONV), lambda i: (0, 0)),
                  vec, vec, vec,
                  pl.BlockSpec((D_CONV, D_MODEL), lambda i: (0, 0))],
        out_specs=[pl.BlockSpec((DEC_BATCH, D_MODEL), lambda i: (0, 0)),
                   pl.BlockSpec((DEC_BATCH, nbuf * D_CONV), lambda i: (0, 0))],
        out_shape=[jax.ShapeDtypeStruct((DEC_BATCH, D_MODEL), F32),
                   jax.ShapeDtypeStruct((DEC_BATCH, nbuf * D_CONV), F32)],
        compiler_params=pltpu.CompilerParams(dimension_semantics=("arbitrary",)),
        name="conv_sample",
    )(glu, st, dww, dwb, lng, lnb, wout)


def _qkv_act(c):
    c = _silu(c)
    qs, ks, vs = [], [], []
    for h in range(N_HEADS):
        q = c[:, h * HEAD_K:(h + 1) * HEAD_K]
        k = c[:, D_DELTA_K + h * HEAD_K:D_DELTA_K + (h + 1) * HEAD_K]
        qs.append(q * lax.rsqrt(jnp.sum(q * q, axis=-1, keepdims=True) + EPS) * (HEAD_K ** -0.5))
        ks.append(k * lax.rsqrt(jnp.sum(k * k, axis=-1, keepdims=True) + EPS))
        vs.append(c[:, 2 * D_DELTA_K + h * HEAD_V:2 * D_DELTA_K + (h + 1) * HEAD_V])
    return qs, ks, vs


def _gate_norm(o, ong, z):
    o = o * lax.rsqrt(jnp.mean(o * o, axis=-1, keepdims=True) + EPS) * ong
    return o * _silu(z)


def _gdn_pre_kernel(qkv_ref, ab_ref, cw_ref, alog_ref, dtb_ref,
                    rp_ref, qg_ref, kt_ref, dl_ref, m_ref, aq_ref, sq_ref, xbuf):
    t = pl.program_id(0) % SEQ_TILES
    halo = 8

    @pl.when(t == 0)
    def _():
        xbuf[0:halo, :] = jnp.zeros((halo, CONV_DIM), F32)

    @pl.when(t > 0)
    def _():
        xbuf[0:halo, :] = xbuf[SEQ_TILE:SEQ_TILE + halo, :]

    xbuf[halo:halo + SEQ_TILE, :] = qkv_ref[...]

    @pl.when(t == SEQ_TILES - 1)
    def _():
        sq_ref[...] = xbuf[halo + SEQ_TILE - (QK_CONV - 1):halo + SEQ_TILE, :]

    ab = ab_ref[...]
    lane = lax.broadcasted_iota(jnp.int32, (SEQ_TILE, 128), 1)
    gb = jnp.where(lane < N_HEADS,
                   -jnp.exp(alog_ref[...]) * jax.nn.softplus(ab + dtb_ref[...]),
                   jax.nn.sigmoid(ab))
    cw = cw_ref[...]
    first = halo - (QK_CONV - 1)
    ri = lax.broadcasted_iota(jnp.int32, (CHUNK, 128), 0)
    li = lax.broadcasted_iota(jnp.int32, (CHUNK, 128), 1)
    left = li < CHUNK
    jj = jnp.where(left, li, li - CHUNK)
    strict = ri > jj
    tril = ri >= jj
    r64 = lax.broadcasted_iota(jnp.int32, (CHUNK, CHUNK), 0)
    c64 = lax.broadcasted_iota(jnp.int32, (CHUNK, CHUNK), 1)
    csum = jnp.where(r64 >= c64, 1.0, 0.0).astype(F32)
    for c in range(SEQ_TILE // CHUNK):
        r0 = c * CHUNK
        acc = cw[0:1, :] * xbuf[r0 + first:r0 + first + CHUNK, :]
        for j in range(1, QK_CONV):
            acc = acc + cw[j:j + 1, :] * xbuf[r0 + first + j:r0 + first + j + CHUNK, :]
        qs, ks, vs = _qkv_act(acc)
        gbc = gb[r0:r0 + CHUNK, :]
        gc = _hdot(csum, gbc)
        gct = gc.T
        eg = jnp.exp(gc)
        glast = gc[CHUNK - 1:CHUNK, :]
        kdec = jnp.exp(glast - gc)
        dl = jnp.exp(glast)
        for hp in range(N_HEADS // 2):
            h0, h1 = 2 * hp, 2 * hp + 1
            slab = 2 * c + hp
            kb, rhs, qg, kd = [], [], [], []
            for h in (h0, h1):
                beta = gbc[:, N_HEADS + h:N_HEADS + h + 1]
                egh = eg[:, h:h + 1]
                kb.append(ks[h] * beta)
                rhs.append(jnp.concatenate([vs[h] * beta, kb[-1] * egh], axis=1))
                qg.append(qs[h] * egh)
                kd.append(ks[h] * kdec[:, h:h + 1])
            rows = slice(slab * 2 * CHUNK, (slab + 1) * 2 * CHUNK)
            rp_ref[rows, :] = jnp.concatenate(rhs, axis=0).astype(BF16)
            qg_ref[rows, :] = jnp.concatenate(qg, axis=0).astype(BF16)
            kt_ref[rows, :] = jnp.concatenate(kd, axis=0).T.astype(BF16)
            dl_ref[slab * 8:(slab + 1) * 8, :] = jnp.concatenate(
                [jnp.broadcast_to(dl[:, h0:h0 + 1], (8, HEAD_V)),
                 jnp.broadcast_to(dl[:, h1:h1 + 1], (8, HEAD_V))], axis=1)
            lhs = jnp.concatenate(kb + [qs[h0], qs[h1]], axis=0)
            keys = jnp.concatenate([ks[h0], ks[h1]], axis=0)
            prod = lax.dot_general(lhs.astype(BF16), keys.astype(BF16), (((1,), (1,)), ((), ())),
                                   preferred_element_type=F32)
            kk = jnp.where(left, prod[0:CHUNK], prod[CHUNK:2 * CHUNK])
            qk = jnp.where(left, prod[2 * CHUNK:3 * CHUNK], prod[3 * CHUNK:4 * CHUNK])
            gcol = jnp.where(left, gc[:, h0:h0 + 1], gc[:, h1:h1 + 1])
            grow = jnp.concatenate([gct[h0:h0 + 1, :], gct[h1:h1 + 1, :]], axis=1)
            diff = gcol - grow
            m_ref[slab * M_PITCH:slab * M_PITCH + CHUNK, :] = jnp.where(
                strict, kk * jnp.exp(jnp.where(strict, diff, 0.0)), 0.0)
            m_ref[slab * M_PITCH + CHUNK:(slab + 1) * M_PITCH, :] = jnp.zeros(
                (M_PITCH - CHUNK, 128), F32)
            aq_ref[slab * CHUNK:(slab + 1) * CHUNK, :] = jnp.where(
                tril, qk * jnp.exp(jnp.where(tril, diff, 0.0)), 0.0).astype(BF16)


def _gdn_pre(qkv, ab, cw, alog, dtb):
    tile = lambda w: pl.BlockSpec((SEQ_TILE, w), lambda i: (i, 0))
    vec = pl.BlockSpec((1, 128), lambda i: (0, 0))
    slab_rows = SLABS_TILE * 2 * CHUNK
    per_tile = lambda rows, w: pl.BlockSpec((rows, w), lambda i: (i, 0))
    return pl.pallas_call(
        _gdn_pre_kernel,
        grid=(PROMPT_TILES,),
        in_specs=[tile(CONV_DIM), tile(128),
                  pl.BlockSpec((QK_CONV, CONV_DIM), lambda i: (0, 0)), vec, vec],
        out_specs=[per_tile(slab_rows, 2 * HEAD_V), per_tile(slab_rows, HEAD_K),
                   per_tile(slab_rows, 128), per_tile(SLABS_TILE * 8, 2 * HEAD_V),
                   per_tile(M_ROWS_TILE, 128), per_tile(SLABS_TILE * CHUNK, 128),
                   pl.BlockSpec((None, QK_CONV - 1, CONV_DIM), lambda i: (i // SEQ_TILES, 0, 0))],
        out_shape=[jax.ShapeDtypeStruct((PROMPT_TILES * slab_rows, 2 * HEAD_V), BF16),
                   jax.ShapeDtypeStruct((PROMPT_TILES * slab_rows, HEAD_K), BF16),
                   jax.ShapeDtypeStruct((PROMPT_TILES * slab_rows, 128), BF16),
                   jax.ShapeDtypeStruct((PROMPT_TILES * SLABS_TILE * 8, 2 * HEAD_V), F32),
                   jax.ShapeDtypeStruct((PROMPT_TILES * M_ROWS_TILE, 128), F32),
                   jax.ShapeDtypeStruct((PROMPT_TILES * SLABS_TILE * CHUNK, 128), BF16),
                   jax.ShapeDtypeStruct((BATCH, QK_CONV - 1, CONV_DIM), F32)],
        scratch_shapes=[pltpu.VMEM((SEQ_TILE + 8, CONV_DIM), F32)],
        compiler_params=pltpu.CompilerParams(dimension_semantics=("arbitrary",)),
        name="gdn_pre",
    )(qkv, ab, cw, alog, dtb)


def _gdn_solve_kernel(m_ref, t_ref, mt, xt):
    def to_lanes(i, c):
        g = m_ref[pl.ds(i, SOLVE_SLABS, stride=M_PITCH), :]
        mt[pl.ds(pl.multiple_of(i * 128, 128), 128), :] = g.T
        return c

    lax.fori_loop(0, CHUNK, to_lanes, 0, unroll=4)

    sub = lax.broadcasted_iota(jnp.int32, (8, SOLVE_SLABS), 0)
    zeros8 = jnp.zeros((8, SOLVE_SLABS), F32)

    for ib in range(CHUNK // 8):
        def solve_row(i, c, ib=ib):
            base = pl.multiple_of(i * 128, 128)
            unit = jnp.where(sub + 8 * ib == i, 1.0, 0.0).astype(F32)
            acc = tuple([zeros8] * ib + [unit]) * 2

            def columns(groups):
                def step(j, acc):
                    xb = pl.multiple_of(j * 128, 128)
                    m0 = mt[pl.ds(base + j, 1), :]
                    m1 = mt[pl.ds(base + CHUNK + j, 1), :]
                    acc = list(acc)
                    for k in range(groups):
                        acc[k] = acc[k] - m0 * xt[pl.ds(xb + 8 * k, 8), :]
                        acc[ib + 1 + k] = acc[ib + 1 + k] - m1 * xt[pl.ds(xb + CHUNK + 8 * k, 8), :]
                    return tuple(acc)
                return step

            for jb in range(ib):
                acc = lax.fori_loop(8 * jb, 8 * jb + 8, columns(jb + 1), acc, unroll=True)
            acc = lax.fori_loop(8 * ib, i, columns(ib + 1), acc)
            for k in range(CHUNK // 8):
                xt[pl.ds(base + 8 * k, 8), :] = acc[k] if k <= ib else zeros8
                xt[pl.ds(base + CHUNK + 8 * k, 8), :] = acc[ib + 1 + k] if k <= ib else zeros8
            return c

        lax.fori_loop(8 * ib, 8 * ib + 8, solve_row, 0)

    def from_lanes(i, c):
        x = xt[pl.ds(pl.multiple_of(i * 128, 128), 128), :]
        t_ref[pl.ds(i, SOLVE_SLABS, stride=M_PITCH), :] = x.T
        return c

    lax.fori_loop(0, CHUNK, from_lanes, 0, unroll=4)
    for k in range(CHUNK, M_PITCH):
        t_ref[pl.ds(k, SOLVE_SLABS, stride=M_PITCH), :] = jnp.zeros((SOLVE_SLABS, 128), F32)


def _gdn_solve(m):
    rows = SOLVE_SLABS * M_PITCH
    return pl.pallas_call(
        _gdn_solve_kernel,
        grid=(PROMPT_TILES * SLABS_TILE // SOLVE_SLABS,),
        in_specs=[pl.BlockSpec((rows, 128), lambda i: (i, 0))],
        out_specs=pl.BlockSpec((rows, 128), lambda i: (i, 0)),
        out_shape=jax.ShapeDtypeStruct(m.shape, F32),
        scratch_shapes=[pltpu.VMEM((CHUNK * 128, SOLVE_SLABS), F32),
                        pltpu.VMEM((CHUNK * 128, SOLVE_SLABS), F32)],
        compiler_params=pltpu.CompilerParams(dimension_semantics=("arbitrary",)),
        name="gdn_solve",
    )(m)


def _gdn_scan_kernel(rp_ref, qg_ref, kt_ref, dl_ref, tm_ref, aq_ref, z0_ref, z1_ref, ys_ref,
                     ong_ref, wout_ref, yb_ref, ss_ref, obuf, s_scr):
    i = pl.program_id(0)
    pl.when(i < HALF_TILES)(functools.partial(
        _gdn_scan_tile, i % SEQ_TILES, rp_ref, qg_ref, kt_ref, dl_ref, tm_ref, aq_ref,
        (z0_ref, z1_ref), ong_ref, wout_ref, yb_ref, ss_ref, obuf, s_scr))

    @pl.when(i == HALF_TILES)
    def _():
        yb_ref[0] = jnp.zeros((TOK_TILE, D_MODEL), F32)
        _sample_rows_tile(ys_ref, yb_ref.at[1])


def _gdn_scan_tile(t, rp_ref, qg_ref, kt_ref, dl_ref, tm_ref, aq_ref, z_refs, ong_ref, wout_ref,
                   yb_ref, ss_ref, obuf, s_scr):
    @pl.when(t == 0)
    def _():
        s_scr[...] = jnp.zeros((2, N_HEADS // 2, HEAD_K, 2 * HEAD_V), F32)

    left = lax.broadcasted_iota(jnp.int32, (CHUNK, 128), 1) < CHUNK
    ong = ong_ref[...]
    zero = jnp.zeros((CHUNK, HEAD_V), BF16)

    def pair_diag(x):
        x = x.astype(BF16)
        none = jnp.zeros_like(x)
        return jnp.concatenate([jnp.where(left, x, none), jnp.where(left, none, x)], axis=0)

    for c in range(SEQ_TILE // CHUNK):
        r0 = c * CHUNK
        for half in range(2):
            for hp in range(N_HEADS // 2):
                slab = 2 * c + hp
                rows = slice(slab * 2 * CHUNK, (slab + 1) * 2 * CHUNK)
                tm = pair_diag(tm_ref[half, slab * M_PITCH:slab * M_PITCH + CHUNK, :])
                aq = pair_diag(aq_ref[half, slab * CHUNK:(slab + 1) * CHUNK, :])
                uw = jnp.dot(tm, rp_ref[half, rows, :], preferred_element_type=F32)
                s = s_scr[half, hp]
                lhs = jnp.concatenate([uw[:, HEAD_V:].astype(BF16), qg_ref[half, rows, :]], axis=0)
                wq = jnp.dot(lhs, s.astype(BF16), preferred_element_type=F32)
                ws = jnp.concatenate([wq[0:CHUNK, 0:HEAD_V], wq[CHUNK:2 * CHUNK, HEAD_V:]], axis=0)
                qs = jnp.concatenate([wq[2 * CHUNK:3 * CHUNK, 0:HEAD_V], wq[3 * CHUNK:, HEAD_V:]],
                                     axis=0)
                vnew = (uw[:, :HEAD_V] - ws).astype(BF16)
                vn_bd = jnp.concatenate(
                    [jnp.concatenate([vnew[0:CHUNK], zero], axis=1),
                     jnp.concatenate([zero, vnew[CHUNK:]], axis=1)], axis=0)
                s_scr[half, hp] = s * dl_ref[half, slab * 8:slab * 8 + 1, :] + jnp.dot(
                    kt_ref[half, rows, :], vn_bd, preferred_element_type=F32)
                o = qs + jnp.dot(aq, vnew, preferred_element_type=F32)
                for n in range(2):
                    h = 2 * hp + n
                    zh = z_refs[half][r0:r0 + CHUNK, h * HEAD_V:(h + 1) * HEAD_V]
                    obuf[half, r0:r0 + CHUNK, h * HEAD_V:(h + 1) * HEAD_V] = _gate_norm(
                        o[n * CHUNK:(n + 1) * CHUNK], ong, zh)

    for half in range(2):
        yb_ref[half] = _bdot(obuf[half], wout_ref[...])

    @pl.when(t == SEQ_TILES - 1)
    def _():
        for half in range(2):
            for h in range(N_HEADS):
                ss_ref[half, h] = s_scr[half, h // 2, :, (h % 2) * HEAD_V:(h % 2 + 1) * HEAD_V]


def _gdn_scan(rp, qg, kt, dl, tm, aq, z, yb_sample, ong, wout):
    last = HALF_TILES - 1
    halves = lambda a: a.reshape((2, a.shape[0] // 2) + a.shape[1:])
    both = lambda a: pl.BlockSpec((2, a.shape[0] // PROMPT_TILES, a.shape[1]),
                                  lambda i: (0, jnp.minimum(i, last), 0))
    ztile = lambda off: pl.BlockSpec((SEQ_TILE, D_DELTA_V),
                                     lambda i: (jnp.minimum(i, last) + off, 0))
    ops = (rp, qg, kt, dl, tm, aq)
    yb, ss = pl.pallas_call(
        _gdn_scan_kernel,
        grid=(HALF_TILES + 1,),
        in_specs=[both(a) for a in ops] + [
            ztile(0), ztile(HALF_TILES),
            pl.BlockSpec((DEC_BATCH, D_MODEL), lambda i: (0, 0)),
            pl.BlockSpec((1, 128), lambda i: (0, 0)),
            pl.BlockSpec((D_DELTA_V, D_MODEL), lambda i: (0, 0))],
        out_specs=[pl.BlockSpec((2, SEQ_TILE, D_MODEL), lambda i: (0, i, 0)),
                   pl.BlockSpec((2, None, N_HEADS, HEAD_K, HEAD_V),
                                lambda i: (0, jnp.minimum(i // SEQ_TILES, HALF_SEQS - 1), 0, 0, 0))],
        out_shape=[jax.ShapeDtypeStruct((2, (HALF_TILES + 1) * TOK_TILE, D_MODEL), F32),
                   jax.ShapeDtypeStruct((2, HALF_SEQS, N_HEADS, HEAD_K, HEAD_V), F32)],
        scratch_shapes=[pltpu.VMEM((2, SEQ_TILE, D_DELTA_V), F32),
                        pltpu.VMEM((2, N_HEADS // 2, HEAD_K, 2 * HEAD_V), F32)],
        compiler_params=pltpu.CompilerParams(
            dimension_semantics=("arbitrary",), vmem_limit_bytes=VMEM_BIG),
        name="gdn_scan",
    )(*[halves(a) for a in ops], z, z, yb_sample, ong, wout)
    return yb, ss.reshape(BATCH, N_HEADS, HEAD_K, HEAD_V)


def _gdn_sample_kernel(qkv_ref, z_ref, ab_ref, st_ref, s_ref, cw_ref, alog_ref, dtb_ref,
                       ong_ref, wout_ref, yb_ref, nst_ref, ns_ref):
    x = qkv_ref[...]
    cw = cw_ref[...]
    nbuf = QK_CONV - 1
    acc = cw[nbuf:nbuf + 1, :] * x
    for j in range(nbuf):
        acc = acc + cw[j:j + 1, :] * st_ref[:, j * CONV_DIM:(j + 1) * CONV_DIM]
    nst_ref[:, 0:(nbuf - 1) * CONV_DIM] = st_ref[:, CONV_DIM:nbuf * CONV_DIM]
    nst_ref[:, (nbuf - 1) * CONV_DIM:nbuf * CONV_DIM] = x
    qs, ks, vs = _qkv_act(acc)
    ab = ab_ref[...]
    g = -jnp.exp(alog_ref[...]) * jax.nn.softplus(ab + dtb_ref[...])
    beta = jax.nn.sigmoid(ab)
    eg = jnp.exp(g)
    z = z_ref[...]
    outs = []
    for h in range(N_HEADS):
        bh = beta[:, N_HEADS + h:N_HEADS + h + 1]
        egh = eg[:, h:h + 1]
        kb = ks[h] * bh
        w = kb * egh
        qg = qs[h] * egh
        u = vs[h] * bh
        qk = jnp.sum(qs[h] * ks[h], axis=-1, keepdims=True)
        o_rows = []
        for s in range(DEC_BLK):
            st = s_ref[s, h]
            wq = _bdot(jnp.concatenate([w[s:s + 1], qg[s:s + 1]], axis=0), st)
            vnew = u[s:s + 1] - wq[0:1]
            o_rows.append(wq[1:2] + qk[s:s + 1] * vnew)
            ns_ref[s, h] = st * egh[s:s + 1] + lax.dot_general(
                ks[h][s:s + 1].astype(BF16), vnew.astype(BF16), (((0,), (0,)), ((), ())),
                preferred_element_type=F32)
        o = jnp.concatenate(o_rows, axis=0)
        outs.append(_gate_norm(o, ong_ref[...], z[:, h * HEAD_V:(h + 1) * HEAD_V]))
    yb_ref[...] = _bdot(jnp.concatenate(outs, axis=1), wout_ref[...])


def _gdn_sample(qkv, z, ab, st, state_delta, layer, cw, alog, dtb, ong, wout):
    nbuf = QK_CONV - 1
    blk0 = N_PROMPT // DEC_BLK
    rows = lambda w: pl.BlockSpec((DEC_BLK, w), lambda i: (blk0 + i, 0))
    vec = pl.BlockSpec((1, 128), lambda i: (0, 0))
    sblk = pl.BlockSpec((DEC_BLK, N_HEADS, HEAD_K, HEAD_V), lambda i: (i, 0, 0, 0))
    return pl.pallas_call(
        _gdn_sample_kernel,
        grid=(DEC_BATCH // DEC_BLK,),
        in_specs=[rows(CONV_DIM), rows(D_DELTA_V), rows(128),
                  pl.BlockSpec((DEC_BLK, nbuf * CONV_DIM), lambda i: (i, 0)),
                  pl.BlockSpec((None, DEC_BLK, N_HEADS, HEAD_K, HEAD_V),
                               lambda i: (layer, i, 0, 0, 0)),
                  pl.BlockSpec((QK_CONV, CONV_DIM), lambda i: (0, 0)),
                  vec, vec, vec,
                  pl.BlockSpec((D_DELTA_V, D_MODEL), lambda i: (0, 0))],
        out_specs=[pl.BlockSpec((DEC_BLK, D_MODEL), lambda i: (i, 0)),
                   pl.BlockSpec((DEC_BLK, nbuf * CONV_DIM), lambda i: (i, 0)),
                   sblk],
        out_shape=[jax.ShapeDtypeStruct((DEC_BATCH, D_MODEL), F32),
                   jax.ShapeDtypeStruct((DEC_BATCH, nbuf * CONV_DIM), F32),
                   jax.ShapeDtypeStruct((DEC_BATCH, N_HEADS, HEAD_K, HEAD_V), F32)],
        compiler_params=pltpu.CompilerParams(dimension_semantics=("arbitrary",)),
        name="gdn_sample",
    )(qkv, z, ab, st, state_delta, cw, alog, dtb, ong, wout)


def _router_kernel(x_ref, yb_ref, gb_ref, wo_ref, g_ref, r_ref, xo_ref, h_ref, info_ref, cnt_ref):
    x = _gated_branch(x_ref[...], gb_ref, yb_ref[...], wo_ref)
    xo_ref[...] = x
    h = _rms(x, g_ref[...])
    h_ref[...] = h
    logits = _dot3(h, r_ref[...])
    lane = lax.broadcasted_iota(jnp.int32, logits.shape, 1)
    neg = jnp.float32(-jnp.inf)
    logits = jnp.where(lane < N_EXPERTS, logits, neg)
    m1 = jnp.max(logits, axis=-1, keepdims=True)
    i1 = jnp.min(jnp.where(logits == m1, lane, 128), axis=-1, keepdims=True)
    rest = jnp.where(lane == i1, neg, logits)
    m2 = jnp.max(rest, axis=-1, keepdims=True)
    i2 = jnp.min(jnp.where(rest == m2, lane, 128), axis=-1, keepdims=True)
    e = jnp.exp(m2 - m1)
    p1 = 1.0 / (1.0 + e)
    p2 = e / (1.0 + e)
    oh1 = jnp.where(lane == i1, 1.0, 0.0).astype(F32)
    oh2 = jnp.where(lane == i2, 1.0, 0.0).astype(F32)
    tr = lax.broadcasted_iota(jnp.int32, (TOK_TILE, TOK_TILE), 0)
    tc = lax.broadcasted_iota(jnp.int32, (TOK_TILE, TOK_TILE), 1)
    cum = _bdot(jnp.where(tr >= tc, 1.0, 0.0), oh1 + oh2)
    rank = cum - (oh1 + oh2)
    r1 = jnp.sum(oh1 * rank, axis=-1, keepdims=True)
    r2 = jnp.sum(oh2 * rank, axis=-1, keepdims=True)
    info = jnp.where(lane == 0, i1.astype(F32), 0.0)
    info = jnp.where(lane == 1, i2.astype(F32), info)
    info = jnp.where(lane == 2, p1, info)
    info = jnp.where(lane == 3, p2, info)
    info = jnp.where(lane == 4, r1, info)
    info = jnp.where(lane == 5, r2, info)
    info_ref[...] = info
    cnt_ref[...] = jnp.broadcast_to(cum[TOK_TILE - 1:TOK_TILE, :], (8, 128))


def _router(x, yb, gb, wo, g, r_pad):
    row = lambda w: pl.BlockSpec((TOK_TILE, w), lambda i: (i, 0))
    return pl.pallas_call(
        _router_kernel,
        grid=(N_TOK // TOK_TILE,),
        in_specs=[row(D_MODEL), _halves_tile_spec(), row(D_MODEL),
                  pl.BlockSpec((D_MODEL, D_MODEL), lambda i: (0, 0)),
                  pl.BlockSpec((1, D_MODEL), lambda i: (0, 0)),
                  pl.BlockSpec((D_MODEL, 128), lambda i: (0, 0))],
        out_specs=[row(D_MODEL), row(D_MODEL), row(128),
                   pl.BlockSpec((None, 8, 128), lambda i: (i, 0, 0))],
        out_shape=[jax.ShapeDtypeStruct((N_TOK, D_MODEL), F32),
                   jax.ShapeDtypeStruct((N_TOK, D_MODEL), F32),
                   jax.ShapeDtypeStruct((N_TOK, 128), F32),
                   jax.ShapeDtypeStruct((N_TOK // TOK_TILE, 8, 128), F32)],
        compiler_params=pltpu.CompilerParams(
            dimension_semantics=("arbitrary",), vmem_limit_bytes=VMEM_BIG),
        name="moe_router",
    )(x, yb, gb, wo, g, r_pad)


def _tile_rows_wait(src_ref, dst_ref, sem):
    pltpu.make_async_copy(src_ref.at[pl.ds(0, TOK_TILE), :], dst_ref, sem).wait()


def _dispatch_kernel(dest_ref, pad_ref, h_ref, xs_ref, zbuf, sem):
    base = pl.program_id(0) * (2 * TOK_TILE)

    @pl.when(pl.program_id(0) == 0)
    def _():
        zbuf[...] = jnp.zeros(zbuf.shape, F32)

        def zero_row(j, c):
            pltpu.make_async_copy(
                zbuf.at[pl.ds(0, 1), :], xs_ref.at[pl.ds(pad_ref[j], 1), :], sem).start()
            return c

        lax.fori_loop(0, MOE_PAD_ROWS, zero_row, 0, unroll=8)
        for _ in range(MOE_PAD_ROWS // TOK_TILE):
            _tile_rows_wait(h_ref, xs_ref.at[pl.ds(0, TOK_TILE), :], sem)

    def start(r, c):
        for k in range(2):
            pltpu.make_async_copy(
                h_ref.at[pl.ds(r, 1), :],
                xs_ref.at[pl.ds(dest_ref[base + 2 * r + k], 1), :], sem).start()
        return c

    lax.fori_loop(0, TOK_TILE, start, 0, unroll=8)
    for _ in range(2):
        _tile_rows_wait(h_ref, xs_ref.at[pl.ds(0, TOK_TILE), :], sem)


def _dispatch(dest, pad_rows, h):
    return pl.pallas_call(
        _dispatch_kernel,
        grid_spec=pltpu.PrefetchScalarGridSpec(
            num_scalar_prefetch=2,
            grid=(N_TOK // TOK_TILE,),
            in_specs=[pl.BlockSpec((TOK_TILE, D_MODEL), lambda i, d, p: (i, 0))],
            out_specs=pl.BlockSpec(memory_space=pl.ANY),
            scratch_shapes=[pltpu.VMEM((8, D_MODEL), F32), pltpu.SemaphoreType.DMA(())]),
        out_shape=jax.ShapeDtypeStruct((MOE_ROWS, D_MODEL), F32),
        compiler_params=pltpu.CompilerParams(dimension_semantics=("arbitrary",)),
        name="moe_dispatch",
    )(dest, pad_rows, h)


def _experts_kernel(te_ref, tv_ref, x_ref, wg_ref, wu_ref, wd_ref, o_ref):
    i = pl.program_id(0)

    @pl.when(tv_ref[i] > 0)
    def _():
        o_ref[...] = _swiglu_rows(x_ref[...].astype(BF16), wg_ref, wu_ref, wd_ref)

    @pl.when(tv_ref[i] == 0)
    def _():
        o_ref[...] = jnp.zeros((MOE_TILE, D_MODEL), F32)


def _experts(tile_expert, tile_valid, xs, wg, wu, wd):
    row = pl.BlockSpec((MOE_TILE, D_MODEL), lambda i, te, tv: (i, 0))
    return pl.pallas_call(
        _experts_kernel,
        grid_spec=pltpu.PrefetchScalarGridSpec(
            num_scalar_prefetch=2,
            grid=(MOE_ROWS // MOE_TILE,),
            in_specs=[row,
                      pl.BlockSpec((None, D_MODEL, D_FF), lambda i, te, tv: (te[i], 0, 0)),
                      pl.BlockSpec((None, D_MODEL, D_FF), lambda i, te, tv: (te[i], 0, 0)),
                      pl.BlockSpec((None, D_FF, D_MODEL), lambda i, te, tv: (te[i], 0, 0))],
            out_specs=row),
        out_shape=jax.ShapeDtypeStruct((MOE_ROWS, D_MODEL), F32),
        compiler_params=pltpu.CompilerParams(
            dimension_semantics=("arbitrary",), vmem_limit_bytes=VMEM_BIG),
        name="moe_experts",
    )(tile_expert, tile_valid, xs, wg, wu, wd)


def _combine_kernel(dest_ref, x_ref, info_ref, g_ref, ys_ref, yp_ref, ysm_ref, ybuf, sem):
    i = pl.program_id(0)
    n = pl.num_programs(0)

    def start_tile(tile):
        slot = tile % 2
        base = tile * (2 * TOK_TILE)

        def start(r, c):
            for k in range(2):
                pltpu.make_async_copy(
                    ys_ref.at[pl.ds(dest_ref[base + 2 * r + k], 1), :],
                    ybuf.at[slot, k, pl.ds(r, 1), :], sem.at[slot]).start()
            return c

        lax.fori_loop(0, TOK_TILE, start, 0, unroll=8)

    @pl.when(i == 0)
    def _():
        start_tile(i)

    @pl.when(i + 1 < n)
    def _():
        start_tile(i + 1)

    slot = i % 2
    for k in range(2):
        _tile_rows_wait(ys_ref, ybuf.at[slot, k], sem.at[slot])
    info = info_ref[...]
    x = x_ref[...] + info[:, 2:3] * ybuf[slot, 0] + info[:, 3:4] * ybuf[slot, 1]
    y = _rms(x, g_ref[...])

    @pl.when(i < PROMPT_TILES)
    def _():
        yp_ref[...] = y

    @pl.when(i == PROMPT_TILES)
    def _():
        ysm_ref[...] = y[0:DEC_BATCH, :]


def _combine(dest, x, info, g, ys):
    last = PROMPT_TILES - 1
    return pl.pallas_call(
        _combine_kernel,
        grid_spec=pltpu.PrefetchScalarGridSpec(
            num_scalar_prefetch=1,
            grid=(N_TOK // TOK_TILE,),
            in_specs=[pl.BlockSpec((TOK_TILE, D_MODEL), lambda i, d: (i, 0)),
                      pl.BlockSpec((TOK_TILE, 128), lambda i, d: (i, 0)),
                      pl.BlockSpec((1, D_MODEL), lambda i, d: (0, 0)),
                      pl.BlockSpec(memory_space=pl.ANY)],
            out_specs=[pl.BlockSpec((TOK_TILE, D_MODEL), lambda i, d: (jnp.minimum(i, last), 0)),
                       pl.BlockSpec((DEC_BATCH, D_MODEL), lambda i, d: (0, 0))],
            scratch_shapes=[pltpu.VMEM((2, 2, TOK_TILE, D_MODEL), F32),
                            pltpu.SemaphoreType.DMA((2,))]),
        out_shape=[jax.ShapeDtypeStruct((N_PROMPT, D_MODEL), F32),
                   jax.ShapeDtypeStruct((DEC_BATCH, D_MODEL), F32)],
        compiler_params=pltpu.CompilerParams(dimension_semantics=("arbitrary",)),
        name="moe_combine",
    )(dest, x, info, g, ys)


def _moe_layer(x, yb, gb, wo, g2, router_w, wg, wu, wd, final_g):
    r_pad = jnp.zeros((D_MODEL, 128), F32).at[:, :N_EXPERTS].set(router_w)
    x, h, info, cnt = _router(x, yb, gb, wo, g2, r_pad)
    n_tiles = N_TOK // TOK_TILE
    counts = cnt[:, 0, :N_EXPERTS].astype(jnp.int32)
    total = jnp.sum(counts, axis=0)
    padded = ((total + MOE_TILE - 1) // MOE_TILE) * MOE_TILE
    ends = jnp.cumsum(padded)
    starts = ends - padded
    base = starts[None, :] + jnp.cumsum(counts, axis=0) - counts
    choice = info[:, 0:2].astype(jnp.int32).reshape(n_tiles, TOK_TILE, 2)
    rank = info[:, 4:6].astype(jnp.int32).reshape(n_tiles, TOK_TILE, 2)
    onehot = choice[..., None] == jnp.arange(N_EXPERTS, dtype=jnp.int32)
    dest = (jnp.sum(jnp.where(onehot, base[:, None, None, :], 0), axis=-1) + rank).reshape(-1)
    tile_start = jnp.arange(MOE_ROWS // MOE_TILE, dtype=jnp.int32) * MOE_TILE
    tile_expert = jnp.minimum(
        jnp.sum((tile_start[:, None] >= ends[None, :]).astype(jnp.int32), axis=1), N_EXPERTS - 1)
    tile_valid = (tile_start < ends[-1]).astype(jnp.int32)
    last_used = jnp.max(jnp.where(tile_valid > 0, tile_expert, 0))
    tile_expert = jnp.where(tile_valid > 0, tile_expert, last_used).astype(jnp.int32)

    gaps = padded - total
    gap_end = jnp.cumsum(gaps)
    j = jnp.arange(MOE_PAD_ROWS, dtype=jnp.int32)
    seg = jnp.sum((j[:, None] >= gap_end[None, :]).astype(jnp.int32), axis=1)
    seg_e = jnp.minimum(seg, N_EXPERTS - 1)
    in_gap = (starts + total)[seg_e] + j - (gap_end - gaps)[seg_e]
    pad_rows = jnp.where(seg < N_EXPERTS, in_gap, ends[-1] + j - gap_end[-1]).astype(jnp.int32)

    xs = _dispatch(dest, pad_rows, h)
    ys = _experts(tile_expert, tile_valid, xs, wg, wu, wd)
    return _combine(dest, x, info, final_g, ys)


def _rearranged_w_in(w_in):
    off_qkv = 2 * D_CONV
    off_z = off_qkv + CONV_DIM
    off_a = off_z + D_DELTA_V
    off_ga = off_a + 2 * N_HEADS
    off_gb = off_ga + D_MODEL
    w_in = w_in.astype(BF16)
    return jnp.concatenate(
        [w_in[:, :off_a], w_in[:, off_ga:off_gb], w_in[:, off_gb:], w_in[:, off_a:off_ga],
         jnp.zeros((D_MODEL, 128 - 2 * N_HEADS), BF16)], axis=1)


def _lane_vec(v, offset=0):
    return jnp.zeros((1, 128), F32).at[0, offset:offset + v.shape[0]].set(v)


def kernel(x_prompt, x_sample, state_delta, state_qkv_conv, state_dwconv, norm1_g, w_in, qkv_conv_w, a_log, dt_bias, o_norm_g, w_delta_out, dw_w, dw_b, ln_g, ln_b, w_conv_out, w_o, norm2_g, ffn_w_gate, ffn_w_up, ffn_w_down, router_w, exp_w_gate, exp_w_up, exp_w_down, final_norm_g):
    x = (x_prompt.reshape(N_PROMPT, D_MODEL), x_sample.reshape(DEC_BATCH, D_MODEL))
    s_p, q_p, d_p, s_s, q_s, d_s = [], [], [], [], [], []
    for l in range(DEPTH):
        glu, qkv, z, ga, gb, ab = _norm_proj(x, norm1_g[l][None, :], _rearranged_w_in(w_in[l]))
        dwb, lng, lnb = dw_b[l][None, :], ln_g[l][None, :], ln_b[l][None, :]
        wco = w_conv_out[l].astype(BF16)
        ya_s, dst_s = _conv_sample(glu, state_dwconv[l].reshape(DEC_BATCH, -1), dw_w[l], dwb, lng, lnb, wco)
        wo = w_o[l].astype(BF16)
        x, dst_p = _conv_prompt(x, glu, ga, ya_s, dw_w[l], dwb, lng, lnb, wco, wo)
        alog, dtb, ong = _lane_vec(a_log[l]), _lane_vec(dt_bias[l]), o_norm_g[l][None, :]
        wdo = w_delta_out[l].astype(BF16)
        yb_s, qst_s, sst_s = _gdn_sample(qkv, z, ab, state_qkv_conv[l].reshape(DEC_BATCH, -1),
                                         state_delta, l, qkv_conv_w[l], alog, dtb, ong, wdo)
        rp, qg, kt, dl, m, aq, qst_p = _gdn_pre(qkv, ab, qkv_conv_w[l], alog, dtb)
        yb, sst_p = _gdn_scan(rp, qg, kt, dl, _gdn_solve(m), aq, z, yb_s, ong, wdo)
        i = l // 2
        if l % 2 == 0:
            x = _ffn(x, yb, gb, wo, norm2_g[l][None, :], ffn_w_gate[i].astype(BF16),
                     ffn_w_up[i].astype(BF16), ffn_w_down[i].astype(BF16))
        else:
            y_prompt, y_sample = _moe_layer(
                x, yb, gb, wo, norm2_g[l][None, :], router_w[i], exp_w_gate[i].astype(BF16),
                exp_w_up[i].astype(BF16), exp_w_down[i].astype(BF16), final_norm_g[None, :])
        s_p.append(sst_p)
        q_p.append(qst_p)
        d_p.append(dst_p)
        s_s.append(sst_s)
        q_s.append(qst_s.reshape(DEC_BATCH, QK_CONV - 1, CONV_DIM))
        d_s.append(dst_s.reshape(DEC_BATCH, DW_WIDTH - 1, D_CONV))
    return (y_prompt.reshape(BATCH, SEQ, D_MODEL), y_sample.reshape(DEC_BATCH, 1, D_MODEL), jnp.stack(s_p), jnp.stack(q_p), jnp.stack(d_p),
            jnp.stack(s_s), jnp.stack(q_s), jnp.stack(d_s))
```

```python
import functools

import jax
import jax.numpy as jnp
from jax import lax
from jax.experimental import pallas as pl
from jax.experimental.pallas import tpu as pltpu

F32 = jnp.float32
BF16 = jnp.bfloat16
HIGHEST = lax.Precision.HIGHEST

D_MODEL = 1024
BATCH = 8
SEQ = 2048
DEPTH = 2
DEC_BATCH = 128
D_CONV = 512
DW_WIDTH = 31
N_HEADS = 4
HEAD_K = 128
HEAD_V = 128
QK_CONV = 4
D_DELTA_K = N_HEADS * HEAD_K
D_DELTA_V = N_HEADS * HEAD_V
CONV_DIM = 2 * D_DELTA_K + D_DELTA_V
D_FF = 2816
N_EXPERTS = 8
EPS = 1e-6

N_PROMPT = BATCH * SEQ
TOK_TILE = 512
SEQ_TILE = TOK_TILE
N_TOK = N_PROMPT + TOK_TILE
SEQ_TILES = SEQ // SEQ_TILE
PROMPT_TILES = N_PROMPT // TOK_TILE
HALF_TILES = PROMPT_TILES // 2
HALF_SEQS = BATCH // 2
CHUNK = 64
M_PITCH = 68
SLABS_TILE = 2 * (SEQ_TILE // CHUNK)
M_ROWS_TILE = SLABS_TILE * M_PITCH
SOLVE_SLABS = 128
ROW_BLK = 64
MXU_DIM = 256
FF_CHUNKS = ((0, 6 * MXU_DIM), (6 * MXU_DIM, 5 * MXU_DIM))
MOE_TILE = 256
MOE_ROWS = ((2 * N_TOK + N_EXPERTS * (MOE_TILE - 1)) // MOE_TILE + 1) * MOE_TILE
MOE_PAD_ROWS = MOE_ROWS - 2 * N_TOK
assert MOE_PAD_ROWS % TOK_TILE == 0
DEC_BLK = 8

C_GLU, C_QKV, C_Z, C_GA, C_GB, C_AB = 0, 1024, 2560, 3072, 4096, 5120
N_PROJ = 5248

VMEM_BIG = 56 * 1024 * 1024


def _bdot(a, b):
    return jnp.dot(a.astype(BF16), b.astype(BF16), preferred_element_type=F32)


def _hdot(a, b):
    return jnp.dot(a, b, preferred_element_type=F32, precision=HIGHEST)


def _dot3(a, b):
    a_hi, b_hi = a.astype(BF16), b.astype(BF16)
    a_lo = (a - a_hi.astype(F32)).astype(BF16)
    b_lo = (b - b_hi.astype(F32)).astype(BF16)
    dot = functools.partial(jnp.dot, preferred_element_type=F32)
    return dot(a_hi, b_hi) + (dot(a_hi, b_lo) + dot(a_lo, b_hi))


def _rms(x, g):
    return x * lax.rsqrt(jnp.mean(x * x, axis=-1, keepdims=True) + EPS) * g


def _silu(x):
    return x * jax.nn.sigmoid(x)


def _resident(shape):
    return pl.BlockSpec(shape, lambda *_: (0,) * len(shape), pipeline_mode=pl.Buffered(1))


def _stream_x_specs(x):
    if isinstance(x, tuple):
        last = PROMPT_TILES - 1
        return ([pl.BlockSpec((TOK_TILE, D_MODEL), lambda i: (jnp.minimum(i, last), 0)),
                 pl.BlockSpec((DEC_BATCH, D_MODEL), lambda i: (0, 0))], list(x))
    return [pl.BlockSpec((TOK_TILE, D_MODEL), lambda i: (i, 0))], [x]


def _stream_x_tile(x_refs, xbuf):
    if len(x_refs) == 1:
        return x_refs[0][...]
    xp_ref, xs_ref = x_refs
    i = pl.program_id(0)

    @pl.when(i < PROMPT_TILES)
    def _():
        xbuf[...] = xp_ref[...]

    @pl.when(i == PROMPT_TILES)
    def _():
        xbuf[0:DEC_BATCH, :] = xs_ref[...]
        xbuf[DEC_BATCH:, :] = jnp.zeros((TOK_TILE - DEC_BATCH, D_MODEL), F32)

    return xbuf[...]


def _norm_proj_kernel(n_x, *refs):
    x_refs = refs[:n_x]
    g_ref, w_ref, glu_ref, qkv_ref, z_ref, ga_ref, gb_ref, ab_ref, xbuf = refs[n_x:]
    h = _rms(_stream_x_tile(x_refs, xbuf), g_ref[...]).astype(BF16)

    def proj(lo, hi):
        return jnp.dot(h, w_ref[:, lo:hi], preferred_element_type=F32)

    glu_ref[...] = proj(C_GLU, C_QKV)
    qkv_ref[...] = proj(C_QKV, C_Z)
    z_ref[...] = proj(C_Z, C_GA)
    ga_ref[...] = proj(C_GA, C_GB)
    gb_ref[...] = proj(C_GB, C_AB)
    ab_ref[...] = proj(C_AB, N_PROJ)


def _norm_proj(x, g, w_all):
    row = lambda w: pl.BlockSpec((TOK_TILE, w), lambda i: (i, 0))
    widths = (1024, CONV_DIM, D_DELTA_V, D_MODEL, D_MODEL, 128)
    x_specs, x_ops = _stream_x_specs(x)
    return pl.pallas_call(
        functools.partial(_norm_proj_kernel, len(x_ops)),
        grid=(N_TOK // TOK_TILE,),
        in_specs=x_specs + [pl.BlockSpec((1, D_MODEL), lambda i: (0, 0)),
                            _resident((D_MODEL, N_PROJ))],
        out_specs=[row(w) for w in widths],
        out_shape=[jax.ShapeDtypeStruct((N_TOK, w), F32) for w in widths],
        scratch_shapes=[pltpu.VMEM((TOK_TILE, D_MODEL), F32)],
        compiler_params=pltpu.CompilerParams(
            dimension_semantics=("arbitrary",), vmem_limit_bytes=VMEM_BIG),
        name="norm_proj",
    )(*x_ops, g, w_all)


def _gated_branch(x, gate_ref, y, wo_ref):
    return x + _bdot(jax.nn.sigmoid(gate_ref[...]) * y, wo_ref[...])


def _halves_tile_spec():
    half = lambda i: i // HALF_TILES - i // (2 * HALF_TILES)
    return pl.BlockSpec((None, TOK_TILE, D_MODEL),
                        lambda i, *_: (half(i), i - HALF_TILES * half(i), 0))


def _swiglu_rows(h, wg_ref, wu_ref, wd_ref):
    acc = None
    for lo, n in FF_CHUNKS:
        a = jnp.dot(h, wg_ref[:, lo:lo + n], preferred_element_type=F32)
        b = jnp.dot(h, wu_ref[:, lo:lo + n], preferred_element_type=F32)
        part = jnp.dot((_silu(a) * b).astype(BF16), wd_ref[lo:lo + n, :],
                       preferred_element_type=F32)
        acc = part if acc is None else acc + part
    return acc


def _ffn_kernel(x_ref, yb_ref, gb_ref, wo_ref, g_ref, wg_ref, wu_ref, wd_ref, o_ref):
    x = _gated_branch(x_ref[...], gb_ref, yb_ref[...], wo_ref)
    h = _rms(x, g_ref[...]).astype(BF16)
    o_ref[...] = x + _swiglu_rows(h, wg_ref, wu_ref, wd_ref)


def _ffn(x, yb, gb, wo, g, wg, wu, wd):
    row = pl.BlockSpec((TOK_TILE, D_MODEL), lambda i: (i, 0))
    return pl.pallas_call(
        _ffn_kernel,
        grid=(N_TOK // TOK_TILE,),
        in_specs=[row, _halves_tile_spec(), row, _resident((D_MODEL, D_MODEL)),
                  pl.BlockSpec((1, D_MODEL), lambda i: (0, 0)),
                  _resident((D_MODEL, D_FF)), _resident((D_MODEL, D_FF)),
                  _resident((D_FF, D_MODEL))],
        out_specs=row,
        out_shape=jax.ShapeDtypeStruct((N_TOK, D_MODEL), F32),
        compiler_params=pltpu.CompilerParams(
            dimension_semantics=("arbitrary",), vmem_limit_bytes=VMEM_BIG),
        name="ffn_dense",
    )(x, yb, gb, wo, g, wg, wu, wd)


def _conv_tail(c, dwb, lng, lnb):
    c = c + dwb
    mu = jnp.mean(c, axis=-1, keepdims=True)
    var = jnp.mean(jnp.square(c - mu), axis=-1, keepdims=True)
    c = (c - mu) * lax.rsqrt(var + EPS) * lng + lnb
    return _silu(c)


def _sample_rows_tile(ys_ref, y_ref):
    y_ref[0:DEC_BATCH, :] = ys_ref[...]
    y_ref[DEC_BATCH:, :] = jnp.zeros((TOK_TILE - DEC_BATCH, D_MODEL), F32)


def _conv_prompt_kernel(n_x, *refs):
    x_refs = refs[:n_x]
    (glu_ref, ga_ref, ys_ref, dww_ref, dwb_ref, lng_ref, lnb_ref, wout_ref, wo_ref,
     o_ref, st_ref, ubuf, ushift, cbuf, xbuf) = refs[n_x:]
    i = pl.program_id(0)
    if n_x == 2:
        _stream_x_tile(x_refs, xbuf)
        x_ref = xbuf
    else:
        x_ref = x_refs[0]

    @pl.when(i < PROMPT_TILES)
    def _():
        _conv_prompt_tile(i % SEQ_TILES, glu_ref, dww_ref, dwb_ref, lng_ref, lnb_ref,
                          st_ref, ubuf, ushift, cbuf)
        ya = jnp.dot(cbuf[...], wout_ref[...], preferred_element_type=F32)
        o_ref[...] = _gated_branch(x_ref[...], ga_ref, ya, wo_ref)

    @pl.when(i == PROMPT_TILES)
    def _():
        ya = jnp.concatenate(
            [ys_ref[...], jnp.zeros((TOK_TILE - DEC_BATCH, D_MODEL), F32)], axis=0)
        o_ref[...] = _gated_branch(x_ref[...], ga_ref, ya, wo_ref)


def _conv_prompt_tile(t, glu_ref, dww_ref, dwb_ref, lng_ref, lnb_ref, st_ref, ubuf, ushift, cbuf):
    halo = 32

    @pl.when(t == 0)
    def _():
        ubuf[0:halo, :] = jnp.zeros((halo, D_CONV), F32)

    @pl.when(t > 0)
    def _():
        ubuf[0:halo, :] = ubuf[SEQ_TILE:SEQ_TILE + halo, :]

    glu = glu_ref[...]
    ubuf[halo:halo + SEQ_TILE, :] = glu[:, :D_CONV] * jax.nn.sigmoid(glu[:, D_CONV:])
    for r in range(1, 8):
        ushift[r - 1, 0:SEQ_TILE + halo - 8, :] = ubuf[r:r + SEQ_TILE + halo - 8, :]
    w = dww_ref[...]
    first = halo - (DW_WIDTH - 1)
    for rb in range(SEQ_TILE // ROW_BLK):
        r0 = rb * ROW_BLK
        acc = jnp.zeros((ROW_BLK, D_CONV), F32)
        for j in range(DW_WIDTH):
            a, r = divmod(first + j, 8)
            if r == 0:
                win = ubuf[r0 + 8 * a:r0 + 8 * a + ROW_BLK, :]
            else:
                win = ushift[r - 1, r0 + 8 * a:r0 + 8 * a + ROW_BLK, :]
            acc = acc + w[j:j + 1, :] * win
        c = _conv_tail(acc, dwb_ref[...], lng_ref[...], lnb_ref[...])
        cbuf[r0:r0 + ROW_BLK, :] = c.astype(BF16)

    @pl.when(t == SEQ_TILES - 1)
    def _():
        st_ref[...] = ubuf[halo + SEQ_TILE - (DW_WIDTH - 1):halo + SEQ_TILE, :]


def _seq_of_tile(i):
    return jnp.minimum(i // SEQ_TILES, BATCH - 1)


def _conv_prompt(x, glu, ga, ya_sample, dww, dwb, lng, lnb, wout, wo):
    vec = pl.BlockSpec((1, D_CONV), lambda i: (0, 0))
    row = pl.BlockSpec((SEQ_TILE, D_MODEL), lambda i: (i, 0))
    x_specs, x_ops = _stream_x_specs(x)
    return pl.pallas_call(
        functools.partial(_conv_prompt_kernel, len(x_ops)),
        grid=(PROMPT_TILES + 1,),
        in_specs=x_specs + [
            pl.BlockSpec((SEQ_TILE, 2 * D_CONV), lambda i: (i, 0)), row,
            pl.BlockSpec((DEC_BATCH, D_MODEL), lambda i: (0, 0)),
            pl.BlockSpec((DW_WIDTH, D_CONV), lambda i: (0, 0)),
            vec, vec, vec,
            pl.BlockSpec((D_CONV, D_MODEL), lambda i: (0, 0)),
            pl.BlockSpec((D_MODEL, D_MODEL), lambda i: (0, 0))],
        out_specs=[row,
                   pl.BlockSpec((None, DW_WIDTH - 1, D_CONV), lambda i: (_seq_of_tile(i), 0, 0))],
        out_shape=[jax.ShapeDtypeStruct((N_TOK, D_MODEL), F32),
                   jax.ShapeDtypeStruct((BATCH, DW_WIDTH - 1, D_CONV), F32)],
        scratch_shapes=[pltpu.VMEM((SEQ_TILE + 32, D_CONV), F32),
                        pltpu.VMEM((7, SEQ_TILE + 24, D_CONV), F32),
                        pltpu.VMEM((SEQ_TILE, D_CONV), BF16),
                        pltpu.VMEM((TOK_TILE, D_MODEL), F32)],
        compiler_params=pltpu.CompilerParams(
            dimension_semantics=("arbitrary",), vmem_limit_bytes=VMEM_BIG),
        name="conv_prompt",
    )(*x_ops, glu, ga, ya_sample, dww, dwb, lng, lnb, wout, wo)


def _conv_sample_kernel(glu_ref, st_ref, dww_ref, dwb_ref, lng_ref, lnb_ref, wout_ref,
                        ya_ref, nst_ref):
    glu = glu_ref[...]
    u = glu[:, :D_CONV] * jax.nn.sigmoid(glu[:, D_CONV:])
    w = dww_ref[...]
    nbuf = DW_WIDTH - 1
    acc = w[nbuf:nbuf + 1, :] * u
    for j in range(nbuf):
        acc = acc + w[j:j + 1, :] * st_ref[:, j * D_CONV:(j + 1) * D_CONV]
    c = _conv_tail(acc, dwb_ref[...], lng_ref[...], lnb_ref[...])
    ya_ref[...] = _bdot(c, wout_ref[...])
    nst_ref[:, 0:(nbuf - 1) * D_CONV] = st_ref[:, D_CONV:nbuf * D_CONV]
    nst_ref[:, (nbuf - 1) * D_CONV:nbuf * D_CONV] = u


def _conv_sample(glu, st, dww, dwb, lng, lnb, wout):
    nbuf = DW_WIDTH - 1
    blk = N_PROMPT // DEC_BATCH
    vec = pl.BlockSpec((1, D_CONV), lambda i: (0, 0))
    return pl.pallas_call(
        _conv_sample_kernel,
        grid=(1,),
        in_specs=[pl.BlockSpec((DEC_BATCH, 2 * D_CONV), lambda i: (blk, 0)),
                  pl.BlockSpec((DEC_BATCH, nbuf * D_CONV), lambda i: (0, 0)),
                  pl.BlockSpec((DW_WIDTH, D_CONV), lambda i: (0, 0)),
                  vec, vec, vec,
                  pl.BlockSpec((D_CONV, D_MODEL), lambda i: (0, 0))],
        out_specs=[pl.BlockSpec((DEC_BATCH, D_MODEL), lambda i: (0, 0)),
                   pl.BlockSpec((DEC_BATCH, nbuf * D_CONV), lambda i: (0, 0))],
        out_shape=[jax.ShapeDtypeStruct((DEC_BATCH, D_MODEL), F32),
                   jax.ShapeDtypeStruct((DEC_BATCH, nbuf * D_CONV), F32)],
        compiler_params=pltpu.CompilerParams(dimension_semantics=("arbitrary",)),
        name="conv_sample",
    )(glu, st, dww, dwb, lng, lnb, wout)


def _qkv_act(c):
    c = _silu(c)
    qs, ks, vs = [], [], []
    for h in range(N_HEADS):
        q = c[:, h * HEAD_K:(h + 1) * HEAD_K]
        k = c[:, D_DELTA_K + h * HEAD_K:D_DELTA_K + (h + 1) * HEAD_K]
        qs.append(q * lax.rsqrt(jnp.sum(q * q, axis=-1, keepdims=True) + EPS) * (HEAD_K ** -0.5))
        ks.append(k * lax.rsqrt(jnp.sum(k * k, axis=-1, keepdims=True) + EPS))
        vs.append(c[:, 2 * D_DELTA_K + h * HEAD_V:2 * D_DELTA_K + (h + 1) * HEAD_V])
    return qs, ks, vs


def _gate_norm(o, ong, z):
    o = o * lax.rsqrt(jnp.mean(o * o, axis=-1, keepdims=True) + EPS) * ong
    return o * _silu(z)


def _gdn_pre_kernel(qkv_ref, ab_ref, cw_ref, alog_ref, dtb_ref,
                    rp_ref, qg_ref, kt_ref, dl_ref, m_ref, aq_ref, sq_ref, xbuf):
    t = pl.program_id(0) % SEQ_TILES
    halo = 8

    @pl.when(t == 0)
    def _():
        xbuf[0:halo, :] = jnp.zeros((halo, CONV_DIM), F32)

    @pl.when(t > 0)
    def _():
        xbuf[0:halo, :] = xbuf[SEQ_TILE:SEQ_TILE + halo, :]

    xbuf[halo:halo + SEQ_TILE, :] = qkv_ref[...]

    @pl.when(t == SEQ_TILES - 1)
    def _():
        sq_ref[...] = xbuf[halo + SEQ_TILE - (QK_CONV - 1):halo + SEQ_TILE, :]

    ab = ab_ref[...]
    lane = lax.broadcasted_iota(jnp.int32, (SEQ_TILE, 128), 1)
    gb = jnp.where(lane < N_HEADS,
                   -jnp.exp(alog_ref[...]) * jax.nn.softplus(ab + dtb_ref[...]),
                   jax.nn.sigmoid(ab))
    cw = cw_ref[...]
    first = halo - (QK_CONV - 1)
    ri = lax.broadcasted_iota(jnp.int32, (CHUNK, 128), 0)
    li = lax.broadcasted_iota(jnp.int32, (CHUNK, 128), 1)
    left = li < CHUNK
    jj = jnp.where(left, li, li - CHUNK)
    strict = ri > jj
    tril = ri >= jj
    r64 = lax.broadcasted_iota(jnp.int32, (CHUNK, CHUNK), 0)
    c64 = lax.broadcasted_iota(jnp.int32, (CHUNK, CHUNK), 1)
    csum = jnp.where(r64 >= c64, 1.0, 0.0).astype(F32)
    for c in range(SEQ_TILE // CHUNK):
        r0 = c * CHUNK
        acc = cw[0:1, :] * xbuf[r0 + first:r0 + first + CHUNK, :]
        for j in range(1, QK_CONV):
            acc = acc + cw[j:j + 1, :] * xbuf[r0 + first + j:r0 + first + j + CHUNK, :]
        qs, ks, vs = _qkv_act(acc)
        gbc = gb[r0:r0 + CHUNK, :]
        gc = _hdot(csum, gbc)
        gct = gc.T
        eg = jnp.exp(gc)
        glast = gc[CHUNK - 1:CHUNK, :]
        kdec = jnp.exp(glast - gc)
        dl = jnp.exp(glast)
        for hp in range(N_HEADS // 2):
            h0, h1 = 2 * hp, 2 * hp + 1
            slab = 2 * c + hp
            kb, rhs, qg, kd = [], [], [], []
            for h in (h0, h1):
                beta = gbc[:, N_HEADS + h:N_HEADS + h + 1]
                egh = eg[:, h:h + 1]
                kb.append(ks[h] * beta)
                rhs.append(jnp.concatenate([vs[h] * beta, kb[-1] * egh], axis=1))
                qg.append(qs[h] * egh)
                kd.append(ks[h] * kdec[:, h:h + 1])
            rows = slice(slab * 2 * CHUNK, (slab + 1) * 2 * CHUNK)
            rp_ref[rows, :] = jnp.concatenate(rhs, axis=0).astype(BF16)
            qg_ref[rows, :] = jnp.concatenate(qg, axis=0).astype(BF16)
            kt_ref[rows, :] = jnp.concatenate(kd, axis=0).T.astype(BF16)
            dl_ref[slab * 8:(slab + 1) * 8, :] = jnp.concatenate(
                [jnp.broadcast_to(dl[:, h0:h0 + 1], (8, HEAD_V)),
                 jnp.broadcast_to(dl[:, h1:h1 + 1], (8, HEAD_V))], axis=1)
            lhs = jnp.concatenate(kb + [qs[h0], qs[h1]], axis=0)
            keys = jnp.concatenate([ks[h0], ks[h1]], axis=0)
            prod = lax.dot_general(lhs.astype(BF16), keys.astype(BF16), (((1,), (1,)), ((), ())),
                                   preferred_element_type=F32)
            kk = jnp.where(left, prod[0:CHUNK], prod[CHUNK:2 * CHUNK])
            qk = jnp.where(left, prod[2 * CHUNK:3 * CHUNK], prod[3 * CHUNK:4 * CHUNK])
            gcol = jnp.where(left, gc[:, h0:h0 + 1], gc[:, h1:h1 + 1])
            grow = jnp.concatenate([gct[h0:h0 + 1, :], gct[h1:h1 + 1, :]], axis=1)
            diff = gcol - grow
            m_ref[slab * M_PITCH:slab * M_PITCH + CHUNK, :] = jnp.where(
                strict, kk * jnp.exp(jnp.where(strict, diff, 0.0)), 0.0)
            m_ref[slab * M_PITCH + CHUNK:(slab + 1) * M_PITCH, :] = jnp.zeros(
                (M_PITCH - CHUNK, 128), F32)
            aq_ref[slab * CHUNK:(slab + 1) * CHUNK, :] = jnp.where(
                tril, qk * jnp.exp(jnp.where(tril, diff, 0.0)), 0.0).astype(BF16)


def _gdn_pre(qkv, ab, cw, alog, dtb):
    tile = lambda w: pl.BlockSpec((SEQ_TILE, w), lambda i: (i, 0))
    vec = pl.BlockSpec((1, 128), lambda i: (0, 0))
    slab_rows = SLABS_TILE * 2 * CHUNK
    per_tile = lambda rows, w: pl.BlockSpec((rows, w), lambda i: (i, 0))
    return pl.pallas_call(
        _gdn_pre_kernel,
        grid=(PROMPT_TILES,),
        in_specs=[tile(CONV_DIM), tile(128),
                  pl.BlockSpec((QK_CONV, CONV_DIM), lambda i: (0, 0)), vec, vec],
        out_specs=[per_tile(slab_rows, 2 * HEAD_V), per_tile(slab_rows, HEAD_K),
                   per_tile(slab_rows, 128), per_tile(SLABS_TILE * 8, 2 * HEAD_V),
                   per_tile(M_ROWS_TILE, 128), per_tile(SLABS_TILE * CHUNK, 128),
                   pl.BlockSpec((None, QK_CONV - 1, CONV_DIM), lambda i: (i // SEQ_TILES, 0, 0))],
        out_shape=[jax.ShapeDtypeStruct((PROMPT_TILES * slab_rows, 2 * HEAD_V), BF16),
                   jax.ShapeDtypeStruct((PROMPT_TILES * slab_rows, HEAD_K), BF16),
                   jax.ShapeDtypeStruct((PROMPT_TILES * slab_rows, 128), BF16),
                   jax.ShapeDtypeStruct((PROMPT_TILES * SLABS_TILE * 8, 2 * HEAD_V), F32),
                   jax.ShapeDtypeStruct((PROMPT_TILES * M_ROWS_TILE, 128), F32),
                   jax.ShapeDtypeStruct((PROMPT_TILES * SLABS_TILE * CHUNK, 128), BF16),
                   jax.ShapeDtypeStruct((BATCH, QK_CONV - 1, CONV_DIM), F32)],
        scratch_shapes=[pltpu.VMEM((SEQ_TILE + 8, CONV_DIM), F32)],
        compiler_params=pltpu.CompilerParams(dimension_semantics=("arbitrary",)),
        name="gdn_pre",
    )(qkv, ab, cw, alog, dtb)


def _gdn_solve_kernel(m_ref, t_ref, mt, xt):
    def to_lanes(i, c):
        g = m_ref[pl.ds(i, SOLVE_SLABS, stride=M_PITCH), :]
        mt[pl.ds(pl.multiple_of(i * 128, 128), 128), :] = g.T
        return c

    lax.fori_loop(0, CHUNK, to_lanes, 0, unroll=4)

    sub = lax.broadcasted_iota(jnp.int32, (8, SOLVE_SLABS), 0)
    zeros8 = jnp.zeros((8, SOLVE_SLABS), F32)

    for ib in range(CHUNK // 8):
        def solve_row(i, c, ib=ib):
            base = pl.multiple_of(i * 128, 128)
            unit = jnp.where(sub + 8 * ib == i, 1.0, 0.0).astype(F32)
            acc = tuple([zeros8] * ib + [unit]) * 2

            def columns(groups):
                def step(j, acc):
                    xb = pl.multiple_of(j * 128, 128)
                    m0 = mt[pl.ds(base + j, 1), :]
                    m1 = mt[pl.ds(base + CHUNK + j, 1), :]
                    acc = list(acc)
                    for k in range(groups):
                        acc[k] = acc[k] - m0 * xt[pl.ds(xb + 8 * k, 8), :]
                        acc[ib + 1 + k] = acc[ib + 1 + k] - m1 * xt[pl.ds(xb + CHUNK + 8 * k, 8), :]
                    return tuple(acc)
                return step

            for jb in range(ib):
                acc = lax.fori_loop(8 * jb, 8 * jb + 8, columns(jb + 1), acc, unroll=True)
            acc = lax.fori_loop(8 * ib, i, columns(ib + 1), acc)
            for k in range(CHUNK // 8):
                xt[pl.ds(base + 8 * k, 8), :] = acc[k] if k <= ib else zeros8
                xt[pl.ds(base + CHUNK + 8 * k, 8), :] = acc[ib + 1 + k] if k <= ib else zeros8
            return c

        lax.fori_loop(8 * ib, 8 * ib + 8, solve_row, 0)

    def from_lanes(i, c):
        x = xt[pl.ds(pl.multiple_of(i * 128, 128), 128), :]
        t_ref[pl.ds(i, SOLVE_SLABS, stride=M_PITCH), :] = x.T
        return c

    lax.fori_loop(0, CHUNK, from_lanes, 0, unroll=4)
    for k in range(CHUNK, M_PITCH):
        t_ref[pl.ds(k, SOLVE_SLABS, stride=M_PITCH), :] = jnp.zeros((SOLVE_SLABS, 128), F32)


def _gdn_solve(m):
    rows = SOLVE_SLABS * M_PITCH
    return pl.pallas_call(
        _gdn_solve_kernel,
        grid=(PROMPT_TILES * SLABS_TILE // SOLVE_SLABS,),
        in_specs=[pl.BlockSpec((rows, 128), lambda i: (i, 0))],
        out_specs=pl.BlockSpec((rows, 128), lambda i: (i, 0)),
        out_shape=jax.ShapeDtypeStruct(m.shape, F32),
        scratch_shapes=[pltpu.VMEM((CHUNK * 128, SOLVE_SLABS), F32),
                        pltpu.VMEM((CHUNK * 128, SOLVE_SLABS), F32)],
        compiler_params=pltpu.CompilerParams(dimension_semantics=("arbitrary",)),
        name="gdn_solve",
    )(m)


def _gdn_scan_kernel(rp_ref, qg_ref, kt_ref, dl_ref, tm_ref, aq_ref, z0_ref, z1_ref, ys_ref,
                     ong_ref, wout_ref, yb_ref, ss_ref, obuf, s_scr, uw_scr, lhs_scr, b_scr, d_scr):
    i = pl.program_id(0)
    pl.when(i < HALF_TILES)(functools.partial(
        _gdn_scan_tile, i % SEQ_TILES, rp_ref, qg_ref, kt_ref, dl_ref, tm_ref, aq_ref,
        (z0_ref, z1_ref), ong_ref, wout_ref, yb_ref, ss_ref, obuf, s_scr, uw_scr, lhs_scr, b_scr, d_scr))

    @pl.when(i == HALF_TILES)
    def _():
        yb_ref[0] = jnp.zeros((TOK_TILE, D_MODEL), F32)
        _sample_rows_tile(ys_ref, yb_ref.at[1])


def _gdn_scan_tile(t, rp_ref, qg_ref, kt_ref, dl_ref, tm_ref, aq_ref, z_refs, ong_ref, wout_ref,
                   yb_ref, ss_ref, obuf, s_scr, uw_scr, lhs_scr, b_scr, d_scr):
    @pl.when(t == 0)
    def _():
        s_scr[...] = jnp.zeros((2, N_HEADS // 2, HEAD_K, 2 * HEAD_V), F32)

    left = lax.broadcasted_iota(jnp.int32, (CHUNK, 128), 1) < CHUNK
    ong = ong_ref[...]
    zero = jnp.zeros((CHUNK, HEAD_V), BF16)
    pairs = N_HEADS // 2
    n_chunks = SEQ_TILE // CHUNK

    def pair_diag(x):
        x = x.astype(BF16)
        none = jnp.zeros_like(x)
        return jnp.concatenate([jnp.where(left, x, none), jnp.where(left, none, x)], axis=0)

    def head_diag(x):
        return jnp.concatenate([jnp.concatenate([x[0:CHUNK], zero], axis=1),
                                jnp.concatenate([zero, x[CHUNK:]], axis=1)], axis=0)

    dot = functools.partial(jnp.dot, preferred_element_type=F32)
    for c in range(n_chunks):
        for half in range(2):
            for hp in range(pairs):
                slab = 2 * c + hp
                n = (c * 2 + half) * pairs + hp
                rows = slice(slab * 2 * CHUNK, (slab + 1) * 2 * CHUNK)
                tm = pair_diag(tm_ref[half, slab * M_PITCH:slab * M_PITCH + CHUNK, :])
                uw_scr[n] = dot(tm, rp_ref[half, rows, :]).astype(BF16)

    for c in range(n_chunks):
        for half in range(2):
            for hp in range(pairs):
                slab = 2 * c + hp
                n = (c * 2 + half) * pairs + hp
                rows = slice(slab * 2 * CHUNK, (slab + 1) * 2 * CHUNK)
                aq = pair_diag(aq_ref[half, slab * CHUNK:(slab + 1) * CHUNK, :])
                ub = uw_scr[n, :, :HEAD_V]
                wb = uw_scr[n, :, HEAD_V:]
                gb = dot(kt_ref[half, rows, :],
                         jnp.concatenate([head_diag(wb), head_diag(ub)], axis=1))
                aw_au = dot(aq, jnp.concatenate([wb, ub], axis=1))
                cm = qg_ref[half, rows, :].astype(F32) - aw_au[:, :HEAD_V]
                lhs_scr[n] = jnp.concatenate(
                    [gb[:, 0:HEAD_V], gb[:, HEAD_V:2 * HEAD_V], cm], axis=0).astype(BF16)
                b_scr[n] = gb[:, 2 * HEAD_V:]
                d_scr[n] = aw_au[:, HEAD_V:]

    for c in range(n_chunks):
        r0 = c * CHUNK
        for half in range(2):
            for hp in range(pairs):
                slab = 2 * c + hp
                n = (c * 2 + half) * pairs + hp
                s = s_scr[half, hp]
                r = dot(lhs_scr[n], s.astype(BF16))
                gs = jnp.concatenate([r[0:HEAD_K, 0:HEAD_V], r[HEAD_K:2 * HEAD_K, HEAD_V:]], axis=1)
                s_scr[half, hp] = s * dl_ref[half, slab * 8:slab * 8 + 1, :] - gs + b_scr[n]
                o = jnp.concatenate([r[2 * HEAD_K:2 * HEAD_K + CHUNK, 0:HEAD_V],
                                     r[2 * HEAD_K + CHUNK:, HEAD_V:]], axis=0) + d_scr[n]
                for k in range(2):
                    h = 2 * hp + k
                    zh = z_refs[half][r0:r0 + CHUNK, h * HEAD_V:(h + 1) * HEAD_V]
                    obuf[half, r0:r0 + CHUNK, h * HEAD_V:(h + 1) * HEAD_V] = _gate_norm(
                        o[k * CHUNK:(k + 1) * CHUNK], ong, zh)

    for half in range(2):
        yb_ref[half] = _bdot(obuf[half], wout_ref[...])

    @pl.when(t == SEQ_TILES - 1)
    def _():
        for half in range(2):
            for h in range(N_HEADS):
                ss_ref[half, h] = s_scr[half, h // 2, :, (h % 2) * HEAD_V:(h % 2 + 1) * HEAD_V]


def _gdn_scan(rp, qg, kt, dl, tm, aq, z, yb_sample, ong, wout):
    last = HALF_TILES - 1
    halves = lambda a: a.reshape((2, a.shape[0] // 2) + a.shape[1:])
    both = lambda a: pl.BlockSpec((2, a.shape[0] // PROMPT_TILES, a.shape[1]),
                                  lambda i: (0, jnp.minimum(i, last), 0))
    ztile = lambda off: pl.BlockSpec((SEQ_TILE, D_DELTA_V),
                                     lambda i: (jnp.minimum(i, last) + off, 0))
    ops = (rp, qg, kt, dl, tm, aq)
    n_steps = SEQ_TILE // CHUNK * 2 * (N_HEADS // 2)
    yb, ss = pl.pallas_call(
        _gdn_scan_kernel,
        grid=(HALF_TILES + 1,),
        in_specs=[both(a) for a in ops] + [
            ztile(0), ztile(HALF_TILES),
            pl.BlockSpec((DEC_BATCH, D_MODEL), lambda i: (0, 0)),
            pl.BlockSpec((1, 128), lambda i: (0, 0)),
            pl.BlockSpec((D_DELTA_V, D_MODEL), lambda i: (0, 0))],
        out_specs=[pl.BlockSpec((2, SEQ_TILE, D_MODEL), lambda i: (0, i, 0)),
                   pl.BlockSpec((2, None, N_HEADS, HEAD_K, HEAD_V),
                                lambda i: (0, jnp.minimum(i // SEQ_TILES, HALF_SEQS - 1), 0, 0, 0))],
        out_shape=[jax.ShapeDtypeStruct((2, (HALF_TILES + 1) * TOK_TILE, D_MODEL), F32),
                   jax.ShapeDtypeStruct((2, HALF_SEQS, N_HEADS, HEAD_K, HEAD_V), F32)],
        scratch_shapes=[pltpu.VMEM((2, SEQ_TILE, D_DELTA_V), F32),
                        pltpu.VMEM((2, N_HEADS // 2, HEAD_K, 2 * HEAD_V), F32),
                        pltpu.VMEM((n_steps, 2 * CHUNK, 2 * HEAD_V), BF16),
                        pltpu.VMEM((n_steps, 2 * HEAD_K + 2 * CHUNK, HEAD_K), BF16),
                        pltpu.VMEM((n_steps, HEAD_K, 2 * HEAD_V), F32),
                        pltpu.VMEM((n_steps, 2 * CHUNK, HEAD_V), F32)],
        compiler_params=pltpu.CompilerParams(
            dimension_semantics=("arbitrary",), vmem_limit_bytes=VMEM_BIG),
        name="gdn_scan",
    )(*[halves(a) for a in ops], z, z, yb_sample, ong, wout)
    return yb, ss.reshape(BATCH, N_HEADS, HEAD_K, HEAD_V)


def _gdn_sample_kernel(qkv_ref, z_ref, ab_ref, st_ref, s_ref, cw_ref, alog_ref, dtb_ref,
                       ong_ref, wout_ref, yb_ref, nst_ref, ns_ref):
    x = qkv_ref[...]
    cw = cw_ref[...]
    nbuf = QK_CONV - 1
    acc = cw[nbuf:nbuf + 1, :] * x
    for j in range(nbuf):
        acc = acc + cw[j:j + 1, :] * st_ref[:, j * CONV_DIM:(j + 1) * CONV_DIM]
    nst_ref[:, 0:(nbuf - 1) * CONV_DIM] = st_ref[:, CONV_DIM:nbuf * CONV_DIM]
    nst_ref[:, (nbuf - 1) * CONV_DIM:nbuf * CONV_DIM] = x
    qs, ks, vs = _qkv_act(acc)
    ab = ab_ref[...]
    g = -jnp.exp(alog_ref[...]) * jax.nn.softplus(ab + dtb_ref[...])
    beta = jax.nn.sigmoid(ab)
    eg = jnp.exp(g)
    z = z_ref[...]
    outs = []
    for h in range(N_HEADS):
        bh = beta[:, N_HEADS + h:N_HEADS + h + 1]
        egh = eg[:, h:h + 1]
        kb = ks[h] * bh
        w = kb * egh
        qg = qs[h] * egh
        u = vs[h] * bh
        qk = jnp.sum(qs[h] * ks[h], axis=-1, keepdims=True)
        o_rows = []
        for s in range(DEC_BLK):
            st = s_ref[s, h]
            wq = _bdot(jnp.concatenate([w[s:s + 1], qg[s:s + 1]], axis=0), st)
            vnew = u[s:s + 1] - wq[0:1]
            o_rows.append(wq[1:2] + qk[s:s + 1] * vnew)
            ns_ref[s, h] = st * egh[s:s + 1] + lax.dot_general(
                ks[h][s:s + 1].astype(BF16), vnew.astype(BF16), (((0,), (0,)), ((), ())),
                preferred_element_type=F32)
        o = jnp.concatenate(o_rows, axis=0)
        outs.append(_gate_norm(o, ong_ref[...], z[:, h * HEAD_V:(h + 1) * HEAD_V]))
    yb_ref[...] = _bdot(jnp.concatenate(outs, axis=1), wout_ref[...])


def _gdn_sample(qkv, z, ab, st, state_delta, layer, cw, alog, dtb, ong, wout):
    nbuf = QK_CONV - 1
    blk0 = N_PROMPT // DEC_BLK
    rows = lambda w: pl.BlockSpec((DEC_BLK, w), lambda i: (blk0 + i, 0))
    vec = pl.BlockSpec((1, 128), lambda i: (0, 0))
    sblk = pl.BlockSpec((DEC_BLK, N_HEADS, HEAD_K, HEAD_V), lambda i: (i, 0, 0, 0))
    return pl.pallas_call(
        _gdn_sample_kernel,
        grid=(DEC_BATCH // DEC_BLK,),
        in_specs=[rows(CONV_DIM), rows(D_DELTA_V), rows(128),
                  pl.BlockSpec((DEC_BLK, nbuf * CONV_DIM), lambda i: (i, 0)),
                  pl.BlockSpec((None, DEC_BLK, N_HEADS, HEAD_K, HEAD_V),
                               lambda i: (layer, i, 0, 0, 0)),
                  pl.BlockSpec((QK_CONV, CONV_DIM), lambda i: (0, 0)),
                  vec, vec, vec,
                  pl.BlockSpec((D_DELTA_V, D_MODEL), lambda i: (0, 0))],
        out_specs=[pl.BlockSpec((DEC_BLK, D_MODEL), lambda i: (i, 0)),
                   pl.BlockSpec((DEC_BLK, nbuf * CONV_DIM), lambda i: (i, 0)),
                   sblk],
        out_shape=[jax.ShapeDtypeStruct((DEC_BATCH, D_MODEL), F32),
                   jax.ShapeDtypeStruct((DEC_BATCH, nbuf * CONV_DIM), F32),
                   jax.ShapeDtypeStruct((DEC_BATCH, N_HEADS, HEAD_K, HEAD_V), F32)],
        compiler_params=pltpu.CompilerParams(dimension_semantics=("arbitrary",)),
        name="gdn_sample",
    )(qkv, z, ab, st, state_delta, cw, alog, dtb, ong, wout)


def _router_kernel(x_ref, yb_ref, gb_ref, wo_ref, g_ref, r_ref, xo_ref, h_ref, info_ref, cnt_ref):
    x = _gated_branch(x_ref[...], gb_ref, yb_ref[...], wo_ref)
    xo_ref[...] = x
    h = _rms(x, g_ref[...])
    h_ref[...] = h
    logits = _dot3(h, r_ref[...])
    lane = lax.broadcasted_iota(jnp.int32, logits.shape, 1)
    neg = jnp.float32(-jnp.inf)
    logits = jnp.where(lane < N_EXPERTS, logits, neg)
    m1 = jnp.max(logits, axis=-1, keepdims=True)
    i1 = jnp.min(jnp.where(logits == m1, lane, 128), axis=-1, keepdims=True)
    rest = jnp.where(lane == i1, neg, logits)
    m2 = jnp.max(rest, axis=-1, keepdims=True)
    i2 = jnp.min(jnp.where(rest == m2, lane, 128), axis=-1, keepdims=True)
    e = jnp.exp(m2 - m1)
    p1 = 1.0 / (1.0 + e)
    p2 = e / (1.0 + e)
    oh1 = jnp.where(lane == i1, 1.0, 0.0).astype(F32)
    oh2 = jnp.where(lane == i2, 1.0, 0.0).astype(F32)
    tr = lax.broadcasted_iota(jnp.int32, (TOK_TILE, TOK_TILE), 0)
    tc = lax.broadcasted_iota(jnp.int32, (TOK_TILE, TOK_TILE), 1)
    cum = _bdot(jnp.where(tr >= tc, 1.0, 0.0), oh1 + oh2)
    rank = cum - (oh1 + oh2)
    r1 = jnp.sum(oh1 * rank, axis=-1, keepdims=True)
    r2 = jnp.sum(oh2 * rank, axis=-1, keepdims=True)
    info = jnp.where(lane == 0, i1.astype(F32), 0.0)
    info = jnp.where(lane == 1, i2.astype(F32), info)
    info = jnp.where(lane == 2, p1, info)
    info = jnp.where(lane == 3, p2, info)
    info = jnp.where(lane == 4, r1, info)
    info = jnp.where(lane == 5, r2, info)
    info_ref[...] = info
    cnt_ref[...] = jnp.broadcast_to(cum[TOK_TILE - 1:TOK_TILE, :], (8, 128))


def _router(x, yb, gb, wo, g, r_pad):
    row = lambda w: pl.BlockSpec((TOK_TILE, w), lambda i: (i, 0))
    return pl.pallas_call(
        _router_kernel,
        grid=(N_TOK // TOK_TILE,),
        in_specs=[row(D_MODEL), _halves_tile_spec(), row(D_MODEL),
                  pl.BlockSpec((D_MODEL, D_MODEL), lambda i: (0, 0)),
                  pl.BlockSpec((1, D_MODEL), lambda i: (0, 0)),
                  pl.BlockSpec((D_MODEL, 128), lambda i: (0, 0))],
        out_specs=[row(D_MODEL), row(D_MODEL), row(128),
                   pl.BlockSpec((None, 8, 128), lambda i: (i, 0, 0))],
        out_shape=[jax.ShapeDtypeStruct((N_TOK, D_MODEL), F32),
                   jax.ShapeDtypeStruct((N_TOK, D_MODEL), F32),
                   jax.ShapeDtypeStruct((N_TOK, 128), F32),
                   jax.ShapeDtypeStruct((N_TOK // TOK_TILE, 8, 128), F32)],
        compiler_params=pltpu.CompilerParams(
            dimension_semantics=("arbitrary",), vmem_limit_bytes=VMEM_BIG),
        name="moe_router",
    )(x, yb, gb, wo, g, r_pad)


def _tile_rows_wait(src_ref, dst_ref, sem):
    pltpu.make_async_copy(src_ref.at[pl.ds(0, TOK_TILE), :], dst_ref, sem).wait()


def _dispatch_kernel(dest_ref, pad_ref, h_ref, xs_ref, zbuf, sem):
    base = pl.program_id(0) * (2 * TOK_TILE)

    @pl.when(pl.program_id(0) == 0)
    def _():
        zbuf[...] = jnp.zeros(zbuf.shape, F32)

        def zero_row(j, c):
            pltpu.make_async_copy(
                zbuf.at[pl.ds(0, 1), :], xs_ref.at[pl.ds(pad_ref[j], 1), :], sem).start()
            return c

        lax.fori_loop(0, MOE_PAD_ROWS, zero_row, 0, unroll=8)
        for _ in range(MOE_PAD_ROWS // TOK_TILE):
            _tile_rows_wait(h_ref, xs_ref.at[pl.ds(0, TOK_TILE), :], sem)

    def start(r, c):
        for k in range(2):
            pltpu.make_async_copy(
                h_ref.at[pl.ds(r, 1), :],
                xs_ref.at[pl.ds(dest_ref[base + 2 * r + k], 1), :], sem).start()
        return c

    lax.fori_loop(0, TOK_TILE, start, 0, unroll=8)
    for _ in range(2):
        _tile_rows_wait(h_ref, xs_ref.at[pl.ds(0, TOK_TILE), :], sem)


def _dispatch(dest, pad_rows, h):
    return pl.pallas_call(
        _dispatch_kernel,
        grid_spec=pltpu.PrefetchScalarGridSpec(
            num_scalar_prefetch=2,
            grid=(N_TOK // TOK_TILE,),
            in_specs=[pl.BlockSpec((TOK_TILE, D_MODEL), lambda i, d, p: (i, 0))],
            out_specs=pl.BlockSpec(memory_space=pl.ANY),
            scratch_shapes=[pltpu.VMEM((8, D_MODEL), F32), pltpu.SemaphoreType.DMA(())]),
        out_shape=jax.ShapeDtypeStruct((MOE_ROWS, D_MODEL), F32),
        compiler_params=pltpu.CompilerParams(dimension_semantics=("arbitrary",)),
        name="moe_dispatch",
    )(dest, pad_rows, h)


def _experts_kernel(te_ref, tv_ref, x_ref, wg_ref, wu_ref, wd_ref, o_ref):
    i = pl.program_id(0)

    @pl.when(tv_ref[i] > 0)
    def _():
        o_ref[...] = _swiglu_rows(x_ref[...].astype(BF16), wg_ref, wu_ref, wd_ref)

    @pl.when(tv_ref[i] == 0)
    def _():
        o_ref[...] = jnp.zeros((MOE_TILE, D_MODEL), F32)


def _experts(tile_expert, tile_valid, xs, wg, wu, wd):
    row = pl.BlockSpec((MOE_TILE, D_MODEL), lambda i, te, tv: (i, 0))
    return pl.pallas_call(
        _experts_kernel,
        grid_spec=pltpu.PrefetchScalarGridSpec(
            num_scalar_prefetch=2,
            grid=(MOE_ROWS // MOE_TILE,),
            in_specs=[row,
                      pl.BlockSpec((None, D_MODEL, D_FF), lambda i, te, tv: (te[i], 0, 0)),
                      pl.BlockSpec((None, D_MODEL, D_FF), lambda i, te, tv: (te[i], 0, 0)),
                      pl.BlockSpec((None, D_FF, D_MODEL), lambda i, te, tv: (te[i], 0, 0))],
            out_specs=row),
        out_shape=jax.ShapeDtypeStruct((MOE_ROWS, D_MODEL), F32),
        compiler_params=pltpu.CompilerParams(
            dimension_semantics=("arbitrary",), vmem_limit_bytes=VMEM_BIG),
        name="moe_experts",
    )(tile_expert, tile_valid, xs, wg, wu, wd)


def _combine_kernel(dest_ref, x_ref, info_ref, g_ref, ys_ref, yp_ref, ysm_ref, ybuf, sem):
    i = pl.program_id(0)
    n = pl.num_programs(0)

    def start_tile(tile):
        slot = tile % 2
        base = tile * (2 * TOK_TILE)

        def start(r, c):
            for k in range(2):
                pltpu.make_async_copy(
                    ys_ref.at[pl.ds(dest_ref[base + 2 * r + k], 1), :],
                    ybuf.at[slot, k, pl.ds(r, 1), :], sem.at[slot]).start()
            return c

        lax.fori_loop(0, TOK_TILE, start, 0, unroll=8)

    @pl.when(i == 0)
    def _():
        start_tile(i)

    @pl.when(i + 1 < n)
    def _():
        start_tile(i + 1)

    slot = i % 2
    for k in range(2):
        _tile_rows_wait(ys_ref, ybuf.at[slot, k], sem.at[slot])
    info = info_ref[...]
    x = x_ref[...] + info[:, 2:3] * ybuf[slot, 0] + info[:, 3:4] * ybuf[slot, 1]
    y = _rms(x, g_ref[...])

    @pl.when(i < PROMPT_TILES)
    def _():
        yp_ref[...] = y

    @pl.when(i == PROMPT_TILES)
    def _():
        ysm_ref[...] = y[0:DEC_BATCH, :]


def _combine(dest, x, info, g, ys):
    last = PROMPT_TILES - 1
    return pl.pallas_call(
        _combine_kernel,
        grid_spec=pltpu.PrefetchScalarGridSpec(
            num_scalar_prefetch=1,
            grid=(N_TOK // TOK_TILE,),
            in_specs=[pl.BlockSpec((TOK_TILE, D_MODEL), lambda i, d: (i, 0)),
                      pl.BlockSpec((TOK_TILE, 128), lambda i, d: (i, 0)),
                      pl.BlockSpec((1, D_MODEL), lambda i, d: (0, 0)),
                      pl.BlockSpec(memory_space=pl.ANY)],
            out_specs=[pl.BlockSpec((TOK_TILE, D_MODEL), lambda i, d: (jnp.minimum(i, last), 0)),
                       pl.BlockSpec((DEC_BATCH, D_MODEL), lambda i, d: (0, 0))],
            scratch_shapes=[pltpu.VMEM((2, 2, TOK_TILE, D_MODEL), F32),
                            pltpu.SemaphoreType.DMA((2,))]),
        out_shape=[jax.ShapeDtypeStruct((N_PROMPT, D_MODEL), F32),
                   jax.ShapeDtypeStruct((DEC_BATCH, D_MODEL), F32)],
        compiler_params=pltpu.CompilerParams(dimension_semantics=("arbitrary",)),
        name="moe_combine",
    )(dest, x, info, g, ys)


def _moe_layer(x, yb, gb, wo, g2, router_w, wg, wu, wd, final_g):
    r_pad = jnp.zeros((D_MODEL, 128), F32).at[:, :N_EXPERTS].set(router_w)
    x, h, info, cnt = _router(x, yb, gb, wo, g2, r_pad)
    n_tiles = N_TOK // TOK_TILE
    counts = cnt[:, 0, :N_EXPERTS].astype(jnp.int32)
    total = jnp.sum(counts, axis=0)
    padded = ((total + MOE_TILE - 1) // MOE_TILE) * MOE_TILE
    ends = jnp.cumsum(padded)
    starts = ends - padded
    base = starts[None, :] + jnp.cumsum(counts, axis=0) - counts
    choice = info[:, 0:2].astype(jnp.int32).reshape(n_tiles, TOK_TILE, 2)
    rank = info[:, 4:6].astype(jnp.int32).reshape(n_tiles, TOK_TILE, 2)
    onehot = choice[..., None] == jnp.arange(N_EXPERTS, dtype=jnp.int32)
    dest = (jnp.sum(jnp.where(onehot, base[:, None, None, :], 0), axis=-1) + rank).reshape(-1)
    tile_start = jnp.arange(MOE_ROWS // MOE_TILE, dtype=jnp.int32) * MOE_TILE
    tile_expert = jnp.minimum(
        jnp.sum((tile_start[:, None] >= ends[None, :]).astype(jnp.int32), axis=1), N_EXPERTS - 1)
    tile_valid = (tile_start < ends[-1]).astype(jnp.int32)
    last_used = jnp.max(jnp.where(tile_valid > 0, tile_expert, 0))
    tile_expert = jnp.where(tile_valid > 0, tile_expert, last_used).astype(jnp.int32)

    gaps = padded - total
    gap_end = jnp.cumsum(gaps)
    j = jnp.arange(MOE_PAD_ROWS, dtype=jnp.int32)
    seg = jnp.sum((j[:, None] >= gap_end[None, :]).astype(jnp.int32), axis=1)
    seg_e = jnp.minimum(seg, N_EXPERTS - 1)
    in_gap = (starts + total)[seg_e] + j - (gap_end - gaps)[seg_e]
    pad_rows = jnp.where(seg < N_EXPERTS, in_gap, ends[-1] + j - gap_end[-1]).astype(jnp.int32)

    xs = _dispatch(dest, pad_rows, h)
    ys = _experts(tile_expert, tile_valid, xs, wg, wu, wd)
    return _combine(dest, x, info, final_g, ys)


def _rearranged_w_in(w_in):
    off_qkv = 2 * D_CONV
    off_z = off_qkv + CONV_DIM
    off_a = off_z + D_DELTA_V
    off_ga = off_a + 2 * N_HEADS
    off_gb = off_ga + D_MODEL
    w_in = w_in.astype(BF16)
    return jnp.concatenate(
        [w_in[:, :off_a], w_in[:, off_ga:off_gb], w_in[:, off_gb:], w_in[:, off_a:off_ga],
         jnp.zeros((D_MODEL, 128 - 2 * N_HEADS), BF16)], axis=1)


def _lane_vec(v, offset=0):
    return jnp.zeros((1, 128), F32).at[0, offset:offset + v.shape[0]].set(v)


def kernel(x_prompt, x_sample, state_delta, state_qkv_conv, state_dwconv, norm1_g, w_in, qkv_conv_w, a_log, dt_bias, o_norm_g, w_delta_out, dw_w, dw_b, ln_g, ln_b, w_conv_out, w_o, norm2_g, ffn_w_gate, ffn_w_up, ffn_w_down, router_w, exp_w_gate, exp_w_up, exp_w_down, final_norm_g):
    x = (x_prompt.reshape(N_PROMPT, D_MODEL), x_sample.reshape(DEC_BATCH, D_MODEL))
    s_p, q_p, d_p, s_s, q_s, d_s = [], [], [], [], [], []
    for l in range(DEPTH):
        glu, qkv, z, ga, gb, ab = _norm_proj(x, norm1_g[l][None, :], _rearranged_w_in(w_in[l]))
        dwb, lng, lnb = dw_b[l][None, :], ln_g[l][None, :], ln_b[l][None, :]
        wco = w_conv_out[l].astype(BF16)
        ya_s, dst_s = _conv_sample(glu, state_dwconv[l].reshape(DEC_BATCH, -1), dw_w[l], dwb, lng, lnb, wco)
        wo = w_o[l].astype(BF16)
        x, dst_p = _conv_prompt(x, glu, ga, ya_s, dw_w[l], dwb, lng, lnb, wco, wo)
        alog, dtb, ong = _lane_vec(a_log[l]), _lane_vec(dt_bias[l]), o_norm_g[l][None, :]
        wdo = w_delta_out[l].astype(BF16)
        yb_s, qst_s, sst_s = _gdn_sample(qkv, z, ab, state_qkv_conv[l].reshape(DEC_BATCH, -1),
                                         state_delta, l, qkv_conv_w[l], alog, dtb, ong, wdo)
        rp, qg, kt, dl, m, aq, qst_p = _gdn_pre(qkv, ab, qkv_conv_w[l], alog, dtb)
        yb, sst_p = _gdn_scan(rp, qg, kt, dl, _gdn_solve(m), aq, z, yb_s, ong, wdo)
        i = l // 2
        if l % 2 == 0:
            x = _ffn(x, yb, gb, wo, norm2_g[l][None, :], ffn_w_gate[i].astype(BF16),
                     ffn_w_up[i].astype(BF16), ffn_w_down[i].astype(BF16))
        else:
            y_prompt, y_sample = _moe_layer(
                x, yb, gb, wo, norm2_g[l][None, :], router_w[i], exp_w_gate[i].astype(BF16),
                exp_w_up[i].astype(BF16), exp_w_down[i].astype(BF16), final_norm_g[None, :])
        s_p.append(sst_p)
        q_p.append(qst_p)
        d_p.append(dst_p)
        s_s.append(sst_s)
        q_s.append(qst_s.reshape(DEC_BATCH, QK_CONV - 1, CONV_DIM))
        d_s.append(dst_s.reshape(DEC_BATCH, DW_WIDTH - 1, D_CONV))
    return (y_prompt.reshape(BATCH, SEQ, D_MODEL), y_sample.reshape(DEC_BATCH, 1, D_MODEL), jnp.stack(s_p), jnp.stack(q_p), jnp.stack(d_p),
            jnp.stack(s_s), jnp.stack(q_s), jnp.stack(d_s))
```

```python
import functools

import jax
import jax.numpy as jnp
from jax import lax
from jax.experimental import pallas as pl
from jax.experimental.pallas import tpu as pltpu

F32 = jnp.float32
BF16 = jnp.bfloat16
HIGHEST = lax.Precision.HIGHEST

D_MODEL = 1024
BATCH = 8
SEQ = 2048
DEPTH = 2
DEC_BATCH = 128
D_CONV = 512
DW_WIDTH = 31
N_HEADS = 4
HEAD_K = 128
HEAD_V = 128
QK_CONV = 4
D_DELTA_K = N_HEADS * HEAD_K
D_DELTA_V = N_HEADS * HEAD_V
CONV_DIM = 2 * D_DELTA_K + D_DELTA_V
D_FF = 2816
N_EXPERTS = 8
EPS = 1e-6

N_PROMPT = BATCH * SEQ
TOK_TILE = 512
SEQ_TILE = TOK_TILE
N_TOK = N_PROMPT + TOK_TILE
SEQ_TILES = SEQ // SEQ_TILE
PROMPT_TILES = N_PROMPT // TOK_TILE
HALF_TILES = PROMPT_TILES // 2
HALF_SEQS = BATCH // 2
CHUNK = 64
M_PITCH = 68
SLABS_TILE = 2 * (SEQ_TILE // CHUNK)
M_ROWS_TILE = SLABS_TILE * M_PITCH
SOLVE_SLABS = 128
ROW_BLK = 64
MXU_DIM = 256
FF_CHUNKS = ((0, 6 * MXU_DIM), (6 * MXU_DIM, 5 * MXU_DIM))
MOE_TILE = 256
MOE_ROWS = ((2 * N_TOK + N_EXPERTS * (MOE_TILE - 1)) // MOE_TILE + 1) * MOE_TILE
MOE_PAD_ROWS = MOE_ROWS - 2 * N_TOK
assert MOE_PAD_ROWS % TOK_TILE == 0
DEC_BLK = 8

C_GLU, C_QKV, C_Z, C_GA, C_GB, C_AB = 0, 1024, 2560, 3072, 4096, 5120
N_PROJ = 5248

VMEM_BIG = 56 * 1024 * 1024


def _bdot(a, b):
    return jnp.dot(a.astype(BF16), b.astype(BF16), preferred_element_type=F32)


def _hdot(a, b):
    return jnp.dot(a, b, preferred_element_type=F32, precision=HIGHEST)


def _dot3(a, b):
    a_hi, b_hi = a.astype(BF16), b.astype(BF16)
    a_lo = (a - a_hi.astype(F32)).astype(BF16)
    b_lo = (b - b_hi.astype(F32)).astype(BF16)
    dot = functools.partial(jnp.dot, preferred_element_type=F32)
    return dot(a_hi, b_hi) + (dot(a_hi, b_lo) + dot(a_lo, b_hi))


def _rms(x, g):
    return x * lax.rsqrt(jnp.mean(x * x, axis=-1, keepdims=True) + EPS) * g


def _silu(x):
    return x * jax.nn.sigmoid(x)


def _resident(shape):
    return pl.BlockSpec(shape, lambda *_: (0,) * len(shape), pipeline_mode=pl.Buffered(1))


def _stream_x_specs(x):
    if isinstance(x, tuple):
        last = PROMPT_TILES - 1
        return ([pl.BlockSpec((TOK_TILE, D_MODEL), lambda i: (jnp.minimum(i, last), 0)),
                 pl.BlockSpec((DEC_BATCH, D_MODEL), lambda i: (0, 0))], list(x))
    return [pl.BlockSpec((TOK_TILE, D_MODEL), lambda i: (i, 0))], [x]


def _stream_x_tile(x_refs, xbuf):
    if len(x_refs) == 1:
        return x_refs[0][...]
    xp_ref, xs_ref = x_refs
    i = pl.program_id(0)

    @pl.when(i < PROMPT_TILES)
    def _():
        xbuf[...] = xp_ref[...]

    @pl.when(i == PROMPT_TILES)
    def _():
        xbuf[0:DEC_BATCH, :] = xs_ref[...]
        xbuf[DEC_BATCH:, :] = jnp.zeros((TOK_TILE - DEC_BATCH, D_MODEL), F32)

    return xbuf[...]


def _norm_proj_kernel(n_x, *refs):
    x_refs = refs[:n_x]
    g_ref, w_ref, glu_ref, qkv_ref, z_ref, ga_ref, gb_ref, ab_ref, xbuf = refs[n_x:]
    h = _rms(_stream_x_tile(x_refs, xbuf), g_ref[...]).astype(BF16)

    def proj(lo, hi):
        return jnp.dot(h, w_ref[:, lo:hi], preferred_element_type=F32)

    glu_ref[...] = proj(C_GLU, C_QKV)
    qkv_ref[...] = proj(C_QKV, C_Z)
    z_ref[...] = proj(C_Z, C_GA)
    ga_ref[...] = proj(C_GA, C_GB)
    gb_ref[...] = proj(C_GB, C_AB)
    ab_ref[...] = proj(C_AB, N_PROJ)


def _norm_proj(x, g, w_all):
    row = lambda w: pl.BlockSpec((TOK_TILE, w), lambda i: (i, 0))
    widths = (1024, CONV_DIM, D_DELTA_V, D_MODEL, D_MODEL, 128)
    x_specs, x_ops = _stream_x_specs(x)
    return pl.pallas_call(
        functools.partial(_norm_proj_kernel, len(x_ops)),
        grid=(N_TOK // TOK_TILE,),
        in_specs=x_specs + [pl.BlockSpec((1, D_MODEL), lambda i: (0, 0)),
                            _resident((D_MODEL, N_PROJ))],
        out_specs=[row(w) for w in widths],
        out_shape=[jax.ShapeDtypeStruct((N_TOK, w), F32) for w in widths],
        scratch_shapes=[pltpu.VMEM((TOK_TILE, D_MODEL), F32)],
        compiler_params=pltpu.CompilerParams(
            dimension_semantics=("arbitrary",), vmem_limit_bytes=VMEM_BIG),
        name="norm_proj",
    )(*x_ops, g, w_all)


def _gated_branch(x, gate_ref, y, wo_ref):
    return x + _bdot(jax.nn.sigmoid(gate_ref[...]) * y, wo_ref[...])


def _halves_tile_spec():
    half = lambda i: i // HALF_TILES - i // (2 * HALF_TILES)
    return pl.BlockSpec((None, TOK_TILE, D_MODEL),
                        lambda i, *_: (half(i), i - HALF_TILES * half(i), 0))


def _swiglu_rows(h, wg_ref, wu_ref, wd_ref):
    acc = None
    for lo, n in FF_CHUNKS:
        a = jnp.dot(h, wg_ref[:, lo:lo + n], preferred_element_type=F32)
        b = jnp.dot(h, wu_ref[:, lo:lo + n], preferred_element_type=F32)
        part = jnp.dot((_silu(a) * b).astype(BF16), wd_ref[lo:lo + n, :],
                       preferred_element_type=F32)
        acc = part if acc is None else acc + part
    return acc


def _ffn_kernel(x_ref, yb_ref, gb_ref, wo_ref, g_ref, wg_ref, wu_ref, wd_ref, o_ref):
    x = _gated_branch(x_ref[...], gb_ref, yb_ref[...], wo_ref)
    h = _rms(x, g_ref[...]).astype(BF16)
    o_ref[...] = x + _swiglu_rows(h, wg_ref, wu_ref, wd_ref)


def _ffn(x, yb, gb, wo, g, wg, wu, wd):
    row = pl.BlockSpec((TOK_TILE, D_MODEL), lambda i: (i, 0))
    return pl.pallas_call(
        _ffn_kernel,
        grid=(N_TOK // TOK_TILE,),
        in_specs=[row, _halves_tile_spec(), row, _resident((D_MODEL, D_MODEL)),
                  pl.BlockSpec((1, D_MODEL), lambda i: (0, 0)),
                  _resident((D_MODEL, D_FF)), _resident((D_MODEL, D_FF)),
                  _resident((D_FF, D_MODEL))],
        out_specs=row,
        out_shape=jax.ShapeDtypeStruct((N_TOK, D_MODEL), F32),
        compiler_params=pltpu.CompilerParams(
            dimension_semantics=("arbitrary",), vmem_limit_bytes=VMEM_BIG),
        name="ffn_dense",
    )(x, yb, gb, wo, g, wg, wu, wd)


def _conv_tail(c, dwb, lng, lnb):
    c = c + dwb
    mu = jnp.mean(c, axis=-1, keepdims=True)
    var = jnp.mean(jnp.square(c - mu), axis=-1, keepdims=True)
    c = (c - mu) * lax.rsqrt(var + EPS) * lng + lnb
    return _silu(c)


def _sample_rows_tile(ys_ref, y_ref):
    y_ref[0:DEC_BATCH, :] = ys_ref[...]
    y_ref[DEC_BATCH:, :] = jnp.zeros((TOK_TILE - DEC_BATCH, D_MODEL), F32)


def _conv_prompt_kernel(n_x, *refs):
    x_refs = refs[:n_x]
    (glu_ref, ga_ref, ys_ref, dww_ref, dwb_ref, lng_ref, lnb_ref, wout_ref, wo_ref,
     o_ref, st_ref, ubuf, ushift, cbuf, xbuf) = refs[n_x:]
    i = pl.program_id(0)
    if n_x == 2:
        _stream_x_tile(x_refs, xbuf)
        x_ref = xbuf
    else:
        x_ref = x_refs[0]

    @pl.when(i < PROMPT_TILES)
    def _():
        _conv_prompt_tile(i % SEQ_TILES, glu_ref, dww_ref, dwb_ref, lng_ref, lnb_ref,
                          st_ref, ubuf, ushift, cbuf)
        ya = jnp.dot(cbuf[...], wout_ref[...], preferred_element_type=F32)
        o_ref[...] = _gated_branch(x_ref[...], ga_ref, ya, wo_ref)

    @pl.when(i == PROMPT_TILES)
    def _():
        ya = jnp.concatenate(
            [ys_ref[...], jnp.zeros((TOK_TILE - DEC_BATCH, D_MODEL), F32)], axis=0)
        o_ref[...] = _gated_branch(x_ref[...], ga_ref, ya, wo_ref)


def _conv_prompt_tile(t, glu_ref, dww_ref, dwb_ref, lng_ref, lnb_ref, st_ref, ubuf, ushift, cbuf):
    halo = 32

    @pl.when(t == 0)
    def _():
        ubuf[0:halo, :] = jnp.zeros((halo, D_CONV), F32)

    @pl.when(t > 0)
    def _():
        ubuf[0:halo, :] = ubuf[SEQ_TILE:SEQ_TILE + halo, :]

    glu = glu_ref[...]
    ubuf[halo:halo + SEQ_TILE, :] = glu[:, :D_CONV] * jax.nn.sigmoid(glu[:, D_CONV:])
    for r in range(1, 8):
        ushift[r - 1, 0:SEQ_TILE + halo - 8, :] = ubuf[r:r + SEQ_TILE + halo - 8, :]
    w = dww_ref[...]
    first = halo - (DW_WIDTH - 1)
    for rb in range(SEQ_TILE // ROW_BLK):
        r0 = rb * ROW_BLK
        acc = jnp.zeros((ROW_BLK, D_CONV), F32)
        for j in range(DW_WIDTH):
            a, r = divmod(first + j, 8)
            if r == 0:
                win = ubuf[r0 + 8 * a:r0 + 8 * a + ROW_BLK, :]
            else:
                win = ushift[r - 1, r0 + 8 * a:r0 + 8 * a + ROW_BLK, :]
            acc = acc + w[j:j + 1, :] * win
        c = _conv_tail(acc, dwb_ref[...], lng_ref[...], lnb_ref[...])
        cbuf[r0:r0 + ROW_BLK, :] = c.astype(BF16)

    @pl.when(t == SEQ_TILES - 1)
    def _():
        st_ref[...] = ubuf[halo + SEQ_TILE - (DW_WIDTH - 1):halo + SEQ_TILE, :]


def _seq_of_tile(i):
    return jnp.minimum(i // SEQ_TILES, BATCH - 1)


def _conv_prompt(x, glu, ga, ya_sample, dww, dwb, lng, lnb, wout, wo):
    vec = pl.BlockSpec((1, D_CONV), lambda i: (0, 0))
    row = pl.BlockSpec((SEQ_TILE, D_MODEL), lambda i: (i, 0))
    x_specs, x_ops = _stream_x_specs(x)
    return pl.pallas_call(
        functools.partial(_conv_prompt_kernel, len(x_ops)),
        grid=(PROMPT_TILES + 1,),
        in_specs=x_specs + [
            pl.BlockSpec((SEQ_TILE, 2 * D_CONV), lambda i: (i, 0)), row,
            pl.BlockSpec((DEC_BATCH, D_MODEL), lambda i: (0, 0)),
            pl.BlockSpec((DW_WIDTH, D_CONV), lambda i: (0, 0)),
            vec, vec, vec,
            pl.BlockSpec((D_CONV, D_MODEL), lambda i: (0, 0)),
            pl.BlockSpec((D_MODEL, D_MODEL), lambda i: (0, 0))],
        out_specs=[row,
                   pl.BlockSpec((None, DW_WIDTH - 1, D_CONV), lambda i: (_seq_of_tile(i), 0, 0))],
        out_shape=[jax.ShapeDtypeStruct((N_TOK, D_MODEL), F32),
                   jax.ShapeDtypeStruct((BATCH, DW_WIDTH - 1, D_CONV), F32)],
        scratch_shapes=[pltpu.VMEM((SEQ_TILE + 32, D_CONV), F32),
                        pltpu.VMEM((7, SEQ_TILE + 24, D_CONV), F32),
                        pltpu.VMEM((SEQ_TILE, D_CONV), BF16),
                        pltpu.VMEM((TOK_TILE, D_MODEL), F32)],
        compiler_params=pltpu.CompilerParams(
            dimension_semantics=("arbitrary",), vmem_limit_bytes=VMEM_BIG),
        name="conv_prompt",
    )(*x_ops, glu, ga, ya_sample, dww, dwb, lng, lnb, wout, wo)


def _conv_sample_kernel(glu_ref, st_ref, dww_ref, dwb_ref, lng_ref, lnb_ref, wout_ref,
                        ya_ref, nst_ref):
    glu = glu_ref[...]
    u = glu[:, :D_CONV] * jax.nn.sigmoid(glu[:, D_CONV:])
    w = dww_ref[...]
    nbuf = DW_WIDTH - 1
    acc = w[nbuf:nbuf + 1, :] * u
    for j in range(nbuf):
        acc = acc + w[j:j + 1, :] * st_ref[:, j, :]
    c = _conv_tail(acc, dwb_ref[...], lng_ref[...], lnb_ref[...])
    ya_ref[...] = _bdot(c, wout_ref[...])
    nst_ref[:, 0:nbuf - 1, :] = st_ref[:, 1:nbuf, :]
    nst_ref[:, nbuf - 1, :] = u


def _conv_sample(glu, state_dwconv, layer, dww, dwb, lng, lnb, wout):
    nbuf = DW_WIDTH - 1
    blk = N_PROMPT // DEC_BATCH
    vec = pl.BlockSpec((1, D_CONV), lambda i: (0, 0))
    return pl.pallas_call(
        _conv_sample_kernel,
        grid=(1,),
        in_specs=[pl.BlockSpec((DEC_BATCH, 2 * D_CONV), lambda i: (blk, 0)),
                  pl.BlockSpec((None, DEC_BATCH, nbuf, D_CONV), lambda i: (layer, 0, 0, 0)),
                  pl.BlockSpec((DW_WIDTH, D_CONV), lambda i: (0, 0)),
                  vec, vec, vec,
                  pl.BlockSpec((D_CONV, D_MODEL), lambda i: (0, 0))],
        out_specs=[pl.BlockSpec((DEC_BATCH, D_MODEL), lambda i: (0, 0)),
                   pl.BlockSpec((DEC_BATCH, nbuf, D_CONV), lambda i: (0, 0, 0))],
        out_shape=[jax.ShapeDtypeStruct((DEC_BATCH, D_MODEL), F32),
                   jax.ShapeDtypeStruct((DEC_BATCH, nbuf, D_CONV), F32)],
        compiler_params=pltpu.CompilerParams(dimension_semantics=("arbitrary",)),
        name="conv_sample",
    )(glu, state_dwconv, dww, dwb, lng, lnb, wout)


def _qkv_act(c):
    c = _silu(c)
    qs, ks, vs = [], [], []
    for h in range(N_HEADS):
        q = c[:, h * HEAD_K:(h + 1) * HEAD_K]
        k = c[:, D_DELTA_K + h * HEAD_K:D_DELTA_K + (h + 1) * HEAD_K]
        qs.append(q * lax.rsqrt(jnp.sum(q * q, axis=-1, keepdims=True) + EPS) * (HEAD_K ** -0.5))
        ks.append(k * lax.rsqrt(jnp.sum(k * k, axis=-1, keepdims=True) + EPS))
        vs.append(c[:, 2 * D_DELTA_K + h * HEAD_V:2 * D_DELTA_K + (h + 1) * HEAD_V])
    return qs, ks, vs


def _gate_norm(o, ong, z):
    o = o * lax.rsqrt(jnp.mean(o * o, axis=-1, keepdims=True) + EPS) * ong
    return o * _silu(z)


def _gdn_pre_kernel(qkv_ref, ab_ref, cw_ref, alog_ref, dtb_ref,
                    rp_ref, qg_ref, kt_ref, dl_ref, m_ref, aq_ref, sq_ref, xbuf):
    t = pl.program_id(0) % SEQ_TILES
    halo = 8

    @pl.when(t == 0)
    def _():
        xbuf[0:halo, :] = jnp.zeros((halo, CONV_DIM), F32)

    @pl.when(t > 0)
    def _():
        xbuf[0:halo, :] = xbuf[SEQ_TILE:SEQ_TILE + halo, :]

    xbuf[halo:halo + SEQ_TILE, :] = qkv_ref[...]

    @pl.when(t == SEQ_TILES - 1)
    def _():
        sq_ref[...] = xbuf[halo + SEQ_TILE - (QK_CONV - 1):halo + SEQ_TILE, :]

    ab = ab_ref[...]
    lane = lax.broadcasted_iota(jnp.int32, (SEQ_TILE, 128), 1)
    gb = jnp.where(lane < N_HEADS,
                   -jnp.exp(alog_ref[...]) * jax.nn.softplus(ab + dtb_ref[...]),
                   jax.nn.sigmoid(ab))
    cw = cw_ref[...]
    first = halo - (QK_CONV - 1)
    ri = lax.broadcasted_iota(jnp.int32, (CHUNK, 128), 0)
    li = lax.broadcasted_iota(jnp.int32, (CHUNK, 128), 1)
    left = li < CHUNK
    jj = jnp.where(left, li, li - CHUNK)
    strict = ri > jj
    tril = ri >= jj
    r64 = lax.broadcasted_iota(jnp.int32, (CHUNK, CHUNK), 0)
    c64 = lax.broadcasted_iota(jnp.int32, (CHUNK, CHUNK), 1)
    csum = jnp.where(r64 >= c64, 1.0, 0.0).astype(F32)
    for c in range(SEQ_TILE // CHUNK):
        r0 = c * CHUNK
        acc = cw[0:1, :] * xbuf[r0 + first:r0 + first + CHUNK, :]
        for j in range(1, QK_CONV):
            acc = acc + cw[j:j + 1, :] * xbuf[r0 + first + j:r0 + first + j + CHUNK, :]
        qs, ks, vs = _qkv_act(acc)
        gbc = gb[r0:r0 + CHUNK, :]
        gc = _hdot(csum, gbc)
        gct = gc.T
        eg = jnp.exp(gc)
        glast = gc[CHUNK - 1:CHUNK, :]
        kdec = jnp.exp(glast - gc)
        dl = jnp.exp(glast)
        for hp in range(N_HEADS // 2):
            h0, h1 = 2 * hp, 2 * hp + 1
            slab = 2 * c + hp
            kb, rhs, qg, kd = [], [], [], []
            for h in (h0, h1):
                beta = gbc[:, N_HEADS + h:N_HEADS + h + 1]
                egh = eg[:, h:h + 1]
                kb.append(ks[h] * beta)
                rhs.append(jnp.concatenate([vs[h] * beta, kb[-1] * egh], axis=1))
                qg.append(qs[h] * egh)
                kd.append(ks[h] * kdec[:, h:h + 1])
            rows = slice(slab * 2 * CHUNK, (slab + 1) * 2 * CHUNK)
            rp_ref[rows, :] = jnp.concatenate(rhs, axis=0).astype(BF16)
            qg_ref[rows, :] = jnp.concatenate(qg, axis=0).astype(BF16)
            kt_ref[rows, :] = jnp.concatenate(kd, axis=0).T.astype(BF16)
            dl_ref[slab * 8:(slab + 1) * 8, :] = jnp.concatenate(
                [jnp.broadcast_to(dl[:, h0:h0 + 1], (8, HEAD_V)),
                 jnp.broadcast_to(dl[:, h1:h1 + 1], (8, HEAD_V))], axis=1)
            lhs = jnp.concatenate(kb + [qs[h0], qs[h1]], axis=0)
            keys = jnp.concatenate([ks[h0], ks[h1]], axis=0)
            prod = lax.dot_general(lhs.astype(BF16), keys.astype(BF16), (((1,), (1,)), ((), ())),
                                   preferred_element_type=F32)
            kk = jnp.where(left, prod[0:CHUNK], prod[CHUNK:2 * CHUNK])
            qk = jnp.where(left, prod[2 * CHUNK:3 * CHUNK], prod[3 * CHUNK:4 * CHUNK])
            gcol = jnp.where(left, gc[:, h0:h0 + 1], gc[:, h1:h1 + 1])
            grow = jnp.concatenate([gct[h0:h0 + 1, :], gct[h1:h1 + 1, :]], axis=1)
            diff = gcol - grow
            m_ref[slab * M_PITCH:slab * M_PITCH + CHUNK, :] = jnp.where(
                strict, kk * jnp.exp(jnp.where(strict, diff, 0.0)), 0.0)
            m_ref[slab * M_PITCH + CHUNK:(slab + 1) * M_PITCH, :] = jnp.zeros(
                (M_PITCH - CHUNK, 128), F32)
            aq_ref[slab * CHUNK:(slab + 1) * CHUNK, :] = jnp.where(
                tril, qk * jnp.exp(jnp.where(tril, diff, 0.0)), 0.0).astype(BF16)


def _gdn_pre(qkv, ab, cw, alog, dtb):
    tile = lambda w: pl.BlockSpec((SEQ_TILE, w), lambda i: (i, 0))
    vec = pl.BlockSpec((1, 128), lambda i: (0, 0))
    slab_rows = SLABS_TILE * 2 * CHUNK
    per_tile = lambda rows, w: pl.BlockSpec((rows, w), lambda i: (i, 0))
    return pl.pallas_call(
        _gdn_pre_kernel,
        grid=(PROMPT_TILES,),
        in_specs=[tile(CONV_DIM), tile(128),
                  pl.BlockSpec((QK_CONV, CONV_DIM), lambda i: (0, 0)), vec, vec],
        out_specs=[per_tile(slab_rows, 2 * HEAD_V), per_tile(slab_rows, HEAD_K),
                   per_tile(slab_rows, 128), per_tile(SLABS_TILE * 8, 2 * HEAD_V),
                   per_tile(M_ROWS_TILE, 128), per_tile(SLABS_TILE * CHUNK, 128),
                   pl.BlockSpec((None, QK_CONV - 1, CONV_DIM), lambda i: (i // SEQ_TILES, 0, 0))],
        out_shape=[jax.ShapeDtypeStruct((PROMPT_TILES * slab_rows, 2 * HEAD_V), BF16),
                   jax.ShapeDtypeStruct((PROMPT_TILES * slab_rows, HEAD_K), BF16),
                   jax.ShapeDtypeStruct((PROMPT_TILES * slab_rows, 128), BF16),
                   jax.ShapeDtypeStruct((PROMPT_TILES * SLABS_TILE * 8, 2 * HEAD_V), F32),
                   jax.ShapeDtypeStruct((PROMPT_TILES * M_ROWS_TILE, 128), F32),
                   jax.ShapeDtypeStruct((PROMPT_TILES * SLABS_TILE * CHUNK, 128), BF16),
                   jax.ShapeDtypeStruct((BATCH, QK_CONV - 1, CONV_DIM), F32)],
        scratch_shapes=[pltpu.VMEM((SEQ_TILE + 8, CONV_DIM), F32)],
        compiler_params=pltpu.CompilerParams(dimension_semantics=("arbitrary",)),
        name="gdn_pre",
    )(qkv, ab, cw, alog, dtb)


def _gdn_solve_kernel(m_ref, t_ref, mt, xt):
    def to_lanes(i, c):
        g = m_ref[pl.ds(i, SOLVE_SLABS, stride=M_PITCH), :]
        mt[pl.ds(pl.multiple_of(i * 128, 128), 128), :] = g.T
        return c

    lax.fori_loop(0, CHUNK, to_lanes, 0, unroll=4)

    sub = lax.broadcasted_iota(jnp.int32, (8, SOLVE_SLABS), 0)
    zeros8 = jnp.zeros((8, SOLVE_SLABS), F32)

    for ib in range(CHUNK // 8):
        def solve_row(i, c, ib=ib):
            base = pl.multiple_of(i * 128, 128)
            unit = jnp.where(sub + 8 * ib == i, 1.0, 0.0).astype(F32)
            acc = tuple([zeros8] * ib + [unit]) * 2

            def columns(groups):
                def step(j, acc):
                    xb = pl.multiple_of(j * 128, 128)
                    m0 = mt[pl.ds(base + j, 1), :]
                    m1 = mt[pl.ds(base + CHUNK + j, 1), :]
                    acc = list(acc)
                    for k in range(groups):
                        acc[k] = acc[k] - m0 * xt[pl.ds(xb + 8 * k, 8), :]
                        acc[ib + 1 + k] = acc[ib + 1 + k] - m1 * xt[pl.ds(xb + CHUNK + 8 * k, 8), :]
                    return tuple(acc)
                return step

            for jb in range(ib):
                acc = lax.fori_loop(8 * jb, 8 * jb + 8, columns(jb + 1), acc, unroll=True)
            acc = lax.fori_loop(8 * ib, i, columns(ib + 1), acc)
            for k in range(CHUNK // 8):
                xt[pl.ds(base + 8 * k, 8), :] = acc[k] if k <= ib else zeros8
                xt[pl.ds(base + CHUNK + 8 * k, 8), :] = acc[ib + 1 + k] if k <= ib else zeros8
            return c

        lax.fori_loop(8 * ib, 8 * ib + 8, solve_row, 0)

    def from_lanes(i, c):
        x = xt[pl.ds(pl.multiple_of(i * 128, 128), 128), :]
        t_ref[pl.ds(i, SOLVE_SLABS, stride=M_PITCH), :] = x.T
        return c

    lax.fori_loop(0, CHUNK, from_lanes, 0, unroll=4)
    for k in range(CHUNK, M_PITCH):
        t_ref[pl.ds(k, SOLVE_SLABS, stride=M_PITCH), :] = jnp.zeros((SOLVE_SLABS, 128), F32)


def _gdn_solve(m):
    rows = SOLVE_SLABS * M_PITCH
    return pl.pallas_call(
        _gdn_solve_kernel,
        grid=(PROMPT_TILES * SLABS_TILE // SOLVE_SLABS,),
        in_specs=[pl.BlockSpec((rows, 128), lambda i: (i, 0))],
        out_specs=pl.BlockSpec((rows, 128), lambda i: (i, 0)),
        out_shape=jax.ShapeDtypeStruct(m.shape, F32),
        scratch_shapes=[pltpu.VMEM((CHUNK * 128, SOLVE_SLABS), F32),
                        pltpu.VMEM((CHUNK * 128, SOLVE_SLABS), F32)],
        compiler_params=pltpu.CompilerParams(dimension_semantics=("arbitrary",)),
        name="gdn_solve",
    )(m)


def _gdn_scan_kernel(rp_ref, qg_ref, kt_ref, dl_ref, tm_ref, aq_ref, z0_ref, z1_ref, ys_ref,
                     ong_ref, wout_ref, yb_ref, ss_ref, obuf, s_scr, uw_scr, lhs_scr, b_scr, d_scr):
    i = pl.program_id(0)
    pl.when(i < HALF_TILES)(functools.partial(
        _gdn_scan_tile, i % SEQ_TILES, rp_ref, qg_ref, kt_ref, dl_ref, tm_ref, aq_ref,
        (z0_ref, z1_ref), ong_ref, wout_ref, yb_ref, ss_ref, obuf, s_scr, uw_scr, lhs_scr, b_scr, d_scr))

    @pl.when(i == HALF_TILES)
    def _():
        yb_ref[0] = jnp.zeros((TOK_TILE, D_MODEL), F32)
        _sample_rows_tile(ys_ref, yb_ref.at[1])


def _gdn_scan_tile(t, rp_ref, qg_ref, kt_ref, dl_ref, tm_ref, aq_ref, z_refs, ong_ref, wout_ref,
                   yb_ref, ss_ref, obuf, s_scr, uw_scr, lhs_scr, b_scr, d_scr):
    @pl.when(t == 0)
    def _():
        s_scr[...] = jnp.zeros((2, N_HEADS // 2, HEAD_K, 2 * HEAD_V), F32)

    left = lax.broadcasted_iota(jnp.int32, (CHUNK, 128), 1) < CHUNK
    ong = ong_ref[...]
    zero = jnp.zeros((CHUNK, HEAD_V), BF16)
    pairs = N_HEADS // 2
    n_chunks = SEQ_TILE // CHUNK

    def pair_diag(x):
        x = x.astype(BF16)
        none = jnp.zeros_like(x)
        return jnp.concatenate([jnp.where(left, x, none), jnp.where(left, none, x)], axis=0)

    def head_diag(x):
        return jnp.concatenate([jnp.concatenate([x[0:CHUNK], zero], axis=1),
                                jnp.concatenate([zero, x[CHUNK:]], axis=1)], axis=0)

    dot = functools.partial(jnp.dot, preferred_element_type=F32)
    for c in range(n_chunks):
        for half in range(2):
            for hp in range(pairs):
                slab = 2 * c + hp
                n = (c * 2 + half) * pairs + hp
                rows = slice(slab * 2 * CHUNK, (slab + 1) * 2 * CHUNK)
                tm = pair_diag(tm_ref[half, slab * M_PITCH:slab * M_PITCH + CHUNK, :])
                uw_scr[n] = dot(tm, rp_ref[half, rows, :]).astype(BF16)

    for c in range(n_chunks):
        for half in range(2):
            for hp in range(pairs):
                slab = 2 * c + hp
                n = (c * 2 + half) * pairs + hp
                rows = slice(slab * 2 * CHUNK, (slab + 1) * 2 * CHUNK)
                aq = pair_diag(aq_ref[half, slab * CHUNK:(slab + 1) * CHUNK, :])
                ub = uw_scr[n, :, :HEAD_V]
                wb = uw_scr[n, :, HEAD_V:]
                gb = dot(kt_ref[half, rows, :],
                         jnp.concatenate([head_diag(wb), head_diag(ub)], axis=1))
                aw_au = dot(aq, jnp.concatenate([wb, ub], axis=1))
                cm = qg_ref[half, rows, :].astype(F32) - aw_au[:, :HEAD_V]
                lhs_scr[n] = jnp.concatenate(
                    [gb[:, 0:HEAD_V], gb[:, HEAD_V:2 * HEAD_V], cm], axis=0).astype(BF16)
                b_scr[n] = gb[:, 2 * HEAD_V:]
                d_scr[n] = aw_au[:, HEAD_V:]

    for c in range(n_chunks):
        r0 = c * CHUNK
        for half in range(2):
            for hp in range(pairs):
                slab = 2 * c + hp
                n = (c * 2 + half) * pairs + hp
                s = s_scr[half, hp]
                r = dot(lhs_scr[n], s.astype(BF16))
                gs = jnp.concatenate([r[0:HEAD_K, 0:HEAD_V], r[HEAD_K:2 * HEAD_K, HEAD_V:]], axis=1)
                s_scr[half, hp] = s * dl_ref[half, slab * 8:slab * 8 + 1, :] - gs + b_scr[n]
                o = jnp.concatenate([r[2 * HEAD_K:2 * HEAD_K + CHUNK, 0:HEAD_V],
                                     r[2 * HEAD_K + CHUNK:, HEAD_V:]], axis=0) + d_scr[n]
                for k in range(2):
                    h = 2 * hp + k
                    zh = z_refs[half][r0:r0 + CHUNK, h * HEAD_V:(h + 1) * HEAD_V]
                    obuf[half, r0:r0 + CHUNK, h * HEAD_V:(h + 1) * HEAD_V] = _gate_norm(
                        o[k * CHUNK:(k + 1) * CHUNK], ong, zh)

    for half in range(2):
        yb_ref[half] = _bdot(obuf[half], wout_ref[...])

    @pl.when(t == SEQ_TILES - 1)
    def _():
        for half in range(2):
            for h in range(N_HEADS):
                ss_ref[half, h] = s_scr[half, h // 2, :, (h % 2) * HEAD_V:(h % 2 + 1) * HEAD_V]


def _gdn_scan(rp, qg, kt, dl, tm, aq, z, yb_sample, ong, wout):
    last = HALF_TILES - 1
    halves = lambda a: a.reshape((2, a.shape[0] // 2) + a.shape[1:])
    both = lambda a: pl.BlockSpec((2, a.shape[0] // PROMPT_TILES, a.shape[1]),
                                  lambda i: (0, jnp.minimum(i, last), 0))
    ztile = lambda off: pl.BlockSpec((SEQ_TILE, D_DELTA_V),
                                     lambda i: (jnp.minimum(i, last) + off, 0))
    ops = (rp, qg, kt, dl, tm, aq)
    n_steps = SEQ_TILE // CHUNK * 2 * (N_HEADS // 2)
    yb, ss = pl.pallas_call(
        _gdn_scan_kernel,
        grid=(HALF_TILES + 1,),
        in_specs=[both(a) for a in ops] + [
            ztile(0), ztile(HALF_TILES),
            pl.BlockSpec((DEC_BATCH, D_MODEL), lambda i: (0, 0)),
            pl.BlockSpec((1, 128), lambda i: (0, 0)),
            pl.BlockSpec((D_DELTA_V, D_MODEL), lambda i: (0, 0))],
        out_specs=[pl.BlockSpec((2, SEQ_TILE, D_MODEL), lambda i: (0, i, 0)),
                   pl.BlockSpec((2, None, N_HEADS, HEAD_K, HEAD_V),
                                lambda i: (0, jnp.minimum(i // SEQ_TILES, HALF_SEQS - 1), 0, 0, 0))],
        out_shape=[jax.ShapeDtypeStruct((2, (HALF_TILES + 1) * TOK_TILE, D_MODEL), F32),
                   jax.ShapeDtypeStruct((2, HALF_SEQS, N_HEADS, HEAD_K, HEAD_V), F32)],
        scratch_shapes=[pltpu.VMEM((2, SEQ_TILE, D_DELTA_V), F32),
                        pltpu.VMEM((2, N_HEADS // 2, HEAD_K, 2 * HEAD_V), F32),
                        pltpu.VMEM((n_steps, 2 * CHUNK, 2 * HEAD_V), BF16),
                        pltpu.VMEM((n_steps, 2 * HEAD_K + 2 * CHUNK, HEAD_K), BF16),
                        pltpu.VMEM((n_steps, HEAD_K, 2 * HEAD_V), F32),
                        pltpu.VMEM((n_steps, 2 * CHUNK, HEAD_V), F32)],
        compiler_params=pltpu.CompilerParams(
            dimension_semantics=("arbitrary",), vmem_limit_bytes=VMEM_BIG),
        name="gdn_scan",
    )(*[halves(a) for a in ops], z, z, yb_sample, ong, wout)
    return yb, ss.reshape(BATCH, N_HEADS, HEAD_K, HEAD_V)


def _gdn_sample_kernel(qkv_ref, z_ref, ab_ref, st_ref, s_ref, cw_ref, alog_ref, dtb_ref,
                       ong_ref, wout_ref, yb_ref, nst_ref, ns_ref):
    x = qkv_ref[...]
    cw = cw_ref[...]
    nbuf = QK_CONV - 1
    acc = cw[nbuf:nbuf + 1, :] * x
    for j in range(nbuf):
        acc = acc + cw[j:j + 1, :] * st_ref[:, j * CONV_DIM:(j + 1) * CONV_DIM]
    nst_ref[:, 0:(nbuf - 1) * CONV_DIM] = st_ref[:, CONV_DIM:nbuf * CONV_DIM]
    nst_ref[:, (nbuf - 1) * CONV_DIM:nbuf * CONV_DIM] = x
    qs, ks, vs = _qkv_act(acc)
    ab = ab_ref[...]
    g = -jnp.exp(alog_ref[...]) * jax.nn.softplus(ab + dtb_ref[...])
    beta = jax.nn.sigmoid(ab)
    eg = jnp.exp(g)
    z = z_ref[...]
    outs = []
    for h in range(N_HEADS):
        bh = beta[:, N_HEADS + h:N_HEADS + h + 1]
        egh = eg[:, h:h + 1]
        kb = ks[h] * bh
        w = kb * egh
        qg = qs[h] * egh
        u = vs[h] * bh
        qk = jnp.sum(qs[h] * ks[h], axis=-1, keepdims=True)
        o_rows = []
        for s in range(DEC_BLK):
            st = s_ref[s, h]
            wq = _bdot(jnp.concatenate([w[s:s + 1], qg[s:s + 1]], axis=0), st)
            vnew = u[s:s + 1] - wq[0:1]
            o_rows.append(wq[1:2] + qk[s:s + 1] * vnew)
            ns_ref[s, h] = st * egh[s:s + 1] + lax.dot_general(
                ks[h][s:s + 1].astype(BF16), vnew.astype(BF16), (((0,), (0,)), ((), ())),
                preferred_element_type=F32)
        o = jnp.concatenate(o_rows, axis=0)
        outs.append(_gate_norm(o, ong_ref[...], z[:, h * HEAD_V:(h + 1) * HEAD_V]))
    yb_ref[...] = _bdot(jnp.concatenate(outs, axis=1), wout_ref[...])


def _gdn_sample(qkv, z, ab, st, state_delta, layer, cw, alog, dtb, ong, wout):
    nbuf = QK_CONV - 1
    blk0 = N_PROMPT // DEC_BLK
    rows = lambda w: pl.BlockSpec((DEC_BLK, w), lambda i: (blk0 + i, 0))
    vec = pl.BlockSpec((1, 128), lambda i: (0, 0))
    sblk = pl.BlockSpec((DEC_BLK, N_HEADS, HEAD_K, HEAD_V), lambda i: (i, 0, 0, 0))
    return pl.pallas_call(
        _gdn_sample_kernel,
        grid=(DEC_BATCH // DEC_BLK,),
        in_specs=[rows(CONV_DIM), rows(D_DELTA_V), rows(128),
                  pl.BlockSpec((DEC_BLK, nbuf * CONV_DIM), lambda i: (i, 0)),
                  pl.BlockSpec((None, DEC_BLK, N_HEADS, HEAD_K, HEAD_V),
                               lambda i: (layer, i, 0, 0, 0)),
                  pl.BlockSpec((QK_CONV, CONV_DIM), lambda i: (0, 0)),
                  vec, vec, vec,
                  pl.BlockSpec((D_DELTA_V, D_MODEL), lambda i: (0, 0))],
        out_specs=[pl.BlockSpec((DEC_BLK, D_MODEL), lambda i: (i, 0)),
                   pl.BlockSpec((DEC_BLK, nbuf * CONV_DIM), lambda i: (i, 0)),
                   sblk],
        out_shape=[jax.ShapeDtypeStruct((DEC_BATCH, D_MODEL), F32),
                   jax.ShapeDtypeStruct((DEC_BATCH, nbuf * CONV_DIM), F32),
                   jax.ShapeDtypeStruct((DEC_BATCH, N_HEADS, HEAD_K, HEAD_V), F32)],
        compiler_params=pltpu.CompilerParams(dimension_semantics=("arbitrary",)),
        name="gdn_sample",
    )(qkv, z, ab, st, state_delta, cw, alog, dtb, ong, wout)


def _router_kernel(x_ref, yb_ref, gb_ref, wo_ref, g_ref, r_ref, xo_ref, h_ref, info_ref, cnt_ref):
    x = _gated_branch(x_ref[...], gb_ref, yb_ref[...], wo_ref)
    xo_ref[...] = x
    h = _rms(x, g_ref[...])
    h_ref[...] = h
    logits = _dot3(h, r_ref[...])
    lane = lax.broadcasted_iota(jnp.int32, logits.shape, 1)
    neg = jnp.float32(-jnp.inf)
    logits = jnp.where(lane < N_EXPERTS, logits, neg)
    m1 = jnp.max(logits, axis=-1, keepdims=True)
    i1 = jnp.min(jnp.where(logits == m1, lane, 128), axis=-1, keepdims=True)
    rest = jnp.where(lane == i1, neg, logits)
    m2 = jnp.max(rest, axis=-1, keepdims=True)
    i2 = jnp.min(jnp.where(rest == m2, lane, 128), axis=-1, keepdims=True)
    e = jnp.exp(m2 - m1)
    p1 = 1.0 / (1.0 + e)
    p2 = e / (1.0 + e)
    oh1 = jnp.where(lane == i1, 1.0, 0.0).astype(F32)
    oh2 = jnp.where(lane == i2, 1.0, 0.0).astype(F32)
    tr = lax.broadcasted_iota(jnp.int32, (TOK_TILE, TOK_TILE), 0)
    tc = lax.broadcasted_iota(jnp.int32, (TOK_TILE, TOK_TILE), 1)
    cum = _bdot(jnp.where(tr >= tc, 1.0, 0.0), oh1 + oh2)
    rank = cum - (oh1 + oh2)
    r1 = jnp.sum(oh1 * rank, axis=-1, keepdims=True)
    r2 = jnp.sum(oh2 * rank, axis=-1, keepdims=True)
    info = jnp.where(lane == 0, i1.astype(F32), 0.0)
    info = jnp.where(lane == 1, i2.astype(F32), info)
    info = jnp.where(lane == 2, p1, info)
    info = jnp.where(lane == 3, p2, info)
    info = jnp.where(lane == 4, r1, info)
    info = jnp.where(lane == 5, r2, info)
    info_ref[...] = info
    cnt_ref[...] = jnp.broadcast_to(cum[TOK_TILE - 1:TOK_TILE, :], (8, 128))


def _router(x, yb, gb, wo, g, r_pad):
    row = lambda w: pl.BlockSpec((TOK_TILE, w), lambda i: (i, 0))
    return pl.pallas_call(
        _router_kernel,
        grid=(N_TOK // TOK_TILE,),
        in_specs=[row(D_MODEL), _halves_tile_spec(), row(D_MODEL),
                  pl.BlockSpec((D_MODEL, D_MODEL), lambda i: (0, 0)),
                  pl.BlockSpec((1, D_MODEL), lambda i: (0, 0)),
                  pl.BlockSpec((D_MODEL, 128), lambda i: (0, 0))],
        out_specs=[row(D_MODEL), row(D_MODEL), row(128),
                   pl.BlockSpec((None, 8, 128), lambda i: (i, 0, 0))],
        out_shape=[jax.ShapeDtypeStruct((N_TOK, D_MODEL), F32),
                   jax.ShapeDtypeStruct((N_TOK, D_MODEL), F32),
                   jax.ShapeDtypeStruct((N_TOK, 128), F32),
                   jax.ShapeDtypeStruct((N_TOK // TOK_TILE, 8, 128), F32)],
        compiler_params=pltpu.CompilerParams(
            dimension_semantics=("arbitrary",), vmem_limit_bytes=VMEM_BIG),
        name="moe_router",
    )(x, yb, gb, wo, g, r_pad)


def _tile_rows_wait(src_ref, dst_ref, sem):
    pltpu.make_async_copy(src_ref.at[pl.ds(0, TOK_TILE), :], dst_ref, sem).wait()


def _dispatch_kernel(dest_ref, pad_ref, h_ref, xs_ref, zbuf, sem):
    base = pl.program_id(0) * (2 * TOK_TILE)

    @pl.when(pl.program_id(0) == 0)
    def _():
        zbuf[...] = jnp.zeros(zbuf.shape, F32)

        def zero_row(j, c):
            pltpu.make_async_copy(
                zbuf.at[pl.ds(0, 1), :], xs_ref.at[pl.ds(pad_ref[j], 1), :], sem).start()
            return c

        lax.fori_loop(0, MOE_PAD_ROWS, zero_row, 0, unroll=8)
        for _ in range(MOE_PAD_ROWS // TOK_TILE):
            _tile_rows_wait(h_ref, xs_ref.at[pl.ds(0, TOK_TILE), :], sem)

    def start(r, c):
        for k in range(2):
            pltpu.make_async_copy(
                h_ref.at[pl.ds(r, 1), :],
                xs_ref.at[pl.ds(dest_ref[base + 2 * r + k], 1), :], sem).start()
        return c

    lax.fori_loop(0, TOK_TILE, start, 0, unroll=8)
    for _ in range(2):
        _tile_rows_wait(h_ref, xs_ref.at[pl.ds(0, TOK_TILE), :], sem)


def _dispatch(dest, pad_rows, h):
    return pl.pallas_call(
        _dispatch_kernel,
        grid_spec=pltpu.PrefetchScalarGridSpec(
            num_scalar_prefetch=2,
            grid=(N_TOK // TOK_TILE,),
            in_specs=[pl.BlockSpec((TOK_TILE, D_MODEL), lambda i, d, p: (i, 0))],
            out_specs=pl.BlockSpec(memory_space=pl.ANY),
            scratch_shapes=[pltpu.VMEM((8, D_MODEL), F32), pltpu.SemaphoreType.DMA(())]),
        out_shape=jax.ShapeDtypeStruct((MOE_ROWS, D_MODEL), F32),
        compiler_params=pltpu.CompilerParams(dimension_semantics=("arbitrary",)),
        name="moe_dispatch",
    )(dest, pad_rows, h)


def _experts_kernel(te_ref, tv_ref, x_ref, wg_ref, wu_ref, wd_ref, o_ref):
    i = pl.program_id(0)

    @pl.when(tv_ref[i] > 0)
    def _():
        o_ref[...] = _swiglu_rows(x_ref[...].astype(BF16), wg_ref, wu_ref, wd_ref)

    @pl.when(tv_ref[i] == 0)
    def _():
        o_ref[...] = jnp.zeros((MOE_TILE, D_MODEL), F32)


def _experts(tile_expert, tile_valid, xs, wg, wu, wd):
    row = pl.BlockSpec((MOE_TILE, D_MODEL), lambda i, te, tv: (i, 0))
    return pl.pallas_call(
        _experts_kernel,
        grid_spec=pltpu.PrefetchScalarGridSpec(
            num_scalar_prefetch=2,
            grid=(MOE_ROWS // MOE_TILE,),
            in_specs=[row,
                      pl.BlockSpec((None, D_MODEL, D_FF), lambda i, te, tv: (te[i], 0, 0)),
                      pl.BlockSpec((None, D_MODEL, D_FF), lambda i, te, tv: (te[i], 0, 0)),
                      pl.BlockSpec((None, D_FF, D_MODEL), lambda i, te, tv: (te[i], 0, 0))],
            out_specs=row),
        out_shape=jax.ShapeDtypeStruct((MOE_ROWS, D_MODEL), F32),
        compiler_params=pltpu.CompilerParams(
            dimension_semantics=("arbitrary",), vmem_limit_bytes=VMEM_BIG),
        name="moe_experts",
    )(tile_expert, tile_valid, xs, wg, wu, wd)


def _combine_kernel(dest_ref, x_ref, info_ref, g_ref, ys_ref, yp_ref, ysm_ref, ybuf, sem):
    i = pl.program_id(0)
    n = pl.num_programs(0)

    def start_tile(tile):
        slot = tile % 2
        base = tile * (2 * TOK_TILE)

        def start(r, c):
            for k in range(2):
                pltpu.make_async_copy(
                    ys_ref.at[pl.ds(dest_ref[base + 2 * r + k], 1), :],
                    ybuf.at[slot, k, pl.ds(r, 1), :], sem.at[slot]).start()
            return c

        lax.fori_loop(0, TOK_TILE, start, 0, unroll=8)

    @pl.when(i == 0)
    def _():
        start_tile(i)

    @pl.when(i + 1 < n)
    def _():
        start_tile(i + 1)

    slot = i % 2
    for k in range(2):
        _tile_rows_wait(ys_ref, ybuf.at[slot, k], sem.at[slot])
    info = info_ref[...]
    x = x_ref[...] + info[:, 2:3] * ybuf[slot, 0] + info[:, 3:4] * ybuf[slot, 1]
    y = _rms(x, g_ref[...])

    @pl.when(i < PROMPT_TILES)
    def _():
        yp_ref[...] = y

    @pl.when(i == PROMPT_TILES)
    def _():
        ysm_ref[...] = y[0:DEC_BATCH, :]


def _combine(dest, x, info, g, ys):
    last = PROMPT_TILES - 1
    return pl.pallas_call(
        _combine_kernel,
        grid_spec=pltpu.PrefetchScalarGridSpec(
            num_scalar_prefetch=1,
            grid=(N_TOK // TOK_TILE,),
            in_specs=[pl.BlockSpec((TOK_TILE, D_MODEL), lambda i, d: (i, 0)),
                      pl.BlockSpec((TOK_TILE, 128), lambda i, d: (i, 0)),
                      pl.BlockSpec((1, D_MODEL), lambda i, d: (0, 0)),
                      pl.BlockSpec(memory_space=pl.ANY)],
            out_specs=[pl.BlockSpec((TOK_TILE, D_MODEL), lambda i, d: (jnp.minimum(i, last), 0)),
                       pl.BlockSpec((DEC_BATCH, D_MODEL), lambda i, d: (0, 0))],
            scratch_shapes=[pltpu.VMEM((2, 2, TOK_TILE, D_MODEL), F32),
                            pltpu.SemaphoreType.DMA((2,))]),
        out_shape=[jax.ShapeDtypeStruct((N_PROMPT, D_MODEL), F32),
                   jax.ShapeDtypeStruct((DEC_BATCH, D_MODEL), F32)],
        compiler_params=pltpu.CompilerParams(dimension_semantics=("arbitrary",)),
        name="moe_combine",
    )(dest, x, info, g, ys)


def _moe_layer(x, yb, gb, wo, g2, router_w, wg, wu, wd, final_g):
    r_pad = jnp.zeros((D_MODEL, 128), F32).at[:, :N_EXPERTS].set(router_w)
    x, h, info, cnt = _router(x, yb, gb, wo, g2, r_pad)
    n_tiles = N_TOK // TOK_TILE
    counts = cnt[:, 0, :N_EXPERTS].astype(jnp.int32)
    total = jnp.sum(counts, axis=0)
    padded = ((total + MOE_TILE - 1) // MOE_TILE) * MOE_TILE
    ends = jnp.cumsum(padded)
    starts = ends - padded
    base = starts[None, :] + jnp.cumsum(counts, axis=0) - counts
    choice = info[:, 0:2].astype(jnp.int32).reshape(n_tiles, TOK_TILE, 2)
    rank = info[:, 4:6].astype(jnp.int32).reshape(n_tiles, TOK_TILE, 2)
    onehot = choice[..., None] == jnp.arange(N_EXPERTS, dtype=jnp.int32)
    dest = (jnp.sum(jnp.where(onehot, base[:, None, None, :], 0), axis=-1) + rank).reshape(-1)
    tile_start = jnp.arange(MOE_ROWS // MOE_TILE, dtype=jnp.int32) * MOE_TILE
    tile_expert = jnp.minimum(
        jnp.sum((tile_start[:, None] >= ends[None, :]).astype(jnp.int32), axis=1), N_EXPERTS - 1)
    tile_valid = (tile_start < ends[-1]).astype(jnp.int32)
    last_used = jnp.max(jnp.where(tile_valid > 0, tile_expert, 0))
    tile_expert = jnp.where(tile_valid > 0, tile_expert, last_used).astype(jnp.int32)

    gaps = padded - total
    gap_end = jnp.cumsum(gaps)
    j = jnp.arange(MOE_PAD_ROWS, dtype=jnp.int32)
    seg = jnp.sum((j[:, None] >= gap_end[None, :]).astype(jnp.int32), axis=1)
    seg_e = jnp.minimum(seg, N_EXPERTS - 1)
    in_gap = (starts + total)[seg_e] + j - (gap_end - gaps)[seg_e]
    pad_rows = jnp.where(seg < N_EXPERTS, in_gap, ends[-1] + j - gap_end[-1]).astype(jnp.int32)

    xs = _dispatch(dest, pad_rows, h)
    ys = _experts(tile_expert, tile_valid, xs, wg, wu, wd)
    return _combine(dest, x, info, final_g, ys)


def _rearranged_w_in(w_in):
    off_qkv = 2 * D_CONV
    off_z = off_qkv + CONV_DIM
    off_a = off_z + D_DELTA_V
    off_ga = off_a + 2 * N_HEADS
    off_gb = off_ga + D_MODEL
    w_in = w_in.astype(BF16)
    return jnp.concatenate(
        [w_in[:, :off_a], w_in[:, off_ga:off_gb], w_in[:, off_gb:], w_in[:, off_a:off_ga],
         jnp.zeros((D_MODEL, 128 - 2 * N_HEADS), BF16)], axis=1)


def _lane_vec(v, offset=0):
    return jnp.zeros((1, 128), F32).at[0, offset:offset + v.shape[0]].set(v)


def kernel(x_prompt, x_sample, state_delta, state_qkv_conv, state_dwconv, norm1_g, w_in, qkv_conv_w, a_log, dt_bias, o_norm_g, w_delta_out, dw_w, dw_b, ln_g, ln_b, w_conv_out, w_o, norm2_g, ffn_w_gate, ffn_w_up, ffn_w_down, router_w, exp_w_gate, exp_w_up, exp_w_down, final_norm_g):
    x = (x_prompt.reshape(N_PROMPT, D_MODEL), x_sample.reshape(DEC_BATCH, D_MODEL))
    s_p, q_p, d_p, s_s, q_s, d_s = [], [], [], [], [], []
    for l in range(DEPTH):
        glu, qkv, z, ga, gb, ab = _norm_proj(x, norm1_g[l][None, :], _rearranged_w_in(w_in[l]))
        dwb, lng, lnb = dw_b[l][None, :], ln_g[l][None, :], ln_b[l][None, :]
        wco = w_conv_out[l].astype(BF16)
        ya_s, dst_s = _conv_sample(glu, state_dwconv, l, dw_w[l], dwb, lng, lnb, wco)
        wo = w_o[l].astype(BF16)
        x, dst_p = _conv_prompt(x, glu, ga, ya_s, dw_w[l], dwb, lng, lnb, wco, wo)
        alog, dtb, ong = _lane_vec(a_log[l]), _lane_vec(dt_bias[l]), o_norm_g[l][None, :]
        wdo = w_delta_out[l].astype(BF16)
        yb_s, qst_s, sst_s = _gdn_sample(qkv, z, ab, state_qkv_conv[l].reshape(DEC_BATCH, -1),
                                         state_delta, l, qkv_conv_w[l], alog, dtb, ong, wdo)
        rp, qg, kt, dl, m, aq, qst_p = _gdn_pre(qkv, ab, qkv_conv_w[l], alog, dtb)
        yb, sst_p = _gdn_scan(rp, qg, kt, dl, _gdn_solve(m), aq, z, yb_s, ong, wdo)
        i = l // 2
        if l % 2 == 0:
            x = _ffn(x, yb, gb, wo, norm2_g[l][None, :], ffn_w_gate[i].astype(BF16),
                     ffn_w_up[i].astype(BF16), ffn_w_down[i].astype(BF16))
        else:
            y_prompt, y_sample = _moe_layer(
                x, yb, gb, wo, norm2_g[l][None, :], router_w[i], exp_w_gate[i].astype(BF16),
                exp_w_up[i].astype(BF16), exp_w_down[i].astype(BF16), final_norm_g[None, :])
        s_p.append(sst_p)
        q_p.append(qst_p)
        d_p.append(dst_p)
        s_s.append(sst_s)
        q_s.append(qst_s.reshape(DEC_BATCH, QK_CONV - 1, CONV_DIM))
        d_s.append(dst_s)
    return (y_prompt.reshape(BATCH, SEQ, D_MODEL), y_sample.reshape(DEC_BATCH, 1, D_MODEL), jnp.stack(s_p), jnp.stack(q_p), jnp.stack(d_p),
            jnp.stack(s_s), jnp.stack(q_s), jnp.stack(d_s))
```
